```python
import math
import jax, jax.numpy as jnp
from jax import lax
import numpy as np

D_MODEL = 1024
BATCH = 2
SEQ = 8192
DEPTH = 1
DEC_BATCH = 128
DEC_SEQ = 1
PAST_LEN = 2048
PAGE_SIZE = 128

D_MIX = D_MODEL
D_CONV = D_MIX // 2
D_ATTN = D_MIX - D_CONV
ATT_HEAD_DIM = 64
N_ATT_HEADS = D_ATTN // (2 * ATT_HEAD_DIM)
CONV_WIDTH = 31
N_BUCKETS = 32
MAX_DISTANCE = 128
Q_BLOCK = 128
PEER_HEADS = 8
PEER_KEYS = 128
PEER_EXPERTS = PEER_KEYS * PEER_KEYS
PEER_QUERY_DIM = 256
PEER_HALF = PEER_QUERY_DIM // 2
PEER_TOPK = 16
PEER_BLOCK = 128
D_IN_PROJ = 2 * D_CONV + 3 * D_ATTN
ALPHA = (2 * DEPTH) ** 0.25
BETA = (8 * DEPTH) ** -0.25
LN_EPS = 1e-5
NEG_INF = -1e30

kernel_name = 'hybrid_conformer_diffattn_peer_step'


def layer_norm(x, g=None, b=None):
    xf = x.astype(jnp.float32)
    mu = jnp.mean(xf, axis=-1, keepdims=True)
    var = jnp.mean(jnp.square(xf - mu), axis=-1, keepdims=True)
    y = ((xf - mu) * lax.rsqrt(var + LN_EPS)).astype(x.dtype)
    if g is not None:
        y = y * g + b
    return y


def adaln_params(c, w_ada, b_ada):
    mod = jax.nn.silu(c) @ w_ada + b_ada
    return jnp.split(mod[:, None, :], 6, axis=-1)


def t5_bucket(q_pos, k_pos):
    n = jnp.maximum(q_pos[:, None] - k_pos[None, :], 0)
    max_exact = N_BUCKETS // 2
    nf = jnp.maximum(n, 1).astype(jnp.float32)
    large = max_exact + (jnp.log(nf / max_exact) / math.log(MAX_DISTANCE / max_exact)
                         * (N_BUCKETS - max_exact)).astype(jnp.int32)
    large = jnp.minimum(large, N_BUCKETS - 1)
    return jnp.where(n < max_exact, n, large)


def diff_attention(q, k, v, q_pos, k_pos, lam, subln_g, lambda_init, rel_bias):
    d = ATT_HEAD_DIM
    qf = q.astype(jnp.float32) * (d ** -0.5)
    kf = k.astype(jnp.float32)
    bias = rel_bias[t5_bucket(q_pos, k_pos)].astype(jnp.float32).transpose(2, 0, 1)
    mask = k_pos[None, :] <= q_pos[:, None]

    def probs(qh, kh):
        s = jnp.einsum('bqhd,bkhd->bhqk', qh, kh) + bias
        return jax.nn.softmax(jnp.where(mask, s, NEG_INF), axis=-1)

    p = probs(qf[..., :d], kf[..., :d]) - lam * probs(qf[..., d:], kf[..., d:])
    o = jnp.einsum('bhqk,bkhd->bqhd', p, v.astype(jnp.float32))
    o = o * lax.rsqrt(jnp.mean(jnp.square(o), axis=-1, keepdims=True) + LN_EPS)
    o = o * subln_g.astype(jnp.float32) * (1.0 - lambda_init)
    return o.reshape(q.shape[0], q.shape[1], D_ATTN).astype(q.dtype)


def conv_module(glu, conv_past, conv_w, conv_b, ln_g, ln_b):
    full = jnp.concatenate([conv_past, glu], axis=1)
    y = lax.conv_general_dilated(full, conv_w[:, None, :], (1,), 'VALID',
                                 dimension_numbers=('NWC', 'WIO', 'NWC'),
                                 feature_group_count=D_CONV) + conv_b
    y = jax.nn.silu(layer_norm(y, ln_g, ln_b))
    return y, full[:, full.shape[1] - (CONV_WIDTH - 1):]


def peer_block(xb, w_query, sub_keys, u_tab, v_tab):
    tb = xb.shape[0]
    qh = (xb @ w_query).reshape(tb, PEER_HEADS, 2, PEER_HALF).astype(jnp.float32)
    s = jnp.einsum('thcd,hcnd->thcn', qh, sub_keys.astype(jnp.float32))
    v1, i1 = lax.top_k(s[:, :, 0], PEER_TOPK)
    v2, i2 = lax.top_k(s[:, :, 1], PEER_TOPK)
    cand = (v1[..., :, None] + v2[..., None, :]).reshape(tb, PEER_HEADS, PEER_TOPK * PEER_TOPK)
    cidx = (i1[..., :, None] * PEER_KEYS + i2[..., None, :]).reshape(tb, PEER_HEADS, PEER_TOPK * PEER_TOPK)
    fv, fi = lax.top_k(cand, PEER_TOPK)
    experts = jnp.take_along_axis(cidx, fi, axis=-1)
    g = jax.nn.softmax(fv, axis=-1).astype(xb.dtype)
    act = jax.nn.gelu(jnp.einsum('td,thkd->thk', xb, u_tab[experts]))
    return jnp.einsum('thk,thkd->td', g * act, v_tab[experts])


def peer(h, w_query, sub_keys, u_tab, v_tab):
    b, t, dm = h.shape
    n = b * t
    n_blk = -(-n // PEER_BLOCK)
    xt = jnp.pad(h.reshape(n, dm), ((0, n_blk * PEER_BLOCK - n), (0, 0))).reshape(n_blk, PEER_BLOCK, dm)
    y = lax.map(lambda xb: peer_block(xb, w_query, sub_keys, u_tab, v_tab), xt)
    return y.reshape(n_blk * PEER_BLOCK, dm)[:n].reshape(b, t, dm)


def hybrid_layer(x, c, conv_past, attend, p, lambda_init):
    sh1, sc1, g1, sh2, sc2, g2 = adaln_params(c, p['w_ada'], p['b_ada'])
    h = layer_norm(x) * (1 + sc1) + sh1
    z = h @ p['w_in'] + p['b_in']
    ga, gb, q, k, v = jnp.split(z, [D_CONV, 2 * D_CONV, 2 * D_CONV + D_ATTN, 2 * D_CONV + 2 * D_ATTN], axis=-1)
    b, t = x.shape[:2]
    glu = ga * jax.nn.sigmoid(gb)
    q = q.reshape(b, t, N_ATT_HEADS, 2 * ATT_HEAD_DIM)
    k = k.reshape(b, t, N_ATT_HEADS, 2 * ATT_HEAD_DIM)
    v = v.reshape(b, t, N_ATT_HEADS, 2 * ATT_HEAD_DIM)
    conv_y, conv_state = conv_module(glu, conv_past, p['conv_w'], p['conv_b'], p['conv_ln_g'], p['conv_ln_b'])
    f32 = jnp.float32
    lam = (jnp.exp(jnp.sum(p['lambda_q1'].astype(f32) * p['lambda_k1'].astype(f32)))
           - jnp.exp(jnp.sum(p['lambda_q2'].astype(f32) * p['lambda_k2'].astype(f32))) + lambda_init)
    attn_y = attend(q, k, v, lam, p['attn_subln_g'], lambda_init)
    mix = jnp.concatenate([conv_y, attn_y], axis=-1) @ p['w_out'] + p['b_out']
    x1 = layer_norm(ALPHA * x + g1 * mix, p['ln1_g'], p['ln1_b'])
    h2 = layer_norm(x1) * (1 + sc2) + sh2
    ff = peer(h2, p['peer_w_query'], p['peer_sub_keys'], p['peer_u'], p['peer_v'])
    x2 = layer_norm(ALPHA * x1 + g2 * ff, p['ln2_g'], p['ln2_b'])
    return x2, k, v, conv_state


def setup_inputs(seed: int = 0) -> dict:
    key = jax.random.key(seed)
    ks = iter(jax.random.split(key, 48))

    def nrm(shape, scale):
        return scale * jax.random.normal(next(ks), shape, jnp.float32)

    def gain(shape):
        return 1.0 + nrm(shape, 0.01)

    n_pages = PAST_LEN // PAGE_SIZE
    n_used = DEC_BATCH * n_pages
    n_pool = n_used + max(1, n_used // 4)
    kv_row = (N_ATT_HEADS, 2 * ATT_HEAD_DIM)
    page_table = jax.random.permutation(next(ks), n_pool)[:n_used].reshape(DEC_BATCH, n_pages).astype(jnp.int32)
    return {
        'x_prompt': nrm((BATCH, SEQ, D_MODEL), 1.0),
        'x_sample': nrm((DEC_BATCH, DEC_SEQ, D_MODEL), 1.0),
        'cache_k': nrm((DEPTH, n_pool, PAGE_SIZE) + kv_row, 1.0),
        'cache_v': nrm((DEPTH, n_pool, PAGE_SIZE) + kv_row, 1.0),
        'state_conv': nrm((DEPTH, DEC_BATCH, CONV_WIDTH - 1, D_CONV), 0.5),
        'page_table': page_table,
        'c_prompt': nrm((BATCH, D_MODEL), 1.0),
        'c_sample': nrm((DEC_BATCH, D_MODEL), 1.0),
        'w_ada': nrm((DEPTH, D_MODEL, 6 * D_MODEL), 0.5 * D_MODEL ** -0.5),
        'b_ada': nrm((DEPTH, 6 * D_MODEL), 0.01),
        'w_in': nrm((DEPTH, D_MODEL, D_IN_PROJ), D_MODEL ** -0.5),
        'b_in': nrm((DEPTH, D_IN_PROJ), 0.01),
        'conv_w': nrm((DEPTH, CONV_WIDTH, D_CONV), CONV_WIDTH ** -0.5),
        'conv_b': nrm((DEPTH, D_CONV), 0.01),
        'conv_ln_g': gain((DEPTH, D_CONV)),
        'conv_ln_b': nrm((DEPTH, D_CONV), 0.01),
        'lambda_q1': nrm((DEPTH, ATT_HEAD_DIM), 0.1),
        'lambda_k1': nrm((DEPTH, ATT_HEAD_DIM), 0.1),
        'lambda_q2': nrm((DEPTH, ATT_HEAD_DIM), 0.1),
        'lambda_k2': nrm((DEPTH, ATT_HEAD_DIM), 0.1),
        'attn_subln_g': gain((DEPTH, 2 * ATT_HEAD_DIM)),
        'rel_bias': nrm((N_BUCKETS, N_ATT_HEADS), 0.5),
        'w_out': nrm((DEPTH, D_MIX, D_MODEL), BETA * D_MIX ** -0.5),
        'b_out': nrm((DEPTH, D_MODEL), 0.01),
        'ln1_g': gain((DEPTH, D_MODEL)),
        'ln1_b': nrm((DEPTH, D_MODEL), 0.01),
        'peer_w_query': nrm((DEPTH, D_MODEL, PEER_HEADS * PEER_QUERY_DIM), D_MODEL ** -0.5),
        'peer_sub_keys': nrm((DEPTH, PEER_HEADS, 2, PEER_KEYS, PEER_HALF), PEER_HALF ** -0.5),
        'peer_u': nrm((DEPTH, PEER_EXPERTS, D_MODEL), D_MODEL ** -0.5),
        'peer_v': nrm((DEPTH, PEER_EXPERTS, D_MODEL), BETA),
        'ln2_g': gain((DEPTH, D_MODEL)),
        'ln2_b': nrm((DEPTH, D_MODEL), 0.01),
    }


def reference(x_prompt, x_sample, cache_k, cache_v, state_conv, page_table, c_prompt, c_sample,
              w_ada, b_ada, w_in, b_in, conv_w, conv_b, conv_ln_g, conv_ln_b,
              lambda_q1, lambda_k1, lambda_q2, lambda_k2, attn_subln_g, rel_bias,
              w_out, b_out, ln1_g, ln1_b, peer_w_query, peer_sub_keys, peer_u, peer_v, ln2_g, ln2_b):
    pos_prompt = jnp.arange(SEQ)
    q_pos_sample = PAST_LEN + jnp.arange(DEC_SEQ)
    k_pos_sample = jnp.arange(PAST_LEN + DEC_SEQ)
    n_blocks = SEQ // Q_BLOCK

    def attend_prompt(q, k, v, lam, g, li):
        qb = q.reshape(BATCH, n_blocks, Q_BLOCK, N_ATT_HEADS, 2 * ATT_HEAD_DIM).transpose(1, 0, 2, 3, 4)

        def one(args):
            qi, start = args
            return diff_attention(qi, k, v, start + jnp.arange(Q_BLOCK), pos_prompt, lam, g, li, rel_bias)

        o = lax.map(one, (qb, jnp.arange(n_blocks) * Q_BLOCK))
        return o.transpose(1, 0, 2, 3).reshape(BATCH, SEQ, D_ATTN)

    xp, xs = x_prompt, x_sample
    kp_l, vp_l, cp_l, ks_l, vs_l, cs_l = [], [], [], [], [], []
    for l in range(DEPTH):
        p = dict(w_ada=w_ada[l], b_ada=b_ada[l], w_in=w_in[l], b_in=b_in[l],
                 conv_w=conv_w[l], conv_b=conv_b[l], conv_ln_g=conv_ln_g[l], conv_ln_b=conv_ln_b[l],
                 lambda_q1=lambda_q1[l], lambda_k1=lambda_k1[l], lambda_q2=lambda_q2[l], lambda_k2=lambda_k2[l],
                 attn_subln_g=attn_subln_g[l], w_out=w_out[l], b_out=b_out[l], ln1_g=ln1_g[l], ln1_b=ln1_b[l],
                 peer_w_query=peer_w_query[l], peer_sub_keys=peer_sub_keys[l], peer_u=peer_u[l], peer_v=peer_v[l],
                 ln2_g=ln2_g[l], ln2_b=ln2_b[l])
        lambda_init = 0.8 - 0.6 * math.exp(-0.3 * l)

        conv_zero = jnp.zeros((BATCH, CONV_WIDTH - 1, D_CONV), x_prompt.dtype)
        xp, kp, vp, cp = hybrid_layer(xp, c_prompt, conv_zero, attend_prompt, p, lambda_init)

        def attend_sample(q, k, v, lam, g, li, ck=cache_k[l], cv=cache_v[l]):
            past_k = ck[page_table].reshape(DEC_BATCH, PAST_LEN, N_ATT_HEADS, 2 * ATT_HEAD_DIM)
            past_v = cv[page_table].reshape(DEC_BATCH, PAST_LEN, N_ATT_HEADS, 2 * ATT_HEAD_DIM)
            k_all = jnp.concatenate([past_k, k], axis=1)
            v_all = jnp.concatenate([past_v, v], axis=1)
            return diff_attention(q, k_all, v_all, q_pos_sample, k_pos_sample, lam, g, li, rel_bias)

        xs, ksn, vsn, csn = hybrid_layer(xs, c_sample, state_conv[l], attend_sample, p, lambda_init)
        kp_l.append(kp); vp_l.append(vp); cp_l.append(cp)
        ks_l.append(ksn); vs_l.append(vsn); cs_l.append(csn)

    k_prompt = jnp.stack(kp_l)
    v_prompt = jnp.stack(vp_l)
    conv_prompt = jnp.stack(cp_l)
    k_sample = jnp.stack(ks_l)
    v_sample = jnp.stack(vs_l)
    conv_sample = jnp.stack(cs_l)
    return (xp, xs, k_prompt, v_prompt, conv_prompt, k_sample, v_sample, conv_sample)
```

```python
import functools
import math

import jax
import jax.numpy as jnp
from jax import lax
from jax.experimental import pallas as pl
from jax.experimental.pallas import tpu as pltpu

F32 = jnp.float32
BF16 = jnp.bfloat16

LN_EPS = 1e-5
NEG_INF = -1e30
ATT_HEAD_DIM = 64
N_BUCKETS = 32
MAX_DISTANCE = 128
PEER_TOPK = 16
LANES = 128
VMEM_LIMIT_BYTES = 56 * 1024 * 1024
TOKEN_TILE = 512
PEER_TOKEN_TILE = 512
PEER_EXPERT_TILE = 1024
ROUTE_TOKEN_TILE = 256


def _params(*sem):
    return pltpu.CompilerParams(dimension_semantics=sem, vmem_limit_bytes=VMEM_LIMIT_BYTES)


def _ln(x):
    mu = jnp.mean(x, axis=-1, keepdims=True)
    xc = x - mu
    var = jnp.mean(xc * xc, axis=-1, keepdims=True)
    return xc * lax.rsqrt(var + LN_EPS)


def _full(shape):
    return pl.BlockSpec(shape, lambda *_: (0,) * len(shape))


def _ada_kernel(c_ref, w_ref, b_ref, o_ref):
    c = c_ref[...]
    s = c * jax.nn.sigmoid(c)
    o_ref[0] = jnp.dot(s.astype(BF16), w_ref[...].astype(BF16), preferred_element_type=F32) + b_ref[0]


def _ada(c_all, w_ada, b_ada):
    nc, d = c_all.shape
    return pl.pallas_call(
        _ada_kernel,
        grid=(6,),
        in_specs=[_full((nc, d)),
                  pl.BlockSpec((d, d), lambda k: (0, k)),
                  pl.BlockSpec((1, 1, d), lambda k: (k, 0, 0))],
        out_specs=pl.BlockSpec((1, nc, d), lambda k: (k, 0, 0)),
        out_shape=jax.ShapeDtypeStruct((6, nc, d), F32),
        compiler_params=_params("arbitrary"),
        name="ada",
    )(c_all, w_ada, b_ada.reshape(6, 1, d))


def _inproj_kernel(x_ref, mod_ref, w_ref, b_ref, *outs, per_token_mod, d_conv, d_attn, transposed):
    x = x_ref[...]
    if per_token_mod:
        sh1, sc1 = mod_ref[0], mod_ref[1]
    else:
        sh1, sc1 = mod_ref[0, 0], mod_ref[1, 0]
    h = _ln(x) * (1.0 + sc1) + sh1
    z = jnp.dot(h.astype(BF16), w_ref[...], preferred_element_type=F32) + b_ref[...]
    ga = z[:, :d_conv]
    gb = z[:, d_conv:2 * d_conv]
    o = 2 * d_conv
    q = z[:, o:o + d_attn]
    k = z[:, o + d_attn:o + 2 * d_attn]
    v = z[:, o + 2 * d_attn:o + 3 * d_attn]
    glu = ga * jax.nn.sigmoid(gb)
    if transposed:
        glu_ref, k_ref, v_ref, kb_ref, qt_ref, vt_ref = outs
        kb_ref[...] = k.astype(BF16)
        qt_ref[0] = (q * (ATT_HEAD_DIM ** -0.5)).T.astype(BF16)
        vt_ref[0] = v.T.astype(BF16)
    else:
        glu_ref, k_ref, v_ref, q_ref = outs
        q_ref[...] = q
    glu_ref[...] = glu
    k_ref[...] = k
    v_ref[...] = v


def _inproj(x, mod, w_in, b_in, *, tm, rows_per_mod, d_conv, d_attn):
    t, d = x.shape
    n_in = w_in.shape[1]
    nt = t // tm
    per_token = rows_per_mod is None
    if per_token:
        mod_spec = pl.BlockSpec((6, tm, d), lambda i: (0, i, 0))
    else:
        mod_spec = pl.BlockSpec((6, 1, 1, d), lambda i: (0, (i * tm) // rows_per_mod, 0, 0))
    row = lambda n, dt: (jax.ShapeDtypeStruct((t, n), dt), pl.BlockSpec((tm, n), lambda i: (i, 0)))
    outs = [row(d_conv, F32), row(d_attn, F32), row(d_attn, F32)]
    if per_token:
        outs.append(row(d_attn, F32))
    else:
        outs.append(row(d_attn, BF16))
        tr = (jax.ShapeDtypeStruct((nt, d_attn, tm), BF16), pl.BlockSpec((1, d_attn, tm), lambda i: (i, 0, 0)))
        outs += [tr, tr]
    return pl.pallas_call(
        functools.partial(_inproj_kernel, per_token_mod=per_token, d_conv=d_conv, d_attn=d_attn,
                          transposed=not per_token),
        grid=(nt,),
        in_specs=[pl.BlockSpec((tm, d), lambda i: (i, 0)), mod_spec, _full((d, n_in)), _full((1, n_in))],
        out_specs=[o[1] for o in outs],
        out_shape=[o[0] for o in outs],
        compiler_params=_params("arbitrary"),
        name="inproj_sample" if per_token else "inproj_prompt",
    )(x, mod, w_in, b_in)


CONV_PAD = 32


def _conv_post(acc, cb_ref, g_ref, b_ref):
    y = _ln(acc + cb_ref[...]) * g_ref[...] + b_ref[...]
    return y * jax.nn.sigmoid(y)


def _conv_prompt_kernel(glu_ref, cw_ref, cb_ref, g_ref, b_ref, y_ref, full_ref, *, ts, width, chunk):
    s = pl.program_id(1)
    hist = width - 1

    @pl.when(s == 0)
    def _():
        full_ref[0:CONV_PAD, :] = jnp.zeros((CONV_PAD, full_ref.shape[1]), F32)

    @pl.when(s > 0)
    def _():
        full_ref[0:CONV_PAD, :] = full_ref[ts:ts + CONV_PAD, :]

    full_ref[CONV_PAD:CONV_PAD + ts, :] = glu_ref[...]
    base = CONV_PAD - hist
    for c in range(ts // chunk):
        r0 = c * chunk
        acc = full_ref[base + r0:base + r0 + chunk, :] * cw_ref[0:1, :]
        for w in range(1, width):
            acc = acc + full_ref[base + r0 + w:base + r0 + w + chunk, :] * cw_ref[w:w + 1, :]
        y_ref[r0:r0 + chunk, :] = _conv_post(acc, cb_ref, g_ref, b_ref).astype(y_ref.dtype)


def _conv_prompt(glu, conv_w, conv_b, ln_g, ln_b, *, batch, seq, ts):
    t, dc = glu.shape
    width = conv_w.shape[0]
    ns = seq // ts
    vec = lambda a: a.reshape(1, dc)
    return pl.pallas_call(
        functools.partial(_conv_prompt_kernel, ts=ts, width=width, chunk=min(64, ts)),
        grid=(batch, ns),
        in_specs=[pl.BlockSpec((ts, dc), lambda b, s: (b * ns + s, 0)),
                  _full((width, dc)), _full((1, dc)), _full((1, dc)), _full((1, dc))],
        out_specs=pl.BlockSpec((ts, dc), lambda b, s: (b * ns + s, 0)),
        out_shape=jax.ShapeDtypeStruct((t, dc), BF16),
        scratch_shapes=[pltpu.VMEM((CONV_PAD + ts, dc), F32)],
        compiler_params=_params("arbitrary", "arbitrary"),
        name="conv_prompt",
    )(glu, conv_w, vec(conv_b), vec(ln_g), vec(ln_b))


def _conv_sample_kernel(st_ref, glu_ref, cw_ref, cb_ref, g_ref, b_ref, y_ref, *, width):
    hist = width - 1
    acc = glu_ref[...] * cw_ref[hist:hist + 1, :]
    for w in range(hist):
        acc = acc + st_ref[w] * cw_ref[w:w + 1, :]
    y_ref[...] = _conv_post(acc, cb_ref, g_ref, b_ref).astype(y_ref.dtype)


def _conv_sample(state_t, glu, conv_w, conv_b, ln_g, ln_b):
    hist, nb, dc = state_t.shape
    width = conv_w.shape[0]
    vec = lambda a: a.reshape(1, dc)
    return pl.pallas_call(
        functools.partial(_conv_sample_kernel, width=width),
        grid=(1,),
        in_specs=[_full((hist, nb, dc)), _full((nb, dc)), _full((width, dc)),
                  _full((1, dc)), _full((1, dc)), _full((1, dc))],
        out_specs=_full((nb, dc)),
        out_shape=jax.ShapeDtypeStruct((nb, dc), BF16),
        compiler_params=_params("arbitrary"),
        name="conv_sample",
    )(state_t, glu, conv_w, vec(conv_b), vec(ln_g), vec(ln_b))


def _bucket(n):
    max_exact = N_BUCKETS // 2
    nf = jnp.maximum(n, 1).astype(F32)
    large = max_exact + (jnp.log(nf / max_exact) / math.log(MAX_DISTANCE / max_exact)
                         * (N_BUCKETS - max_exact)).astype(jnp.int32)
    large = jnp.minimum(large, N_BUCKETS - 1)
    return jnp.where(n < max_exact, n, large)


def _bias_of(n, rb_ref, h):
    bucket = _bucket(n)
    far = jnp.full(n.shape, rb_ref[N_BUCKETS - 1, h], F32)
    out = far
    for j in range(N_BUCKETS - 1):
        out = jnp.where(bucket == j, rb_ref[j, h], out)
    return out - far


def _bias_kernel(rb_ref, o_ref, *, ts):
    h = pl.program_id(0)
    d = pl.program_id(1)
    ik = lax.broadcasted_iota(jnp.int32, (ts, ts), 0)
    jq = lax.broadcasted_iota(jnp.int32, (ts, ts), 1)
    n = jq - ik + ts * (1 - d)
    b = _bias_of(jnp.maximum(n, 0), rb_ref, h)
    o_ref[0, 0] = jnp.where(n >= 0, b, NEG_INF)


def _bias_tiles(rel_bias, *, heads, ts):
    return pl.pallas_call(
        functools.partial(_bias_kernel, ts=ts),
        grid=(heads, 2),
        in_specs=[pl.BlockSpec(memory_space=pltpu.SMEM)],
        out_specs=pl.BlockSpec((1, 1, ts, ts), lambda h, d: (h, d, 0, 0)),
        out_shape=jax.ShapeDtypeStruct((heads, 2, ts, ts), F32),
        compiler_params=_params("arbitrary", "arbitrary"),
        name="bias_tiles",
    )(rel_bias)


def _lambda(lq1, lk1, lq2, lk2, lambda_init):
    s1 = jnp.sum(lq1[...] * lk1[...], axis=-1, keepdims=True)
    s2 = jnp.sum(lq2[...] * lk2[...], axis=-1, keepdims=True)
    return jnp.exp(s1) - jnp.exp(s2) + lambda_init


def _attn_kernel(qt_ref, k_ref, vt_ref, bias_ref, g_ref, lq1, lk1, lq2, lk2, o_ref,
                 qs_ref, m_ref, l_ref, acc_ref, *, ts, lambda_init):
    qi = pl.program_id(2)
    d = ATT_HEAD_DIM
    qt = qt_ref[0]
    row = lax.broadcasted_iota(jnp.int32, qt.shape, 0)
    zero = jnp.zeros_like(qt)
    qs_ref[:, :ts] = jnp.where(row < d, qt, zero)
    qs_ref[:, ts:] = jnp.where(row >= d, qt, zero)
    m_ref[...] = jnp.full(m_ref.shape, NEG_INF, F32)
    l_ref[...] = jnp.zeros(l_ref.shape, F32)
    acc_ref[...] = jnp.zeros(acc_ref.shape, F32)

    def block(ki, bias):
        kblk = k_ref[pl.ds(pl.multiple_of(ki * ts, ts), ts), :]
        s = jnp.dot(kblk, qs_ref[...], preferred_element_type=F32)
        if bias is not None:
            s = s + jnp.concatenate([bias, bias], axis=1)
        m_prev = m_ref[...]
        m_new = jnp.maximum(m_prev, jnp.max(s, axis=0, keepdims=True))
        alpha = jnp.exp(m_prev - m_new)
        p = jnp.exp(s - m_new)
        l_ref[...] = alpha * l_ref[...] + jnp.sum(p, axis=0, keepdims=True)
        acc_ref[...] = alpha * acc_ref[...] + jnp.dot(vt_ref[ki], p.astype(BF16), preferred_element_type=F32)
        m_ref[...] = m_new

    def far(ki, carry):
        block(ki, None)
        return carry

    lax.fori_loop(0, jnp.maximum(qi - 1, 0), far, 0)

    @pl.when(qi > 0)
    def _():
        block(qi - 1, bias_ref[0, 0])

    block(qi, bias_ref[0, 1])

    lam = _lambda(lq1, lk1, lq2, lk2, lambda_init)
    inv_l = 1.0 / l_ref[...]
    acc = acc_ref[...]
    o = acc[:, :ts] * inv_l[:, :ts] - lam * (acc[:, ts:] * inv_l[:, ts:])
    o = o * lax.rsqrt(jnp.mean(o * o, axis=0, keepdims=True) + LN_EPS)
    o = o * g_ref[...] * (1.0 - lambda_init)
    o_ref[...] = o.T.astype(o_ref.dtype)


def _attn_prompt(qt, kb, vt, bias, subln_g, lams, *, batch, seq, heads, ts, lambda_init):
    t, d_attn = kb.shape
    hd = 2 * ATT_HEAD_DIM
    nq = seq // ts
    lam_spec = _full((1, ATT_HEAD_DIM))
    return pl.pallas_call(
        functools.partial(_attn_kernel, ts=ts, lambda_init=lambda_init),
        grid=(batch, heads, nq),
        in_specs=[pl.BlockSpec((1, hd, ts), lambda b, h, q: (b * nq + q, h, 0)),
                  pl.BlockSpec((seq, hd), lambda b, h, q: (b, h)),
                  pl.BlockSpec((nq, hd, ts), lambda b, h, q: (b, h, 0)),
                  pl.BlockSpec((1, 2, ts, ts), lambda b, h, q: (h, 0, 0, 0)),
                  _full((hd, 1)), lam_spec, lam_spec, lam_spec, lam_spec],
        out_specs=pl.BlockSpec((ts, hd), lambda b, h, q: (b * nq + q, h)),
        out_shape=jax.ShapeDtypeStruct((t, d_attn), BF16),
        scratch_shapes=[pltpu.VMEM((hd, 2 * ts), BF16), pltpu.VMEM((1, 2 * ts), F32),
                        pltpu.VMEM((1, 2 * ts), F32), pltpu.VMEM((hd, 2 * ts), F32)],
        compiler_params=_params("arbitrary", "arbitrary", "arbitrary"),
        name="attn_prompt",
    )(qt, kb, vt, bias, subln_g.reshape(hd, 1), *lams)


def _decode_kernel(pt_ref, rb_ref, q_ref, kn_ref, vn_ref, g_ref, lq1, lk1, lq2, lk2, *rest,
                   n_pages, page, heads, lambda_init):
    k_refs = rest[:n_pages]
    v_refs = rest[n_pages:2 * n_pages]
    o_ref = rest[2 * n_pages]
    s_ref, p_ref = rest[2 * n_pages + 1:]
    d = ATT_HEAD_DIM
    hd = 2 * d
    width = heads * hd
    past = n_pages * page
    half2 = LANES // 2

    q = q_ref[0] * (d ** -0.5)
    r = lax.broadcasted_iota(jnp.int32, (LANES, width), 0)
    c = lax.broadcasted_iota(jnp.int32, (LANES, width), 1)
    c_head = lax.shift_right_logical(c, int(math.log2(hd)))
    c_first = (c & (hd - 1)) < d
    r_head = jnp.where(r < half2, r, r - half2)
    r_live = (r < heads) | ((r >= half2) & (r < half2 + heads))
    own = (c_head == r_head) & r_live & (c_first == (r < half2))
    qbd = jnp.where(own, jnp.broadcast_to(q, (LANES, width)), 0.0).T.astype(BF16)

    for j in range(n_pages):
        s_ref[j * page:(j + 1) * page, :] = jnp.dot(k_refs[j][0].astype(BF16), qbd, preferred_element_type=F32)
    kn = jnp.broadcast_to(kn_ref[0], (8, width)).astype(BF16)
    s_new = jnp.dot(kn, qbd, preferred_element_type=F32)
    row8 = lax.broadcasted_iota(jnp.int32, (8, LANES), 0)

    lane = lax.broadcasted_iota(jnp.int32, (page, LANES), 1)
    krow = lax.broadcasted_iota(jnp.int32, (page, LANES), 0)
    head_of_lane = jnp.where(lane < half2, lane, lane - half2)
    n_last = page - krow
    bias_last = jnp.zeros((page, LANES), F32)
    bias_new = jnp.zeros((8, LANES), F32)
    lane8 = lax.broadcasted_iota(jnp.int32, (8, LANES), 1)
    head8 = jnp.where(lane8 < half2, lane8, lane8 - half2)
    for h in range(heads):
        bias_last = jnp.where(head_of_lane == h, _bias_of(n_last, rb_ref, h), bias_last)
        bias_new = jnp.where(head8 == h, _bias_of(jnp.zeros((8, LANES), jnp.int32), rb_ref, h), bias_new)
    s_ref[past - page:past, :] = s_ref[past - page:past, :] + bias_last
    s_ref[past:past + 8, :] = jnp.where(row8 == 0, s_new + bias_new, NEG_INF)

    s = s_ref[...]
    m = jnp.max(s, axis=0, keepdims=True)
    p = jnp.exp(s - m)
    l = jnp.sum(p, axis=0, keepdims=True)
    lam = _lambda(lq1, lk1, lq2, lk2, lambda_init)
    lane1 = lax.broadcasted_iota(jnp.int32, (1, LANES), 1)
    coef = jnp.where(lane1 < half2, 1.0 / l, -lam / l)
    p_ref[...] = (p * coef).astype(BF16)

    expand = jnp.where((c_head == r_head) & r_live, 1.0, 0.0).astype(BF16)

    acc = jnp.zeros((page, width), F32)
    for j in range(n_pages):
        wrep = jnp.dot(p_ref[j * page:(j + 1) * page, :], expand, preferred_element_type=F32)
        acc = acc + wrep * v_refs[j][0]
    w_new = jnp.dot(p_ref[past:past + 8, :], expand, preferred_element_type=F32)
    o = jnp.sum(acc, axis=0, keepdims=True) + w_new[0:1, :] * vn_ref[0]

    col_head = lax.shift_right_logical(lax.broadcasted_iota(jnp.int32, (1, width), 1), int(math.log2(hd)))
    ms = jnp.zeros((1, width), F32)
    for h in range(heads):
        in_h = col_head == h
        ms = jnp.where(in_h, jnp.sum(jnp.where(in_h, o * o, 0.0), axis=-1, keepdims=True) / hd, ms)
    o = o * lax.rsqrt(ms + LN_EPS) * g_ref[...] * (1.0 - lambda_init)
    o_ref[0] = o.astype(o_ref.dtype)


def _attn_sample(page_table, rel_bias, q, k_new, v_new, cache_k, cache_v, subln_g, lams, *, heads, lambda_init):
    nb, n_pages = page_table.shape
    n_pool, page, width = cache_k.shape
    hd = 2 * ATT_HEAD_DIM
    tok = lambda a: a.reshape(nb, 1, width)
    tok_spec = pl.BlockSpec((1, 1, width), lambda b, pt: (b, 0, 0))
    lam_spec = pl.BlockSpec((1, ATT_HEAD_DIM), lambda b, pt: (0, 0))
    page_specs = [pl.BlockSpec((1, page, width), lambda b, pt, j=j: (pt[b, j], 0, 0)) for j in range(n_pages)]
    grid_spec = pltpu.PrefetchScalarGridSpec(
        num_scalar_prefetch=1,
        grid=(nb,),
        in_specs=[pl.BlockSpec(memory_space=pltpu.SMEM), tok_spec, tok_spec, tok_spec,
                  pl.BlockSpec((1, width), lambda b, pt: (0, 0)),
                  lam_spec, lam_spec, lam_spec, lam_spec] + page_specs + page_specs,
        out_specs=tok_spec,
        scratch_shapes=[pltpu.VMEM((n_pages * page + 8, LANES), F32),
                        pltpu.VMEM((n_pages * page + 8, LANES), BF16)],
    )
    out = pl.pallas_call(
        functools.partial(_decode_kernel, n_pages=n_pages, page=page, heads=heads, lambda_init=lambda_init),
        grid_spec=grid_spec,
        out_shape=jax.ShapeDtypeStruct((nb, 1, width), BF16),
        compiler_params=_params("arbitrary"),
        name="attn_sample",
    )(page_table, rel_bias, tok(q), tok(k_new), tok(v_new), jnp.tile(subln_g, heads).reshape(1, width), *lams,
      *([cache_k] * n_pages), *([cache_v] * n_pages))
    return out.reshape(nb, width)


def _mix_kernel(x_ref, cy_ref, ay_ref, mod_ref, wo_ref, bo_ref, g1_ref, b1_ref, wq_ref, sk_ref,
                x1_ref, h2t_ref, st_ref, *, per_token_mod, alpha, d_conv):
    if per_token_mod:
        gate1, sh2, sc2 = mod_ref[2], mod_ref[3], mod_ref[4]
    else:
        gate1, sh2, sc2 = mod_ref[2, 0], mod_ref[3, 0], mod_ref[4, 0]
    mix = (jnp.dot(cy_ref[...], wo_ref[:d_conv, :], preferred_element_type=F32)
           + jnp.dot(ay_ref[...], wo_ref[d_conv:, :], preferred_element_type=F32) + bo_ref[...])
    x1 = _ln(alpha * x_ref[...] + gate1 * mix) * g1_ref[...] + b1_ref[...]
    x1_ref[...] = x1
    h2 = _ln(x1) * (1.0 + sc2) + sh2
    h2b = h2.astype(BF16)
    h2t_ref[...] = h2.T.astype(BF16)
    qh = jnp.dot(h2b, wq_ref[...], preferred_element_type=F32).astype(BF16)
    nk = sk_ref.shape[2]
    for hc in range(sk_ref.shape[0]):
        st_ref[hc] = lax.dot_general(sk_ref[hc], qh[:, hc * nk:(hc + 1) * nk], (((1,), (1,)), ((), ())),
                                     preferred_element_type=F32)


def _mix(x, cy, ay, mod, w_out, b_out, ln_g, ln_b, w_query, sub_keys, *, tm, rows_per_mod, alpha):
    t, d = x.shape
    d_conv = cy.shape[1]
    d_attn = ay.shape[1]
    n_hc, n_keys, half = sub_keys.shape
    per_token = rows_per_mod is None
    if per_token:
        mod_spec = pl.BlockSpec((6, tm, d), lambda i: (0, i, 0))
    else:
        mod_spec = pl.BlockSpec((6, 1, 1, d), lambda i: (0, (i * tm) // rows_per_mod, 0, 0))
    vec = lambda a: a.reshape(1, d)
    return pl.pallas_call(
        functools.partial(_mix_kernel, per_token_mod=per_token, alpha=alpha, d_conv=d_conv),
        grid=(t // tm,),
        in_specs=[pl.BlockSpec((tm, d), lambda i: (i, 0)),
                  pl.BlockSpec((tm, d_conv), lambda i: (i, 0)),
                  pl.BlockSpec((tm, d_attn), lambda i: (i, 0)),
                  mod_spec, _full(w_out.shape), _full((1, d)), _full((1, d)), _full((1, d)),
                  _full(w_query.shape), _full(sub_keys.shape)],
        out_specs=[pl.BlockSpec((tm, d), lambda i: (i, 0)),
                   pl.BlockSpec((d, tm), lambda i: (0, i)),
                   pl.BlockSpec((n_hc, n_keys, tm), lambda i: (0, 0, i))],
        out_shape=[jax.ShapeDtypeStruct((t, d), F32),
                   jax.ShapeDtypeStruct((d, t), BF16),
                   jax.ShapeDtypeStruct((n_hc, n_keys, t), F32)],
        compiler_params=_params("arbitrary"),
        name="mix_sample" if per_token else "mix_prompt",
    )(x, cy, ay, mod, w_out, vec(b_out), vec(ln_g), vec(ln_b), w_query, sub_keys)


def _top_ranks(s):
    work = s
    rank = jnp.full(s.shape, float(PEER_TOPK), F32)
    vals = []
    for r in range(PEER_TOPK):
        m = jnp.max(work, axis=0, keepdims=True)
        hit = work == m
        rank = jnp.where(hit, float(r), rank)
        work = jnp.where(hit, -jnp.inf, work)
        vals.append(m)
    return rank, vals


def _route_kernel(s_ref, r2_ref, e2_ref, n1_ref, e1_ref):
    k = PEER_TOPK
    s1 = s_ref[0]
    s2 = s_ref[1]
    rank1, v1 = _top_ranks(s1)
    rank2, v2 = _top_ranks(s2)
    rowk = lax.broadcasted_iota(jnp.int32, (k,) + s1.shape[1:], 0)

    def stack(vals):
        out = jnp.zeros(rowk.shape, F32)
        for r in range(k):
            out = jnp.where(rowk == r, vals[r], out)
        return out

    v1m = stack(v1)
    v2m = stack(v2)
    row8 = lax.broadcasted_iota(jnp.int32, (8,) + s1.shape[1:], 0)
    cands = [v1[0] + v2m]
    for a in range(1, 8):
        cands.append(jnp.where(row8 < k // (a + 1), v1[a] + v2m[0:8], -jnp.inf))
    cands.append(v1m[8:k] + v2[0])
    work = jnp.concatenate(cands, axis=0)
    thr = None
    for _ in range(k):
        thr = jnp.max(work, axis=0, keepdims=True)
        work = jnp.where(work == thr, -jnp.inf, work)
    e2top = jnp.exp(v2m - v2[0])
    z = jnp.zeros_like(thr)
    n1 = jnp.zeros(s1.shape, F32)
    for a in range(k):
        sel = (v1[a] + v2m) >= thr
        cnt = jnp.sum(jnp.where(sel, 1.0, 0.0), axis=0, keepdims=True)
        z = z + jnp.exp(v1[a] - v1[0]) * jnp.sum(jnp.where(sel, e2top, 0.0), axis=0, keepdims=True)
        n1 = jnp.where(rank1 == float(a), cnt, n1)
    r2_ref[0] = rank2
    n1_ref[0] = n1
    e1_ref[0] = jnp.where(rank1 < float(k), jnp.exp(s1 - v1[0]) / z, 0.0)
    e2_ref[0] = jnp.where(rank2 < float(k), jnp.exp(s2 - v2[0]), 0.0)


def _route(st, *, tl):
    n_hc, n_keys, t = st.shape
    heads = n_hc // 2
    out = jax.ShapeDtypeStruct((heads, n_keys, t), F32)
    spec = pl.BlockSpec((1, n_keys, tl), lambda i, h: (h, 0, i))
    return pl.pallas_call(
        _route_kernel,
        grid=(t // tl, heads),
        in_specs=[pl.BlockSpec((2, n_keys, tl), lambda i, h: (h, 0, i))],
        out_specs=[spec] * 4,
        out_shape=[out] * 4,
        compiler_params=_params("arbitrary", "arbitrary"),
        name="route",
    )(st)


def _peer_kernel(h2t_ref, u_ref, vt_ref, r2_ref, e2_ref, n1_ref, e1_ref, x1_ref, mod_ref, g_ref, b_ref, y_ref,
                 at_ref, zt_ref, acc_ref, *, per_token_mod, alpha, n_keys):
    j = pl.program_id(1)
    groups = u_ref.shape[0] // n_keys
    heads = r2_ref.shape[0]

    @pl.when(j == 0)
    def _():
        acc_ref[...] = jnp.zeros(acc_ref.shape, F32)

    at_ref[...] = jnp.dot(u_ref[...], h2t_ref[...], preferred_element_type=F32)

    def group(g, carry):
        i1 = j * groups + g
        rows = pl.ds(pl.multiple_of(g * n_keys, n_keys), n_keys)
        act = jax.nn.gelu(at_ref[rows, :])
        w = jnp.zeros(act.shape, F32)
        for h in range(heads):
            n1row = n1_ref[h, pl.ds(i1, 1), :]
            e1row = e1_ref[h, pl.ds(i1, 1), :]
            w = w + jnp.where(r2_ref[h] < n1row, e2_ref[h] * e1row, 0.0)
        zt_ref[rows, :] = (w * act).astype(BF16)
        return carry

    lax.fori_loop(0, groups, group, 0)
    acc_ref[...] += jnp.dot(vt_ref[...], zt_ref[...], preferred_element_type=F32)

    @pl.when(j == pl.num_programs(1) - 1)
    def _():
        gate2 = mod_ref[5] if per_token_mod else mod_ref[5, 0]
        ff = acc_ref[...].T
        y_ref[...] = _ln(alpha * x1_ref[...] + gate2 * ff) * g_ref[...] + b_ref[...]


def _peer(h2t, u_b, vt_b, r2, e2, n1, e1, x1, mod, ln_g, ln_b, *, tq, ek, rows_per_mod, alpha):
    d, t = h2t.shape
    ne = u_b.shape[0]
    heads, n_keys, _ = r2.shape
    per_token = rows_per_mod is None
    if per_token:
        mod_spec = pl.BlockSpec((6, tq, d), lambda i, j: (0, i, 0))
    else:
        mod_spec = pl.BlockSpec((6, 1, 1, d), lambda i, j: (0, (i * tq) // rows_per_mod, 0, 0))
    tab = pl.BlockSpec((heads, n_keys, tq), lambda i, j: (0, 0, i))
    vec = lambda a: a.reshape(1, d)
    return pl.pallas_call(
        functools.partial(_peer_kernel, per_token_mod=per_token, alpha=alpha, n_keys=n_keys),
        grid=(t // tq, ne // ek),
        in_specs=[pl.BlockSpec((d, tq), lambda i, j: (0, i)),
                  pl.BlockSpec((ek, d), lambda i, j: (j, 0)),
                  pl.BlockSpec((d, ek), lambda i, j: (0, j)),
                  tab, tab, tab, tab,
                  pl.BlockSpec((tq, d), lambda i, j: (i, 0)),
                  mod_spec, _full((1, d)), _full((1, d))],
        out_specs=pl.BlockSpec((tq, d), lambda i, j: (i, 0)),
        out_shape=jax.ShapeDtypeStruct((t, d), F32),
        scratch_shapes=[pltpu.VMEM((ek, tq), F32), pltpu.VMEM((ek, tq), BF16), pltpu.VMEM((d, tq), F32)],
        compiler_params=_params("arbitrary", "arbitrary"),
        name="peer_sample" if per_token else "peer_prompt",
    )(h2t, u_b, vt_b, r2, e2, n1, e1, x1, mod, vec(ln_g), vec(ln_b))


def _transpose_kernel(x_ref, o_ref):
    o_ref[...] = x_ref[...].T.astype(o_ref.dtype)


def _transpose_bf16(x, *, rows):
    n, d = x.shape
    return pl.pallas_call(
        _transpose_kernel,
        grid=(n // rows,),
        in_specs=[pl.BlockSpec((rows, d), lambda i: (i, 0))],
        out_specs=pl.BlockSpec((d, rows), lambda i: (0, i)),
        out_shape=jax.ShapeDtypeStruct((d, n), BF16),
        compiler_params=_params("arbitrary"),
        name="transpose_v",
    )(x)


def kernel(x_prompt, x_sample, cache_k, cache_v, state_conv, page_table, c_prompt, c_sample, w_ada, b_ada, w_in, b_in, conv_w, conv_b, conv_ln_g, conv_ln_b, lambda_q1, lambda_k1, lambda_q2, lambda_k2, attn_subln_g, rel_bias, w_out, b_out, ln1_g, ln1_b, peer_w_query, peer_sub_keys, peer_u, peer_v, ln2_g, ln2_b):
    batch, seq, d = x_prompt.shape
    nb, dec_seq, _ = x_sample.shape
    depth = w_ada.shape[0]
    assert depth == 1 and dec_seq == 1
    heads = cache_k.shape[3]
    hd = cache_k.shape[4]
    assert hd == 2 * ATT_HEAD_DIM
    d_attn = heads * hd
    d_conv = conv_w.shape[2]
    hist = conv_w.shape[1] - 1
    assert hist <= CONV_PAD - 1
    page = cache_k.shape[2]
    n_keys = peer_sub_keys.shape[3]
    ne = peer_u.shape[1]
    alpha = (2 * depth) ** 0.25
    lambda_init = 0.8 - 0.6 * math.exp(-0.3 * 0)
    t_p = batch * seq
    ts = min(TOKEN_TILE, seq)
    tq_peer = min(PEER_TOKEN_TILE, seq)
    ek = min(PEER_EXPERT_TILE, ne)
    assert seq % ts == 0 and nb % LANES == 0 and page >= MAX_DISTANCE

    l = 0
    n_c = batch + nb
    n_c_pad = -(-n_c // 8) * 8
    c_all = jnp.concatenate([c_prompt, c_sample, jnp.zeros((n_c_pad - n_c, d), F32)], axis=0)
    mod = _ada(c_all, w_ada[l], b_ada[l])
    mod_p = mod[:, :batch].reshape(6, batch, 1, d)
    mod_s = mod[:, batch:n_c]

    w_in_b = w_in[l].astype(BF16)
    b_in_r = b_in[l].reshape(1, -1)
    w_out_b = w_out[l].astype(BF16)
    wq_b = peer_w_query[l].astype(BF16)
    sk_b = peer_sub_keys[l].reshape(-1, n_keys, peer_sub_keys.shape[-1]).astype(BF16)
    u_b = peer_u[l].astype(BF16)
    vt_b = _transpose_bf16(peer_v[l], rows=min(512, ne))
    lams = [a[l].reshape(1, -1) for a in (lambda_q1, lambda_k1, lambda_q2, lambda_k2)]
    subln_g = attn_subln_g[l]

    xp = x_prompt.reshape(t_p, d)
    glu_p, k_p, v_p, kb_p, qt_p, vt_p = _inproj(xp, mod_p, w_in_b, b_in_r, tm=ts, rows_per_mod=seq,
                                                d_conv=d_conv, d_attn=d_attn)
    cy_p = _conv_prompt(glu_p, conv_w[l], conv_b[l], conv_ln_g[l], conv_ln_b[l], batch=batch, seq=seq, ts=ts)
    bias = _bias_tiles(rel_bias, heads=heads, ts=ts)
    ay_p = _attn_prompt(qt_p, kb_p, vt_p, bias, subln_g, lams, batch=batch, seq=seq, heads=heads, ts=ts,
                        lambda_init=lambda_init)
    x1_p, h2t_p, st_p = _mix(xp, cy_p, ay_p, mod_p, w_out_b, b_out[l], ln1_g[l], ln1_b[l], wq_b, sk_b,
                             tm=ts, rows_per_mod=seq, alpha=alpha)
    r2, e2, n1, e1 = _route(st_p, tl=min(ROUTE_TOKEN_TILE, t_p))
    y_p = _peer(h2t_p, u_b, vt_b, r2, e2, n1, e1, x1_p, mod_p, ln2_g[l], ln2_b[l],
                tq=tq_peer, ek=ek, rows_per_mod=seq, alpha=alpha)

    xs = x_sample.reshape(nb, d)
    glu_s, k_s, v_s, q_s = _inproj(xs, mod_s, w_in_b, b_in_r, tm=nb, rows_per_mod=None,
                                   d_conv=d_conv, d_attn=d_attn)
    state = state_conv[l]
    cy_s = _conv_sample(jnp.swapaxes(state, 0, 1), glu_s, conv_w[l], conv_b[l], conv_ln_g[l], conv_ln_b[l])
    n_pool = cache_k.shape[1]
    ay_s = _attn_sample(page_table, rel_bias, q_s, k_s, v_s,
                        cache_k[l].reshape(n_pool, page, d_attn), cache_v[l].reshape(n_pool, page, d_attn),
                        subln_g, lams, heads=heads, lambda_init=lambda_init)
    x1_s, h2t_s, st_s = _mix(xs, cy_s, ay_s, mod_s, w_out_b, b_out[l], ln1_g[l], ln1_b[l], wq_b, sk_b,
                             tm=nb, rows_per_mod=None, alpha=alpha)
    r2s, e2s, n1s, e1s = _route(st_s, tl=nb)
    y_s = _peer(h2t_s, u_b, vt_b, r2s, e2s, n1s, e1s, x1_s, mod_s, ln2_g[l], ln2_b[l],
                tq=nb, ek=ek, rows_per_mod=None, alpha=alpha)

    kv_p = (depth, batch, seq, heads, hd)
    kv_s = (depth, nb, dec_seq, heads, hd)
    conv_p = glu_p.reshape(batch, seq, d_conv)[:, seq - hist:][None]
    conv_s = jnp.concatenate([state[:, 1:], glu_s[:, None, :]], axis=1)[None]
    return (y_p.reshape(batch, seq, d), y_s.reshape(nb, dec_seq, d),
            k_p.reshape(kv_p), v_p.reshape(kv_p), conv_p,
            k_s.reshape(kv_s), v_s.reshape(kv_s), conv_s)
```

```python
import functools
import math

import jax
import jax.numpy as jnp
from jax import lax
from jax.experimental import pallas as pl
from jax.experimental.pallas import tpu as pltpu

F32 = jnp.float32
BF16 = jnp.bfloat16

LN_EPS = 1e-5
NEG_INF = -1e30
ATT_HEAD_DIM = 64
N_BUCKETS = 32
MAX_DISTANCE = 128
PEER_TOPK = 16
LANES = 128
VMEM_LIMIT_BYTES = 56 * 1024 * 1024
TOKEN_TILE = 512
PEER_TOKEN_TILE = 512
PEER_EXPERT_TILE = 2048
PEER_EXPERT_CHUNK = 512
ROUTE_TOKEN_TILE = 256


def _params(*sem):
    return pltpu.CompilerParams(dimension_semantics=sem, vmem_limit_bytes=VMEM_LIMIT_BYTES)


def _ln(x):
    mu = jnp.mean(x, axis=-1, keepdims=True)
    xc = x - mu
    var = jnp.mean(xc * xc, axis=-1, keepdims=True)
    return xc * lax.rsqrt(var + LN_EPS)


def _full(shape):
    return pl.BlockSpec(shape, lambda *_: (0,) * len(shape))


def _ada_kernel(c_ref, w_ref, b_ref, o_ref):
    c = c_ref[...]
    s = c * jax.nn.sigmoid(c)
    o_ref[0] = jnp.dot(s.astype(BF16), w_ref[...].astype(BF16), preferred_element_type=F32) + b_ref[0]


def _ada(c_all, w_ada, b_ada):
    nc, d = c_all.shape
    return pl.pallas_call(
        _ada_kernel,
        grid=(6,),
        in_specs=[_full((nc, d)),
                  pl.BlockSpec((d, d), lambda k: (0, k)),
                  pl.BlockSpec((1, 1, d), lambda k: (k, 0, 0))],
        out_specs=pl.BlockSpec((1, nc, d), lambda k: (k, 0, 0)),
        out_shape=jax.ShapeDtypeStruct((6, nc, d), F32),
        compiler_params=_params("arbitrary"),
        name="ada",
    )(c_all, w_ada, b_ada.reshape(6, 1, d))


def _inproj_kernel(x_ref, mod_ref, w_ref, b_ref, *outs, per_token_mod, d_conv, d_attn, transposed):
    x = x_ref[...]
    if per_token_mod:
        sh1, sc1 = mod_ref[0], mod_ref[1]
    else:
        sh1, sc1 = mod_ref[0, 0], mod_ref[1, 0]
    h = _ln(x) * (1.0 + sc1) + sh1
    z = jnp.dot(h.astype(BF16), w_ref[...], preferred_element_type=F32) + b_ref[...]
    ga = z[:, :d_conv]
    gb = z[:, d_conv:2 * d_conv]
    o = 2 * d_conv
    q = z[:, o:o + d_attn]
    k = z[:, o + d_attn:o + 2 * d_attn]
    v = z[:, o + 2 * d_attn:o + 3 * d_attn]
    glu = ga * jax.nn.sigmoid(gb)
    if transposed:
        glu_ref, k_ref, v_ref, kb_ref, qt_ref, vt_ref = outs
        kb_ref[...] = k.astype(BF16)
        qt_ref[0] = (q * (ATT_HEAD_DIM ** -0.5)).T.astype(BF16)
        vt_ref[0] = v.T.astype(BF16)
    else:
        glu_ref, k_ref, v_ref, q_ref = outs
        q_ref[...] = q
    glu_ref[...] = glu
    k_ref[...] = k
    v_ref[...] = v


def _inproj(x, mod, w_in, b_in, *, tm, rows_per_mod, d_conv, d_attn):
    t, d = x.shape
    n_in = w_in.shape[1]
    nt = t // tm
    per_token = rows_per_mod is None
    if per_token:
        mod_spec = pl.BlockSpec((6, tm, d), lambda i: (0, i, 0))
    else:
        mod_spec = pl.BlockSpec((6, 1, 1, d), lambda i: (0, (i * tm) // rows_per_mod, 0, 0))
    row = lambda n, dt: (jax.ShapeDtypeStruct((t, n), dt), pl.BlockSpec((tm, n), lambda i: (i, 0)))
    outs = [row(d_conv, F32), row(d_attn, F32), row(d_attn, F32)]
    if per_token:
        outs.append(row(d_attn, F32))
    else:
        outs.append(row(d_attn, BF16))
        tr = (jax.ShapeDtypeStruct((nt, d_attn, tm), BF16), pl.BlockSpec((1, d_attn, tm), lambda i: (i, 0, 0)))
        outs += [tr, tr]
    return pl.pallas_call(
        functools.partial(_inproj_kernel, per_token_mod=per_token, d_conv=d_conv, d_attn=d_attn,
                          transposed=not per_token),
        grid=(nt,),
        in_specs=[pl.BlockSpec((tm, d), lambda i: (i, 0)), mod_spec, _full((d, n_in)), _full((1, n_in))],
        out_specs=[o[1] for o in outs],
        out_shape=[o[0] for o in outs],
        compiler_params=_params("arbitrary"),
        name="inproj_sample" if per_token else "inproj_prompt",
    )(x, mod, w_in, b_in)


CONV_PAD = 32


def _conv_post(acc, cb_ref, g_ref, b_ref):
    y = _ln(acc + cb_ref[...]) * g_ref[...] + b_ref[...]
    return y * jax.nn.sigmoid(y)


def _conv_prompt_kernel(glu_ref, cw_ref, cb_ref, g_ref, b_ref, y_ref, full_ref, *, ts, width, chunk):
    s = pl.program_id(1)
    hist = width - 1

    @pl.when(s == 0)
    def _():
        full_ref[0:CONV_PAD, :] = jnp.zeros((CONV_PAD, full_ref.shape[1]), F32)

    @pl.when(s > 0)
    def _():
        full_ref[0:CONV_PAD, :] = full_ref[ts:ts + CONV_PAD, :]

    full_ref[CONV_PAD:CONV_PAD + ts, :] = glu_ref[...]
    base = CONV_PAD - hist
    for c in range(ts // chunk):
        r0 = c * chunk
        acc = full_ref[base + r0:base + r0 + chunk, :] * cw_ref[0:1, :]
        for w in range(1, width):
            acc = acc + full_ref[base + r0 + w:base + r0 + w + chunk, :] * cw_ref[w:w + 1, :]
        y_ref[r0:r0 + chunk, :] = _conv_post(acc, cb_ref, g_ref, b_ref).astype(y_ref.dtype)


def _conv_prompt(glu, conv_w, conv_b, ln_g, ln_b, *, batch, seq, ts):
    t, dc = glu.shape
    width = conv_w.shape[0]
    ns = seq // ts
    vec = lambda a: a.reshape(1, dc)
    return pl.pallas_call(
        functools.partial(_conv_prompt_kernel, ts=ts, width=width, chunk=min(64, ts)),
        grid=(batch, ns),
        in_specs=[pl.BlockSpec((ts, dc), lambda b, s: (b * ns + s, 0)),
                  _full((width, dc)), _full((1, dc)), _full((1, dc)), _full((1, dc))],
        out_specs=pl.BlockSpec((ts, dc), lambda b, s: (b * ns + s, 0)),
        out_shape=jax.ShapeDtypeStruct((t, dc), BF16),
        scratch_shapes=[pltpu.VMEM((CONV_PAD + ts, dc), F32)],
        compiler_params=_params("arbitrary", "arbitrary"),
        name="conv_prompt",
    )(glu, conv_w, vec(conv_b), vec(ln_g), vec(ln_b))


def _conv_sample_kernel(st_ref, glu_ref, cw_ref, cb_ref, g_ref, b_ref, y_ref, *, width):
    hist = width - 1
    acc = glu_ref[...] * cw_ref[hist:hist + 1, :]
    for w in range(hist):
        acc = acc + st_ref[w] * cw_ref[w:w + 1, :]
    y_ref[...] = _conv_post(acc, cb_ref, g_ref, b_ref).astype(y_ref.dtype)


def _conv_sample(state_t, glu, conv_w, conv_b, ln_g, ln_b):
    hist, nb, dc = state_t.shape
    width = conv_w.shape[0]
    vec = lambda a: a.reshape(1, dc)
    return pl.pallas_call(
        functools.partial(_conv_sample_kernel, width=width),
        grid=(1,),
        in_specs=[_full((hist, nb, dc)), _full((nb, dc)), _full((width, dc)),
                  _full((1, dc)), _full((1, dc)), _full((1, dc))],
        out_specs=_full((nb, dc)),
        out_shape=jax.ShapeDtypeStruct((nb, dc), BF16),
        compiler_params=_params("arbitrary"),
        name="conv_sample",
    )(state_t, glu, conv_w, vec(conv_b), vec(ln_g), vec(ln_b))


def _bucket(n):
    max_exact = N_BUCKETS // 2
    nf = jnp.maximum(n, 1).astype(F32)
    large = max_exact + (jnp.log(nf / max_exact) / math.log(MAX_DISTANCE / max_exact)
                         * (N_BUCKETS - max_exact)).astype(jnp.int32)
    large = jnp.minimum(large, N_BUCKETS - 1)
    return jnp.where(n < max_exact, n, large)


def _bias_of(n, rb_ref, h):
    bucket = _bucket(n)
    far = jnp.full(n.shape, rb_ref[N_BUCKETS - 1, h], F32)
    out = far
    for j in range(N_BUCKETS - 1):
        out = jnp.where(bucket == j, rb_ref[j, h], out)
    return out - far


def _bias_kernel(rb_ref, o_ref, *, ts):
    h = pl.program_id(0)
    d = pl.program_id(1)
    ik = lax.broadcasted_iota(jnp.int32, (ts, ts), 0)
    jq = lax.broadcasted_iota(jnp.int32, (ts, ts), 1)
    n = jq - ik + ts * (1 - d)
    b = _bias_of(jnp.maximum(n, 0), rb_ref, h)
    o_ref[0, 0] = jnp.where(n >= 0, b, NEG_INF)


def _bias_tiles(rel_bias, *, heads, ts):
    return pl.pallas_call(
        functools.partial(_bias_kernel, ts=ts),
        grid=(heads, 2),
        in_specs=[pl.BlockSpec(memory_space=pltpu.SMEM)],
        out_specs=pl.BlockSpec((1, 1, ts, ts), lambda h, d: (h, d, 0, 0)),
        out_shape=jax.ShapeDtypeStruct((heads, 2, ts, ts), F32),
        compiler_params=_params("arbitrary", "arbitrary"),
        name="bias_tiles",
    )(rel_bias)


def _lambda(lq1, lk1, lq2, lk2, lambda_init):
    s1 = jnp.sum(lq1[...] * lk1[...], axis=-1, keepdims=True)
    s2 = jnp.sum(lq2[...] * lk2[...], axis=-1, keepdims=True)
    return jnp.exp(s1) - jnp.exp(s2) + lambda_init


def _attn_kernel(qt_ref, k_ref, vt_ref, bias_ref, g_ref, lq1, lk1, lq2, lk2, o_ref,
                 qs_ref, m_ref, l_ref, acc_ref, *, ts, lambda_init):
    qi = pl.program_id(2)
    d = ATT_HEAD_DIM
    qt = qt_ref[0]
    row = lax.broadcasted_iota(jnp.int32, qt.shape, 0)
    zero = jnp.zeros_like(qt)
    qs_ref[:, :ts] = jnp.where(row < d, qt, zero)
    qs_ref[:, ts:] = jnp.where(row >= d, qt, zero)
    m_ref[...] = jnp.full(m_ref.shape, NEG_INF, F32)
    l_ref[...] = jnp.zeros(l_ref.shape, F32)
    acc_ref[...] = jnp.zeros(acc_ref.shape, F32)

    def block(ki, bias):
        kblk = k_ref[pl.ds(pl.multiple_of(ki * ts, ts), ts), :]
        s = jnp.dot(kblk, qs_ref[...], preferred_element_type=F32)
        if bias is not None:
            s = s + jnp.concatenate([bias, bias], axis=1)
        m_prev = m_ref[...]
        m_new = jnp.maximum(m_prev, jnp.max(s, axis=0, keepdims=True))
        alpha = jnp.exp(m_prev - m_new)
        p = jnp.exp(s - m_new)
        l_ref[...] = alpha * l_ref[...] + jnp.sum(p, axis=0, keepdims=True)
        acc_ref[...] = alpha * acc_ref[...] + jnp.dot(vt_ref[ki], p.astype(BF16), preferred_element_type=F32)
        m_ref[...] = m_new

    def far(ki, carry):
        block(ki, None)
        return carry

    lax.fori_loop(0, jnp.maximum(qi - 1, 0), far, 0)

    @pl.when(qi > 0)
    def _():
        block(qi - 1, bias_ref[0, 0])

    block(qi, bias_ref[0, 1])

    lam = _lambda(lq1, lk1, lq2, lk2, lambda_init)
    inv_l = 1.0 / l_ref[...]
    acc = acc_ref[...]
    o = acc[:, :ts] * inv_l[:, :ts] - lam * (acc[:, ts:] * inv_l[:, ts:])
    o = o * lax.rsqrt(jnp.mean(o * o, axis=0, keepdims=True) + LN_EPS)
    o = o * g_ref[...] * (1.0 - lambda_init)
    o_ref[...] = o.T.astype(o_ref.dtype)


def _attn_prompt(qt, kb, vt, bias, subln_g, lams, *, batch, seq, heads, ts, lambda_init):
    t, d_attn = kb.shape
    hd = 2 * ATT_HEAD_DIM
    nq = seq // ts
    lam_spec = _full((1, ATT_HEAD_DIM))
    return pl.pallas_call(
        functools.partial(_attn_kernel, ts=ts, lambda_init=lambda_init),
        grid=(batch, heads, nq),
        in_specs=[pl.BlockSpec((1, hd, ts), lambda b, h, q: (b * nq + q, h, 0)),
                  pl.BlockSpec((seq, hd), lambda b, h, q: (b, h)),
                  pl.BlockSpec((nq, hd, ts), lambda b, h, q: (b, h, 0)),
                  pl.BlockSpec((1, 2, ts, ts), lambda b, h, q: (h, 0, 0, 0)),
                  _full((hd, 1)), lam_spec, lam_spec, lam_spec, lam_spec],
        out_specs=pl.BlockSpec((ts, hd), lambda b, h, q: (b * nq + q, h)),
        out_shape=jax.ShapeDtypeStruct((t, d_attn), BF16),
        scratch_shapes=[pltpu.VMEM((hd, 2 * ts), BF16), pltpu.VMEM((1, 2 * ts), F32),
                        pltpu.VMEM((1, 2 * ts), F32), pltpu.VMEM((hd, 2 * ts), F32)],
        compiler_params=_params("arbitrary", "arbitrary", "arbitrary"),
        name="attn_prompt",
    )(qt, kb, vt, bias, subln_g.reshape(hd, 1), *lams)


def _decode_kernel(pt_ref, rb_ref, q_ref, kn_ref, vn_ref, g_ref, lq1, lk1, lq2, lk2, *rest,
                   n_pages, page, heads, lambda_init):
    k_refs = rest[:n_pages]
    v_refs = rest[n_pages:2 * n_pages]
    o_ref = rest[2 * n_pages]
    s_ref, p_ref = rest[2 * n_pages + 1:]
    d = ATT_HEAD_DIM
    hd = 2 * d
    width = heads * hd
    past = n_pages * page
    half2 = LANES // 2

    q = q_ref[0] * (d ** -0.5)
    r = lax.broadcasted_iota(jnp.int32, (LANES, width), 0)
    c = lax.broadcasted_iota(jnp.int32, (LANES, width), 1)
    c_head = lax.shift_right_logical(c, int(math.log2(hd)))
    c_first = (c & (hd - 1)) < d
    r_head = jnp.where(r < half2, r, r - half2)
    r_live = (r < heads) | ((r >= half2) & (r < half2 + heads))
    own = (c_head == r_head) & r_live & (c_first == (r < half2))
    qbd = jnp.where(own, jnp.broadcast_to(q, (LANES, width)), 0.0).T.astype(BF16)

    for j in range(n_pages):
        s_ref[j * page:(j + 1) * page, :] = jnp.dot(k_refs[j][0].astype(BF16), qbd, preferred_element_type=F32)
    kn = jnp.broadcast_to(kn_ref[0], (8, width)).astype(BF16)
    s_new = jnp.dot(kn, qbd, preferred_element_type=F32)
    row8 = lax.broadcasted_iota(jnp.int32, (8, LANES), 0)

    lane = lax.broadcasted_iota(jnp.int32, (page, LANES), 1)
    krow = lax.broadcasted_iota(jnp.int32, (page, LANES), 0)
    head_of_lane = jnp.where(lane < half2, lane, lane - half2)
    n_last = page - krow
    bias_last = jnp.zeros((page, LANES), F32)
    bias_new = jnp.zeros((8, LANES), F32)
    lane8 = lax.broadcasted_iota(jnp.int32, (8, LANES), 1)
    head8 = jnp.where(lane8 < half2, lane8, lane8 - half2)
    for h in range(heads):
        bias_last = jnp.where(head_of_lane == h, _bias_of(n_last, rb_ref, h), bias_last)
        bias_new = jnp.where(head8 == h, _bias_of(jnp.zeros((8, LANES), jnp.int32), rb_ref, h), bias_new)
    s_ref[past - page:past, :] = s_ref[past - page:past, :] + bias_last
    s_ref[past:past + 8, :] = jnp.where(row8 == 0, s_new + bias_new, NEG_INF)

    s = s_ref[...]
    m = jnp.max(s, axis=0, keepdims=True)
    p = jnp.exp(s - m)
    l = jnp.sum(p, axis=0, keepdims=True)
    lam = _lambda(lq1, lk1, lq2, lk2, lambda_init)
    lane1 = lax.broadcasted_iota(jnp.int32, (1, LANES), 1)
    coef = jnp.where(lane1 < half2, 1.0 / l, -lam / l)
    p_ref[...] = (p * coef).astype(BF16)

    expand = jnp.where((c_head == r_head) & r_live, 1.0, 0.0).astype(BF16)

    acc = jnp.zeros((page, width), F32)
    for j in range(n_pages):
        wrep = jnp.dot(p_ref[j * page:(j + 1) * page, :], expand, preferred_element_type=F32)
        acc = acc + wrep * v_refs[j][0]
    w_new = jnp.dot(p_ref[past:past + 8, :], expand, preferred_element_type=F32)
    o = jnp.sum(acc, axis=0, keepdims=True) + w_new[0:1, :] * vn_ref[0]

    col_head = lax.shift_right_logical(lax.broadcasted_iota(jnp.int32, (1, width), 1), int(math.log2(hd)))
    ms = jnp.zeros((1, width), F32)
    for h in range(heads):
        in_h = col_head == h
        ms = jnp.where(in_h, jnp.sum(jnp.where(in_h, o * o, 0.0), axis=-1, keepdims=True) / hd, ms)
    o = o * lax.rsqrt(ms + LN_EPS) * g_ref[...] * (1.0 - lambda_init)
    o_ref[0] = o.astype(o_ref.dtype)


def _attn_sample(page_table, rel_bias, q, k_new, v_new, cache_k, cache_v, subln_g, lams, *, heads, lambda_init):
    nb, n_pages = page_table.shape
    n_pool, page, width = cache_k.shape
    hd = 2 * ATT_HEAD_DIM
    tok = lambda a: a.reshape(nb, 1, width)
    tok_spec = pl.BlockSpec((1, 1, width), lambda b, pt: (b, 0, 0))
    lam_spec = pl.BlockSpec((1, ATT_HEAD_DIM), lambda b, pt: (0, 0))
    page_specs = [pl.BlockSpec((1, page, width), lambda b, pt, j=j: (pt[b, j], 0, 0)) for j in range(n_pages)]
    grid_spec = pltpu.PrefetchScalarGridSpec(
        num_scalar_prefetch=1,
        grid=(nb,),
        in_specs=[pl.BlockSpec(memory_space=pltpu.SMEM), tok_spec, tok_spec, tok_spec,
                  pl.BlockSpec((1, width), lambda b, pt: (0, 0)),
                  lam_spec, lam_spec, lam_spec, lam_spec] + page_specs + page_specs,
        out_specs=tok_spec,
        scratch_shapes=[pltpu.VMEM((n_pages * page + 8, LANES), F32),
                        pltpu.VMEM((n_pages * page + 8, LANES), BF16)],
    )
    out = pl.pallas_call(
        functools.partial(_decode_kernel, n_pages=n_pages, page=page, heads=heads, lambda_init=lambda_init),
        grid_spec=grid_spec,
        out_shape=jax.ShapeDtypeStruct((nb, 1, width), BF16),
        compiler_params=_params("arbitrary"),
        name="attn_sample",
    )(page_table, rel_bias, tok(q), tok(k_new), tok(v_new), jnp.tile(subln_g, heads).reshape(1, width), *lams,
      *([cache_k] * n_pages), *([cache_v] * n_pages))
    return out.reshape(nb, width)


def _mix_kernel(x_ref, cy_ref, ay_ref, mod_ref, wo_ref, bo_ref, g1_ref, b1_ref, wq_ref, sk_ref,
                x1_ref, h2t_ref, st_ref, *, per_token_mod, alpha, d_conv):
    if per_token_mod:
        gate1, sh2, sc2 = mod_ref[2], mod_ref[3], mod_ref[4]
    else:
        gate1, sh2, sc2 = mod_ref[2, 0], mod_ref[3, 0], mod_ref[4, 0]
    mix = (jnp.dot(cy_ref[...], wo_ref[:d_conv, :], preferred_element_type=F32)
           + jnp.dot(ay_ref[...], wo_ref[d_conv:, :], preferred_element_type=F32) + bo_ref[...])
    x1 = _ln(alpha * x_ref[...] + gate1 * mix) * g1_ref[...] + b1_ref[...]
    x1_ref[...] = x1
    h2 = _ln(x1) * (1.0 + sc2) + sh2
    h2b = h2.astype(BF16)
    h2t_ref[...] = h2.T.astype(BF16)
    qh = jnp.dot(h2b, wq_ref[...], preferred_element_type=F32).astype(BF16)
    nk = sk_ref.shape[2]
    for hc in range(sk_ref.shape[0]):
        st_ref[hc] = lax.dot_general(sk_ref[hc], qh[:, hc * nk:(hc + 1) * nk], (((1,), (1,)), ((), ())),
                                     preferred_element_type=F32)


def _mix(x, cy, ay, mod, w_out, b_out, ln_g, ln_b, w_query, sub_keys, *, tm, rows_per_mod, alpha):
    t, d = x.shape
    d_conv = cy.shape[1]
    d_attn = ay.shape[1]
    n_hc, n_keys, half = sub_keys.shape
    per_token = rows_per_mod is None
    if per_token:
        mod_spec = pl.BlockSpec((6, tm, d), lambda i: (0, i, 0))
    else:
        mod_spec = pl.BlockSpec((6, 1, 1, d), lambda i: (0, (i * tm) // rows_per_mod, 0, 0))
    vec = lambda a: a.reshape(1, d)
    return pl.pallas_call(
        functools.partial(_mix_kernel, per_token_mod=per_token, alpha=alpha, d_conv=d_conv),
        grid=(t // tm,),
        in_specs=[pl.BlockSpec((tm, d), lambda i: (i, 0)),
                  pl.BlockSpec((tm, d_conv), lambda i: (i, 0)),
                  pl.BlockSpec((tm, d_attn), lambda i: (i, 0)),
                  mod_spec, _full(w_out.shape), _full((1, d)), _full((1, d)), _full((1, d)),
                  _full(w_query.shape), _full(sub_keys.shape)],
        out_specs=[pl.BlockSpec((tm, d), lambda i: (i, 0)),
                   pl.BlockSpec((d, tm), lambda i: (0, i)),
                   pl.BlockSpec((n_hc, n_keys, tm), lambda i: (0, 0, i))],
        out_shape=[jax.ShapeDtypeStruct((t, d), F32),
                   jax.ShapeDtypeStruct((d, t), BF16),
                   jax.ShapeDtypeStruct((n_hc, n_keys, t), F32)],
        compiler_params=_params("arbitrary"),
        name="mix_sample" if per_token else "mix_prompt",
    )(x, cy, ay, mod, w_out, vec(b_out), vec(ln_g), vec(ln_b), w_query, sub_keys)


def _top_ranks(s):
    work = s
    rank = jnp.full(s.shape, float(PEER_TOPK), F32)
    vals = []
    for r in range(PEER_TOPK):
        m = jnp.max(work, axis=0, keepdims=True)
        hit = work == m
        rank = jnp.where(hit, float(r), rank)
        work = jnp.where(hit, -jnp.inf, work)
        vals.append(m)
    return rank, vals


def _route_kernel(s_ref, r2_ref, e2_ref, n1_ref, e1_ref):
    k = PEER_TOPK
    s1 = s_ref[0]
    s2 = s_ref[1]
    rank1, v1 = _top_ranks(s1)
    rank2, v2 = _top_ranks(s2)
    rowk = lax.broadcasted_iota(jnp.int32, (k,) + s1.shape[1:], 0)

    def stack(vals):
        out = jnp.zeros(rowk.shape, F32)
        for r in range(k):
            out = jnp.where(rowk == r, vals[r], out)
        return out

    v1m = stack(v1)
    v2m = stack(v2)
    row8 = lax.broadcasted_iota(jnp.int32, (8,) + s1.shape[1:], 0)
    cands = [v1[0] + v2m]
    for a in range(1, 8):
        cands.append(jnp.where(row8 < k // (a + 1), v1[a] + v2m[0:8], -jnp.inf))
    cands.append(v1m[8:k] + v2[0])
    work = jnp.concatenate(cands, axis=0)
    thr = None
    for _ in range(k):
        thr = jnp.max(work, axis=0, keepdims=True)
        work = jnp.where(work == thr, -jnp.inf, work)
    e2top = jnp.exp(v2m - v2[0])
    z = jnp.zeros_like(thr)
    n1 = jnp.zeros(s1.shape, F32)
    for a in range(k):
        sel = (v1[a] + v2m) >= thr
        cnt = jnp.sum(jnp.where(sel, 1.0, 0.0), axis=0, keepdims=True)
        z = z + jnp.exp(v1[a] - v1[0]) * jnp.sum(jnp.where(sel, e2top, 0.0), axis=0, keepdims=True)
        n1 = jnp.where(rank1 == float(a), cnt, n1)
    r2_ref[0] = rank2.astype(r2_ref.dtype)
    n1_ref[0] = _bf16_pair(n1)
    e1_ref[0] = _bf16_pair(jnp.where(rank1 < float(k), jnp.exp(s1 - v1[0]) / z, 0.0))
    e2_ref[0] = jnp.where(rank2 < float(k), jnp.exp(s2 - v2[0]), 0.0).astype(e2_ref.dtype)


def _bf16_pair(x):
    bits = pltpu.bitcast(x.astype(BF16).astype(F32), jnp.uint32)
    return bits | lax.shift_right_logical(bits, jnp.uint32(16))


def _route(st, *, tl):
    n_hc, n_keys, t = st.shape
    heads = n_hc // 2
    out = lambda dt: jax.ShapeDtypeStruct((heads, n_keys, t), dt)
    spec = pl.BlockSpec((1, n_keys, tl), lambda i, h: (h, 0, i))
    return pl.pallas_call(
        _route_kernel,
        grid=(t // tl, heads),
        in_specs=[pl.BlockSpec((2, n_keys, tl), lambda i, h: (h, 0, i))],
        out_specs=[spec] * 4,
        out_shape=[out(BF16), out(BF16), out(jnp.uint32), out(jnp.uint32)],
        compiler_params=_params("arbitrary", "arbitrary"),
        name="route",
    )(st)


GELU_C1 = math.sqrt(2.0 / math.pi)
GELU_C2 = GELU_C1 * 0.044715


def _gelu_tanh(x):
    hx = 0.5 * x
    return hx + hx * jnp.tanh(x * (GELU_C1 + GELU_C2 * (x * x)))


def _packed_rows(words):
    return pltpu.bitcast(jnp.broadcast_to(words, (8, words.shape[1])), BF16)


def _peer_kernel(h2t_ref, u_ref, vt_ref, r2_ref, e2_ref, n1_ref, e1_ref, x1_ref, mod_ref, g_ref, b_ref, y_ref,
                 at_ref, zt_ref, acc_ref, *, per_token_mod, alpha, n_keys, chunk):
    j = pl.program_id(1)
    ek, tq = at_ref.shape
    groups = ek // n_keys
    heads = r2_ref.shape[0]
    sub = 16

    @pl.when(j == 0)
    def _():
        acc_ref[...] = jnp.zeros(acc_ref.shape, F32)

    for c in range(ek // chunk):
        lo = c * chunk
        at_ref[lo:lo + chunk, :] = jnp.dot(u_ref[lo:lo + chunk, :], h2t_ref[...], preferred_element_type=F32)
        for g in range(chunk // n_keys):
            i1 = j * groups + c * (chunk // n_keys) + g
            n1b = [_packed_rows(n1_ref[h, pl.ds(i1, 1), :]) for h in range(heads)]
            e1b = [_packed_rows(e1_ref[h, pl.ds(i1, 1), :]) for h in range(heads)]
            for k in range(n_keys // sub):
                r0 = lo + g * n_keys + k * sub
                keys = slice(k * sub, (k + 1) * sub)
                act = _gelu_tanh(at_ref[r0:r0 + sub, :].astype(BF16))
                w = jnp.zeros((sub, tq), BF16)
                for h in range(heads):
                    w = jnp.where(r2_ref[h, keys, :] < n1b[h], w + e2_ref[h, keys, :] * e1b[h], w)
                zt_ref[r0:r0 + sub, :] = w * act
        acc_ref[...] += jnp.dot(vt_ref[:, lo:lo + chunk], zt_ref[lo:lo + chunk, :], preferred_element_type=F32)

    @pl.when(j == pl.num_programs(1) - 1)
    def _():
        gate2 = mod_ref[5] if per_token_mod else mod_ref[5, 0]
        ff = acc_ref[...].T
        y_ref[...] = _ln(alpha * x1_ref[...] + gate2 * ff) * g_ref[...] + b_ref[...]


def _peer(h2t, u_b, vt_b, r2, e2, n1, e1, x1, mod, ln_g, ln_b, *, tq, ek, rows_per_mod, alpha):
    d, t = h2t.shape
    ne = u_b.shape[0]
    heads, n_keys, _ = r2.shape
    per_token = rows_per_mod is None
    if per_token:
        mod_spec = pl.BlockSpec((6, tq, d), lambda i, j: (0, i, 0))
    else:
        mod_spec = pl.BlockSpec((6, 1, 1, d), lambda i, j: (0, (i * tq) // rows_per_mod, 0, 0))
    tab = pl.BlockSpec((heads, n_keys, tq), lambda i, j: (0, 0, i))
    vec = lambda a: a.reshape(1, d)
    return pl.pallas_call(
        functools.partial(_peer_kernel, per_token_mod=per_token, alpha=alpha, n_keys=n_keys,
                          chunk=min(PEER_EXPERT_CHUNK, ek)),
        grid=(t // tq, ne // ek),
        in_specs=[pl.BlockSpec((d, tq), lambda i, j: (0, i)),
                  pl.BlockSpec((ek, d), lambda i, j: (j, 0)),
                  pl.BlockSpec((d, ek), lambda i, j: (0, j)),
                  tab, tab, tab, tab,
                  pl.BlockSpec((tq, d), lambda i, j: (i, 0)),
                  mod_spec, _full((1, d)), _full((1, d))],
        out_specs=pl.BlockSpec((tq, d), lambda i, j: (i, 0)),
        out_shape=jax.ShapeDtypeStruct((t, d), F32),
        scratch_shapes=[pltpu.VMEM((ek, tq), F32), pltpu.VMEM((ek, tq), BF16), pltpu.VMEM((d, tq), F32)],
        compiler_params=_params("arbitrary", "arbitrary"),
        name="peer_sample" if per_token else "peer_prompt",
    )(h2t, u_b, vt_b, r2, e2, n1, e1, x1, mod, vec(ln_g), vec(ln_b))


def _transpose_kernel(x_ref, o_ref):
    o_ref[...] = x_ref[...].T.astype(o_ref.dtype)


def _transpose_bf16(x, *, rows):
    n, d = x.shape
    return pl.pallas_call(
        _transpose_kernel,
        grid=(n // rows,),
        in_specs=[pl.BlockSpec((rows, d), lambda i: (i, 0))],
        out_specs=pl.BlockSpec((d, rows), lambda i: (0, i)),
        out_shape=jax.ShapeDtypeStruct((d, n), BF16),
        compiler_params=_params("arbitrary"),
        name="transpose_v",
    )(x)


def kernel(x_prompt, x_sample, cache_k, cache_v, state_conv, page_table, c_prompt, c_sample, w_ada, b_ada, w_in, b_in, conv_w, conv_b, conv_ln_g, conv_ln_b, lambda_q1, lambda_k1, lambda_q2, lambda_k2, attn_subln_g, rel_bias, w_out, b_out, ln1_g, ln1_b, peer_w_query, peer_sub_keys, peer_u, peer_v, ln2_g, ln2_b):
    batch, seq, d = x_prompt.shape
    nb, dec_seq, _ = x_sample.shape
    depth = w_ada.shape[0]
    assert depth == 1 and dec_seq == 1
    heads = cache_k.shape[3]
    hd = cache_k.shape[4]
    assert hd == 2 * ATT_HEAD_DIM
    d_attn = heads * hd
    d_conv = conv_w.shape[2]
    hist = conv_w.shape[1] - 1
    assert hist <= CONV_PAD - 1
    page = cache_k.shape[2]
    n_keys = peer_sub_keys.shape[3]
    ne = peer_u.shape[1]
    alpha = (2 * depth) ** 0.25
    lambda_init = 0.8 - 0.6 * math.exp(-0.3 * 0)
    t_p = batch * seq
    ts = min(TOKEN_TILE, seq)
    tq_peer = min(PEER_TOKEN_TILE, seq)
    ek = min(PEER_EXPERT_TILE, ne)
    assert seq % ts == 0 and nb % LANES == 0 and page >= MAX_DISTANCE

    l = 0
    n_c = batch + nb
    n_c_pad = -(-n_c // 8) * 8
    c_all = jnp.concatenate([c_prompt, c_sample, jnp.zeros((n_c_pad - n_c, d), F32)], axis=0)
    mod = _ada(c_all, w_ada[l], b_ada[l])
    mod_p = mod[:, :batch].reshape(6, batch, 1, d)
    mod_s = mod[:, batch:n_c]

    w_in_b = w_in[l].astype(BF16)
    b_in_r = b_in[l].reshape(1, -1)
    w_out_b = w_out[l].astype(BF16)
    wq_b = peer_w_query[l].astype(BF16)
    sk_b = peer_sub_keys[l].reshape(-1, n_keys, peer_sub_keys.shape[-1]).astype(BF16)
    u_b = peer_u[l].astype(BF16)
    vt_b = _transpose_bf16(peer_v[l], rows=min(512, ne))
    lams = [a[l].reshape(1, -1) for a in (lambda_q1, lambda_k1, lambda_q2, lambda_k2)]
    subln_g = attn_subln_g[l]

    xp = x_prompt.reshape(t_p, d)
    glu_p, k_p, v_p, kb_p, qt_p, vt_p = _inproj(xp, mod_p, w_in_b, b_in_r, tm=ts, rows_per_mod=seq,
                                                d_conv=d_conv, d_attn=d_attn)
    cy_p = _conv_prompt(glu_p, conv_w[l], conv_b[l], conv_ln_g[l], conv_ln_b[l], batch=batch, seq=seq, ts=ts)
    bias = _bias_tiles(rel_bias, heads=heads, ts=ts)
    ay_p = _attn_prompt(qt_p, kb_p, vt_p, bias, subln_g, lams, batch=batch, seq=seq, heads=heads, ts=ts,
                        lambda_init=lambda_init)
    x1_p, h2t_p, st_p = _mix(xp, cy_p, ay_p, mod_p, w_out_b, b_out[l], ln1_g[l], ln1_b[l], wq_b, sk_b,
                             tm=ts, rows_per_mod=seq, alpha=alpha)
    r2, e2, n1, e1 = _route(st_p, tl=min(ROUTE_TOKEN_TILE, t_p))
    y_p = _peer(h2t_p, u_b, vt_b, r2, e2, n1, e1, x1_p, mod_p, ln2_g[l], ln2_b[l],
                tq=tq_peer, ek=ek, rows_per_mod=seq, alpha=alpha)

    xs = x_sample.reshape(nb, d)
    glu_s, k_s, v_s, q_s = _inproj(xs, mod_s, w_in_b, b_in_r, tm=nb, rows_per_mod=None,
                                   d_conv=d_conv, d_attn=d_attn)
    state = state_conv[l]
    cy_s = _conv_sample(jnp.swapaxes(state, 0, 1), glu_s, conv_w[l], conv_b[l], conv_ln_g[l], conv_ln_b[l])
    n_pool = cache_k.shape[1]
    ay_s = _attn_sample(page_table + l * n_pool, rel_bias, q_s, k_s, v_s,
                        cache_k.reshape(depth * n_pool, page, d_attn), cache_v.reshape(depth * n_pool, page, d_attn),
                        subln_g, lams, heads=heads, lambda_init=lambda_init)
    x1_s, h2t_s, st_s = _mix(xs, cy_s, ay_s, mod_s, w_out_b, b_out[l], ln1_g[l], ln1_b[l], wq_b, sk_b,
                             tm=nb, rows_per_mod=None, alpha=alpha)
    r2s, e2s, n1s, e1s = _route(st_s, tl=nb)
    y_s = _peer(h2t_s, u_b, vt_b, r2s, e2s, n1s, e1s, x1_s, mod_s, ln2_g[l], ln2_b[l],
                tq=nb, ek=ek, rows_per_mod=None, alpha=alpha)

    kv_p = (depth, batch, seq, heads, hd)
    kv_s = (depth, nb, dec_seq, heads, hd)
    conv_p = glu_p.reshape(batch, seq, d_conv)[:, seq - hist:][None]
    conv_s = jnp.concatenate([state[:, 1:], glu_s[:, None, :]], axis=1)[None]
    return (y_p.reshape(batch, seq, d), y_s.reshape(nb, dec_seq, d),
            k_p.reshape(kv_p), v_p.reshape(kv_p), conv_p,
            k_s.reshape(kv_s), v_s.reshape(kv_s), conv_s)
```

```python
import functools
import math

import jax
import jax.numpy as jnp
from jax import lax
from jax.experimental import pallas as pl
from jax.experimental.pallas import tpu as pltpu

F32 = jnp.float32
BF16 = jnp.bfloat16

LN_EPS = 1e-5
NEG_INF = -1e30
ATT_HEAD_DIM = 64
N_BUCKETS = 32
MAX_DISTANCE = 128
PEER_TOPK = 16
LANES = 128
VMEM_LIMIT_BYTES = 56 * 1024 * 1024
TOKEN_TILE = 512
PEER_TOKEN_TILE = 512
PEER_EXPERT_TILE = 2048
PEER_EXPERT_CHUNK = 512
ROUTE_TOKEN_TILE = 256


def _params(*sem):
    return pltpu.CompilerParams(dimension_semantics=sem, vmem_limit_bytes=VMEM_LIMIT_BYTES)


def _ln(x):
    mu = jnp.mean(x, axis=-1, keepdims=True)
    xc = x - mu
    var = jnp.mean(xc * xc, axis=-1, keepdims=True)
    return xc * lax.rsqrt(var + LN_EPS)


def _full(shape):
    return pl.BlockSpec(shape, lambda *_: (0,) * len(shape))


def _ada_kernel(c_ref, w_ref, b_ref, o_ref):
    c = c_ref[...]
    s = c * jax.nn.sigmoid(c)
    o_ref[0] = jnp.dot(s.astype(BF16), w_ref[...].astype(BF16), preferred_element_type=F32) + b_ref[0]


def _ada(c_all, w_ada, b_ada):
    nc, d = c_all.shape
    return pl.pallas_call(
        _ada_kernel,
        grid=(6,),
        in_specs=[_full((nc, d)),
                  pl.BlockSpec((d, d), lambda k: (0, k)),
                  pl.BlockSpec((1, 1, d), lambda k: (k, 0, 0))],
        out_specs=pl.BlockSpec((1, nc, d), lambda k: (k, 0, 0)),
        out_shape=jax.ShapeDtypeStruct((6, nc, d), F32),
        compiler_params=_params("arbitrary"),
        name="ada",
    )(c_all, w_ada, b_ada.reshape(6, 1, d))


def _inproj_kernel(x_ref, mod_ref, w_ref, b_ref, *outs, per_token_mod, d_conv, d_attn, transposed):
    x = x_ref[...]
    if per_token_mod:
        sh1, sc1 = mod_ref[0], mod_ref[1]
    else:
        sh1, sc1 = mod_ref[0, 0], mod_ref[1, 0]
    h = _ln(x) * (1.0 + sc1) + sh1
    z = jnp.dot(h.astype(BF16), w_ref[...], preferred_element_type=F32) + b_ref[...]
    ga = z[:, :d_conv]
    gb = z[:, d_conv:2 * d_conv]
    o = 2 * d_conv
    q = z[:, o:o + d_attn]
    k = z[:, o + d_attn:o + 2 * d_attn]
    v = z[:, o + 2 * d_attn:o + 3 * d_attn]
    glu = ga * jax.nn.sigmoid(gb)
    if transposed:
        glu_ref, k_ref, v_ref, kb_ref, qt_ref, vt_ref = outs
        kb_ref[...] = k.astype(BF16)
        qt_ref[0] = (q * (ATT_HEAD_DIM ** -0.5)).T.astype(BF16)
        vt_ref[0] = v.T.astype(BF16)
    else:
        glu_ref, k_ref, v_ref, q_ref = outs
        q_ref[...] = q
    glu_ref[...] = glu
    k_ref[...] = k
    v_ref[...] = v


def _inproj(x, mod, w_in, b_in, *, tm, rows_per_mod, d_conv, d_attn):
    t, d = x.shape
    n_in = w_in.shape[1]
    nt = t // tm
    per_token = rows_per_mod is None
    if per_token:
        mod_spec = pl.BlockSpec((6, tm, d), lambda i: (0, i, 0))
    else:
        mod_spec = pl.BlockSpec((6, 1, 1, d), lambda i: (0, (i * tm) // rows_per_mod, 0, 0))
    row = lambda n, dt: (jax.ShapeDtypeStruct((t, n), dt), pl.BlockSpec((tm, n), lambda i: (i, 0)))
    outs = [row(d_conv, F32), row(d_attn, F32), row(d_attn, F32)]
    if per_token:
        outs.append(row(d_attn, F32))
    else:
        outs.append(row(d_attn, BF16))
        tr = (jax.ShapeDtypeStruct((nt, d_attn, tm), BF16), pl.BlockSpec((1, d_attn, tm), lambda i: (i, 0, 0)))
        outs += [tr, tr]
    return pl.pallas_call(
        functools.partial(_inproj_kernel, per_token_mod=per_token, d_conv=d_conv, d_attn=d_attn,
                          transposed=not per_token),
        grid=(nt,),
        in_specs=[pl.BlockSpec((tm, d), lambda i: (i, 0)), mod_spec, _full((d, n_in)), _full((1, n_in))],
        out_specs=[o[1] for o in outs],
        out_shape=[o[0] for o in outs],
        compiler_params=_params("arbitrary"),
        name="inproj_sample" if per_token else "inproj_prompt",
    )(x, mod, w_in, b_in)


CONV_PAD = 32


def _conv_post(acc, cb_ref, g_ref, b_ref):
    y = _ln(acc + cb_ref[...]) * g_ref[...] + b_ref[...]
    return y * jax.nn.sigmoid(y)


def _conv_prompt_kernel(glu_ref, cw_ref, cb_ref, g_ref, b_ref, y_ref, full_ref, *, ts, width, chunk):
    s = pl.program_id(1)
    hist = width - 1

    @pl.when(s == 0)
    def _():
        full_ref[0:CONV_PAD, :] = jnp.zeros((CONV_PAD, full_ref.shape[1]), F32)

    @pl.when(s > 0)
    def _():
        full_ref[0:CONV_PAD, :] = full_ref[ts:ts + CONV_PAD, :]

    full_ref[CONV_PAD:CONV_PAD + ts, :] = glu_ref[...]
    base = CONV_PAD - hist
    for c in range(ts // chunk):
        r0 = c * chunk
        acc = full_ref[base + r0:base + r0 + chunk, :] * cw_ref[0:1, :]
        for w in range(1, width):
            acc = acc + full_ref[base + r0 + w:base + r0 + w + chunk, :] * cw_ref[w:w + 1, :]
        y_ref[r0:r0 + chunk, :] = _conv_post(acc, cb_ref, g_ref, b_ref).astype(y_ref.dtype)


def _conv_prompt(glu, conv_w, conv_b, ln_g, ln_b, *, batch, seq, ts):
    t, dc = glu.shape
    width = conv_w.shape[0]
    ns = seq // ts
    vec = lambda a: a.reshape(1, dc)
    return pl.pallas_call(
        functools.partial(_conv_prompt_kernel, ts=ts, width=width, chunk=min(64, ts)),
        grid=(batch, ns),
        in_specs=[pl.BlockSpec((ts, dc), lambda b, s: (b * ns + s, 0)),
                  _full((width, dc)), _full((1, dc)), _full((1, dc)), _full((1, dc))],
        out_specs=pl.BlockSpec((ts, dc), lambda b, s: (b * ns + s, 0)),
        out_shape=jax.ShapeDtypeStruct((t, dc), BF16),
        scratch_shapes=[pltpu.VMEM((CONV_PAD + ts, dc), F32)],
        compiler_params=_params("arbitrary", "arbitrary"),
        name="conv_prompt",
    )(glu, conv_w, vec(conv_b), vec(ln_g), vec(ln_b))


def _conv_sample_kernel(st_ref, glu_ref, cw_ref, cb_ref, g_ref, b_ref, y_ref, *, width):
    hist = width - 1
    acc = glu_ref[...] * cw_ref[hist:hist + 1, :]
    for w in range(hist):
        acc = acc + st_ref[w] * cw_ref[w:w + 1, :]
    y_ref[...] = _conv_post(acc, cb_ref, g_ref, b_ref).astype(y_ref.dtype)


def _conv_sample(state_t, glu, conv_w, conv_b, ln_g, ln_b):
    hist, nb, dc = state_t.shape
    width = conv_w.shape[0]
    vec = lambda a: a.reshape(1, dc)
    return pl.pallas_call(
        functools.partial(_conv_sample_kernel, width=width),
        grid=(1,),
        in_specs=[_full((hist, nb, dc)), _full((nb, dc)), _full((width, dc)),
                  _full((1, dc)), _full((1, dc)), _full((1, dc))],
        out_specs=_full((nb, dc)),
        out_shape=jax.ShapeDtypeStruct((nb, dc), BF16),
        compiler_params=_params("arbitrary"),
        name="conv_sample",
    )(state_t, glu, conv_w, vec(conv_b), vec(ln_g), vec(ln_b))


def _bucket(n):
    max_exact = N_BUCKETS // 2
    nf = jnp.maximum(n, 1).astype(F32)
    large = max_exact + (jnp.log(nf / max_exact) / math.log(MAX_DISTANCE / max_exact)
                         * (N_BUCKETS - max_exact)).astype(jnp.int32)
    large = jnp.minimum(large, N_BUCKETS - 1)
    return jnp.where(n < max_exact, n, large)


def _bias_of(n, rb_ref, h):
    bucket = _bucket(n)
    far = jnp.full(n.shape, rb_ref[N_BUCKETS - 1, h], F32)
    out = far
    for j in range(N_BUCKETS - 1):
        out = jnp.where(bucket == j, rb_ref[j, h], out)
    return out - far


def _bias_kernel(rb_ref, o_ref, *, ts):
    h = pl.program_id(0)
    d = pl.program_id(1)
    ik = lax.broadcasted_iota(jnp.int32, (ts, ts), 0)
    jq = lax.broadcasted_iota(jnp.int32, (ts, ts), 1)
    n = jq - ik + ts * (1 - d)
    b = _bias_of(jnp.maximum(n, 0), rb_ref, h)
    o_ref[0, 0] = jnp.where(n >= 0, b, NEG_INF)


def _bias_tiles(rel_bias, *, heads, ts):
    return pl.pallas_call(
        functools.partial(_bias_kernel, ts=ts),
        grid=(heads, 2),
        in_specs=[pl.BlockSpec(memory_space=pltpu.SMEM)],
        out_specs=pl.BlockSpec((1, 1, ts, ts), lambda h, d: (h, d, 0, 0)),
        out_shape=jax.ShapeDtypeStruct((heads, 2, ts, ts), F32),
        compiler_params=_params("arbitrary", "arbitrary"),
        name="bias_tiles",
    )(rel_bias)


def _lambda(lq1, lk1, lq2, lk2, lambda_init):
    s1 = jnp.sum(lq1[...] * lk1[...], axis=-1, keepdims=True)
    s2 = jnp.sum(lq2[...] * lk2[...], axis=-1, keepdims=True)
    return jnp.exp(s1) - jnp.exp(s2) + lambda_init


def _attn_kernel(qt_ref, k_ref, vt_ref, bias_ref, g_ref, lq1, lk1, lq2, lk2, o_ref,
                 qs_ref, m_ref, l_ref, acc_ref, sa_ref, sb_ref, *, ts, lambda_init):
    qi = pl.program_id(2)
    d = ATT_HEAD_DIM
    qt = qt_ref[0]
    row = lax.broadcasted_iota(jnp.int32, qt.shape, 0)
    zero = jnp.zeros_like(qt)
    qs_ref[:, :ts] = jnp.where(row < d, qt, zero)
    qs_ref[:, ts:] = jnp.where(row >= d, qt, zero)
    m_ref[...] = jnp.full(m_ref.shape, NEG_INF, F32)
    l_ref[...] = jnp.zeros(l_ref.shape, F32)
    acc_ref[...] = jnp.zeros(acc_ref.shape, F32)

    def scores(ki, bias):
        kblk = k_ref[pl.ds(pl.multiple_of(ki * ts, ts), ts), :]
        s = jnp.dot(kblk, qs_ref[...], preferred_element_type=F32)
        if bias is not None:
            s = s + jnp.concatenate([bias, bias], axis=1)
        return s

    def update(s_ref, ki):
        s = s_ref[...]
        m_prev = m_ref[...]
        m_new = jnp.maximum(m_prev, jnp.max(s, axis=0, keepdims=True))
        alpha = jnp.exp(m_prev - m_new)
        p = jnp.exp(s - m_new)
        l_ref[...] = alpha * l_ref[...] + jnp.sum(p, axis=0, keepdims=True)
        acc_ref[...] = alpha * acc_ref[...] + jnp.dot(vt_ref[ki], p.astype(BF16), preferred_element_type=F32)
        m_ref[...] = m_new

    n_far = jnp.maximum(qi - 1, 0)
    sa_ref[...] = scores(qi, bias_ref[0, 1])

    @pl.when(qi > 0)
    def _():
        sb_ref[...] = scores(qi - 1, bias_ref[0, 0])
        update(sa_ref, qi)

    def far_pair(jj, carry):
        ka = qi - 2 - 2 * jj
        sa_ref[...] = scores(ka, None)
        update(sb_ref, ka + 1)
        sb_ref[...] = scores(ka - 1, None)
        update(sa_ref, ka)
        return carry

    lax.fori_loop(0, n_far // 2, far_pair, 0)

    @pl.when(n_far % 2 == 1)
    def _():
        sa_ref[...] = scores(0, None)
        update(sb_ref, 1)

    @pl.when(qi % 2 == 0)
    def _():
        update(sa_ref, 0)

    @pl.when(qi % 2 == 1)
    def _():
        update(sb_ref, 0)

    lam = _lambda(lq1, lk1, lq2, lk2, lambda_init)
    inv_l = 1.0 / l_ref[...]
    acc = acc_ref[...]
    o = acc[:, :ts] * inv_l[:, :ts] - lam * (acc[:, ts:] * inv_l[:, ts:])
    o = o * lax.rsqrt(jnp.mean(o * o, axis=0, keepdims=True) + LN_EPS)
    o = o * g_ref[...] * (1.0 - lambda_init)
    o_ref[...] = o.T.astype(o_ref.dtype)


def _attn_prompt(qt, kb, vt, bias, subln_g, lams, *, batch, seq, heads, ts, lambda_init):
    t, d_attn = kb.shape
    hd = 2 * ATT_HEAD_DIM
    nq = seq // ts
    lam_spec = _full((1, ATT_HEAD_DIM))
    return pl.pallas_call(
        functools.partial(_attn_kernel, ts=ts, lambda_init=lambda_init),
        grid=(batch, heads, nq),
        in_specs=[pl.BlockSpec((1, hd, ts), lambda b, h, q: (b * nq + q, h, 0)),
                  pl.BlockSpec((seq, hd), lambda b, h, q: (b, h)),
                  pl.BlockSpec((nq, hd, ts), lambda b, h, q: (b, h, 0)),
                  pl.BlockSpec((1, 2, ts, ts), lambda b, h, q: (h, 0, 0, 0)),
                  _full((hd, 1)), lam_spec, lam_spec, lam_spec, lam_spec],
        out_specs=pl.BlockSpec((ts, hd), lambda b, h, q: (b * nq + q, h)),
        out_shape=jax.ShapeDtypeStruct((t, d_attn), BF16),
        scratch_shapes=[pltpu.VMEM((hd, 2 * ts), BF16), pltpu.VMEM((1, 2 * ts), F32),
                        pltpu.VMEM((1, 2 * ts), F32), pltpu.VMEM((hd, 2 * ts), F32),
                        pltpu.VMEM((ts, 2 * ts), F32), pltpu.VMEM((ts, 2 * ts), F32)],
        compiler_params=_params("arbitrary", "arbitrary", "arbitrary"),
        name="attn_prompt",
    )(qt, kb, vt, bias, subln_g.reshape(hd, 1), *lams)


def _decode_kernel(pt_ref, rb_ref, q_ref, kn_ref, vn_ref, g_ref, lq1, lk1, lq2, lk2, *rest,
                   n_pages, page, heads, lambda_init):
    k_refs = rest[:n_pages]
    v_refs = rest[n_pages:2 * n_pages]
    o_ref = rest[2 * n_pages]
    s_ref, p_ref, bias_ref = rest[2 * n_pages + 1:]
    d = ATT_HEAD_DIM
    hd = 2 * d
    rows = page * heads
    past = n_pages * rows
    half2 = LANES // 2
    log_heads = int(math.log2(heads))

    q4 = q_ref[0] * (d ** -0.5)
    r = lax.broadcasted_iota(jnp.int32, (LANES, hd), 0)
    c = lax.broadcasted_iota(jnp.int32, (LANES, hd), 1)
    qrows = jnp.zeros((LANES, hd), F32)
    for h in range(heads):
        qh = jnp.broadcast_to(q4[h:h + 1, :], (LANES, hd))
        qrows = jnp.where((r == h) & (c < d), qh, qrows)
        qrows = jnp.where((r == half2 + h) & (c >= d), qh, qrows)
    qcols = qrows.T.astype(BF16)

    def lane_head(shape):
        lane = lax.broadcasted_iota(jnp.int32, shape, 1)
        return jnp.where(lane < half2, lane, lane - half2)

    def row_head(shape):
        return lax.broadcasted_iota(jnp.int32, shape, 0) & (heads - 1)

    @pl.when(pl.program_id(0) == 0)
    def _():
        key = lax.shift_right_logical(lax.broadcasted_iota(jnp.int32, (rows, LANES), 0), log_heads)
        lh = lane_head((rows, LANES))
        lh8 = lane_head((8, LANES))
        bias = jnp.zeros((rows, LANES), F32)
        bias_new = jnp.zeros((8, LANES), F32)
        for h in range(heads):
            bias = jnp.where(lh == h, _bias_of(page - key, rb_ref, h), bias)
            bias_new = jnp.where(lh8 == h, _bias_of(jnp.zeros((8, LANES), jnp.int32), rb_ref, h), bias_new)
        bias_ref[0:rows, :] = bias
        bias_ref[rows:rows + 8, :] = bias_new

    own = row_head((rows, LANES)) == lane_head((rows, LANES))
    for j in range(n_pages):
        sj = jnp.dot(k_refs[j][0].astype(BF16), qcols, preferred_element_type=F32)
        if j == n_pages - 1:
            sj = sj + bias_ref[0:rows, :]
        s_ref[j * rows:(j + 1) * rows, :] = jnp.where(own, sj, NEG_INF)
    kn = jnp.concatenate([kn_ref[0], jnp.zeros((8 - heads, hd), F32)], axis=0).astype(BF16)
    s_new = jnp.dot(kn, qcols, preferred_element_type=F32)
    r8 = lax.broadcasted_iota(jnp.int32, (8, LANES), 0)
    s_ref[past:past + 8, :] = jnp.where(r8 == lane_head((8, LANES)), s_new + bias_ref[rows:rows + 8, :], NEG_INF)

    s = s_ref[...]
    m = jnp.max(s, axis=0, keepdims=True)
    p = jnp.exp(s - m)
    l = jnp.sum(p, axis=0, keepdims=True)
    lam = _lambda(lq1, lk1, lq2, lk2, lambda_init)
    lane1 = lax.broadcasted_iota(jnp.int32, (1, LANES), 1)
    live = (lane1 < heads) | ((lane1 >= half2) & (lane1 < half2 + heads))
    coef = jnp.where(live, jnp.where(lane1 < half2, 1.0 / l, -lam / l), 0.0)
    p_ref[...] = (p * coef).astype(BF16)

    ones = jnp.ones((LANES, hd), BF16)
    acc = jnp.zeros((rows, hd), F32)
    for j in range(n_pages):
        wrep = jnp.dot(p_ref[j * rows:(j + 1) * rows, :], ones, preferred_element_type=F32)
        acc = acc + wrep * v_refs[j][0]
    acc8 = acc[0:8, :]
    for i in range(1, rows // 8):
        acc8 = acc8 + acc[i * 8:(i + 1) * 8, :]
    w_new = jnp.dot(p_ref[past:past + 8, :], ones, preferred_element_type=F32)
    vn = jnp.concatenate([vn_ref[0], jnp.zeros((8 - heads, hd), F32)], axis=0)
    o = acc8[0:heads, :] + acc8[heads:2 * heads, :] + (w_new * vn)[0:heads, :]
    o = o * lax.rsqrt(jnp.mean(o * o, axis=-1, keepdims=True) + LN_EPS)
    o_ref[0] = (o * g_ref[...] * (1.0 - lambda_init)).astype(o_ref.dtype)


def _attn_sample(page_table, rel_bias, q, k_new, v_new, cache_k, cache_v, subln_g, lams, *, heads, lambda_init):
    nb, n_pages = page_table.shape
    n_pool, rows, hd = cache_k.shape
    page = rows // heads
    assert heads & (heads - 1) == 0 and 2 * heads <= 8
    tok = lambda a: a.reshape(nb, heads, hd)
    tok_spec = pl.BlockSpec((1, heads, hd), lambda b, pt: (b, 0, 0))
    lam_spec = pl.BlockSpec((1, ATT_HEAD_DIM), lambda b, pt: (0, 0))
    page_specs = [pl.BlockSpec((1, rows, hd), lambda b, pt, j=j: (pt[b, j], 0, 0)) for j in range(n_pages)]
    grid_spec = pltpu.PrefetchScalarGridSpec(
        num_scalar_prefetch=1,
        grid=(nb,),
        in_specs=[pl.BlockSpec(memory_space=pltpu.SMEM), tok_spec, tok_spec, tok_spec,
                  pl.BlockSpec((1, hd), lambda b, pt: (0, 0)),
                  lam_spec, lam_spec, lam_spec, lam_spec] + page_specs + page_specs,
        out_specs=tok_spec,
        scratch_shapes=[pltpu.VMEM((n_pages * rows + 8, LANES), F32),
                        pltpu.VMEM((n_pages * rows + 8, LANES), BF16),
                        pltpu.VMEM((rows + 8, LANES), F32)],
    )
    out = pl.pallas_call(
        functools.partial(_decode_kernel, n_pages=n_pages, page=page, heads=heads, lambda_init=lambda_init),
        grid_spec=grid_spec,
        out_shape=jax.ShapeDtypeStruct((nb, heads, hd), BF16),
        compiler_params=_params("arbitrary"),
        name="attn_sample",
    )(page_table, rel_bias, tok(q), tok(k_new), tok(v_new), subln_g.reshape(1, hd), *lams,
      *([cache_k] * n_pages), *([cache_v] * n_pages))
    return out.reshape(nb, heads * hd)


def _mix_kernel(x_ref, cy_ref, ay_ref, mod_ref, wo_ref, bo_ref, g1_ref, b1_ref, wq_ref, sk_ref,
                x1_ref, h2t_ref, st_ref, *, per_token_mod, alpha, d_conv):
    if per_token_mod:
        gate1, sh2, sc2 = mod_ref[2], mod_ref[3], mod_ref[4]
    else:
        gate1, sh2, sc2 = mod_ref[2, 0], mod_ref[3, 0], mod_ref[4, 0]
    mix = (jnp.dot(cy_ref[...], wo_ref[:d_conv, :], preferred_element_type=F32)
           + jnp.dot(ay_ref[...], wo_ref[d_conv:, :], preferred_element_type=F32) + bo_ref[...])
    x1 = _ln(alpha * x_ref[...] + gate1 * mix) * g1_ref[...] + b1_ref[...]
    x1_ref[...] = x1
    h2 = _ln(x1) * (1.0 + sc2) + sh2
    h2b = h2.astype(BF16)
    h2t_ref[...] = h2.T.astype(BF16)
    qh = jnp.dot(h2b, wq_ref[...], preferred_element_type=F32).astype(BF16)
    nk = sk_ref.shape[2]
    for hc in range(sk_ref.shape[0]):
        st_ref[hc] = lax.dot_general(sk_ref[hc], qh[:, hc * nk:(hc + 1) * nk], (((1,), (1,)), ((), ())),
                                     preferred_element_type=F32)


def _mix(x, cy, ay, mod, w_out, b_out, ln_g, ln_b, w_query, sub_keys, *, tm, rows_per_mod, alpha):
    t, d = x.shape
    d_conv = cy.shape[1]
    d_attn = ay.shape[1]
    n_hc, n_keys, half = sub_keys.shape
    per_token = rows_per_mod is None
    if per_token:
        mod_spec = pl.BlockSpec((6, tm, d), lambda i: (0, i, 0))
    else:
        mod_spec = pl.BlockSpec((6, 1, 1, d), lambda i: (0, (i * tm) // rows_per_mod, 0, 0))
    vec = lambda a: a.reshape(1, d)
    return pl.pallas_call(
        functools.partial(_mix_kernel, per_token_mod=per_token, alpha=alpha, d_conv=d_conv),
        grid=(t // tm,),
        in_specs=[pl.BlockSpec((tm, d), lambda i: (i, 0)),
                  pl.BlockSpec((tm, d_conv), lambda i: (i, 0)),
                  pl.BlockSpec((tm, d_attn), lambda i: (i, 0)),
                  mod_spec, _full(w_out.shape), _full((1, d)), _full((1, d)), _full((1, d)),
                  _full(w_query.shape), _full(sub_keys.shape)],
        out_specs=[pl.BlockSpec((tm, d), lambda i: (i, 0)),
                   pl.BlockSpec((d, tm), lambda i: (0, i)),
                   pl.BlockSpec((n_hc, n_keys, tm), lambda i: (0, 0, i))],
        out_shape=[jax.ShapeDtypeStruct((t, d), F32),
                   jax.ShapeDtypeStruct((d, t), BF16),
                   jax.ShapeDtypeStruct((n_hc, n_keys, t), F32)],
        compiler_params=_params("arbitrary"),
        name="mix_sample" if per_token else "mix_prompt",
    )(x, cy, ay, mod, w_out, vec(b_out), vec(ln_g), vec(ln_b), w_query, sub_keys)


def _top_ranks(s):
    work = s
    rank = jnp.full(s.shape, float(PEER_TOPK), F32)
    vals = []
    for r in range(PEER_TOPK):
        m = jnp.max(work, axis=0, keepdims=True)
        hit = work == m
        rank = jnp.where(hit, float(r), rank)
        work = jnp.where(hit, -jnp.inf, work)
        vals.append(m)
    return rank, vals


def _route_kernel(s_ref, r2_ref, e2_ref, n1_ref, e1_ref):
    k = PEER_TOPK
    s1 = s_ref[0]
    s2 = s_ref[1]
    rank1, v1 = _top_ranks(s1)
    rank2, v2 = _top_ranks(s2)
    rowk = lax.broadcasted_iota(jnp.int32, (k,) + s1.shape[1:], 0)

    def stack(vals):
        out = jnp.zeros(rowk.shape, F32)
        for r in range(k):
            out = jnp.where(rowk == r, vals[r], out)
        return out

    v1m = stack(v1)
    v2m = stack(v2)
    row8 = lax.broadcasted_iota(jnp.int32, (8,) + s1.shape[1:], 0)
    cands = [v1[0] + v2m]
    for a in range(1, 8):
        cands.append(jnp.where(row8 < k // (a + 1), v1[a] + v2m[0:8], -jnp.inf))
    cands.append(v1m[8:k] + v2[0])
    work = jnp.concatenate(cands, axis=0)
    thr = None
    for _ in range(k):
        thr = jnp.max(work, axis=0, keepdims=True)
        work = jnp.where(work == thr, -jnp.inf, work)
    e2top = jnp.exp(v2m - v2[0])
    z = jnp.zeros_like(thr)
    n1 = jnp.zeros(s1.shape, F32)
    for a in range(k):
        sel = (v1[a] + v2m) >= thr
        cnt = jnp.sum(jnp.where(sel, 1.0, 0.0), axis=0, keepdims=True)
        z = z + jnp.exp(v1[a] - v1[0]) * jnp.sum(jnp.where(sel, e2top, 0.0), axis=0, keepdims=True)
        n1 = jnp.where(rank1 == float(a), cnt, n1)
    r2_ref[0] = rank2.astype(r2_ref.dtype)
    n1_ref[0] = _bf16_pair(n1)
    e1_ref[0] = _bf16_pair(jnp.where(rank1 < float(k), jnp.exp(s1 - v1[0]) / z, 0.0))
    e2_ref[0] = jnp.where(rank2 < float(k), jnp.exp(s2 - v2[0]), 0.0).astype(e2_ref.dtype)


def _bf16_pair(x):
    bits = pltpu.bitcast(x.astype(BF16).astype(F32), jnp.uint32)
    return bits | lax.shift_right_logical(bits, jnp.uint32(16))


def _route(st, *, tl):
    n_hc, n_keys, t = st.shape
    heads = n_hc // 2
    out = lambda dt: jax.ShapeDtypeStruct((heads, n_keys, t), dt)
    spec = pl.BlockSpec((1, n_keys, tl), lambda i, h: (h, 0, i))
    return pl.pallas_call(
        _route_kernel,
        grid=(t // tl, heads),
        in_specs=[pl.BlockSpec((2, n_keys, tl), lambda i, h: (h, 0, i))],
        out_specs=[spec] * 4,
        out_shape=[out(BF16), out(BF16), out(jnp.uint32), out(jnp.uint32)],
        compiler_params=_params("arbitrary", "arbitrary"),
        name="route",
    )(st)


GELU_C1 = math.sqrt(2.0 / math.pi)
GELU_C2 = GELU_C1 * 0.044715


def _gelu_tanh(x):
    hx = 0.5 * x
    return hx + hx * jnp.tanh(x * (GELU_C1 + GELU_C2 * (x * x)))


def _packed_rows(words):
    return pltpu.bitcast(jnp.broadcast_to(words, (8, words.shape[1])), BF16)


def _peer_kernel(h2t_ref, u_ref, vt_ref, r2_ref, e2_ref, n1_ref, e1_ref, x1_ref, mod_ref, g_ref, b_ref, y_ref,
                 at_ref, zt_ref, acc_ref, *, per_token_mod, alpha, n_keys, chunk):
    j = pl.program_id(1)
    ek, tq = at_ref.shape
    groups = ek // n_keys
    heads = r2_ref.shape[0]
    sub = 16

    @pl.when(j == 0)
    def _():
        acc_ref[...] = jnp.zeros(acc_ref.shape, F32)

    for c in range(ek // chunk):
        lo = c * chunk
        at_ref[lo:lo + chunk, :] = jnp.dot(u_ref[lo:lo + chunk, :], h2t_ref[...], preferred_element_type=F32)
        for g in range(chunk // n_keys):
            i1 = j * groups + c * (chunk // n_keys) + g
            n1b = [_packed_rows(n1_ref[h, pl.ds(i1, 1), :]) for h in range(heads)]
            e1b = [_packed_rows(e1_ref[h, pl.ds(i1, 1), :]) for h in range(heads)]
            for k in range(n_keys // sub):
                r0 = lo + g * n_keys + k * sub
                keys = slice(k * sub, (k + 1) * sub)
                act = _gelu_tanh(at_ref[r0:r0 + sub, :].astype(BF16))
                w = jnp.zeros((sub, tq), BF16)
                for h in range(heads):
                    w = jnp.where(r2_ref[h, keys, :] < n1b[h], w + e2_ref[h, keys, :] * e1b[h], w)
                zt_ref[r0:r0 + sub, :] = w * act
        acc_ref[...] += jnp.dot(vt_ref[:, lo:lo + chunk], zt_ref[lo:lo + chunk, :], preferred_element_type=F32)

    @pl.when(j == pl.num_programs(1) - 1)
    def _():
        gate2 = mod_ref[5] if per_token_mod else mod_ref[5, 0]
        ff = acc_ref[...].T
        y_ref[...] = _ln(alpha * x1_ref[...] + gate2 * ff) * g_ref[...] + b_ref[...]


def _peer(h2t, u_b, vt_b, r2, e2, n1, e1, x1, mod, ln_g, ln_b, *, tq, ek, rows_per_mod, alpha):
    d, t = h2t.shape
    ne = u_b.shape[0]
    heads, n_keys, _ = r2.shape
    per_token = rows_per_mod is None
    if per_token:
        mod_spec = pl.BlockSpec((6, tq, d), lambda i, j: (0, i, 0))
    else:
        mod_spec = pl.BlockSpec((6, 1, 1, d), lambda i, j: (0, (i * tq) // rows_per_mod, 0, 0))
    tab = pl.BlockSpec((heads, n_keys, tq), lambda i, j: (0, 0, i))
    vec = lambda a: a.reshape(1, d)
    return pl.pallas_call(
        functools.partial(_peer_kernel, per_token_mod=per_token, alpha=alpha, n_keys=n_keys,
                          chunk=min(PEER_EXPERT_CHUNK, ek)),
        grid=(t // tq, ne // ek),
        in_specs=[pl.BlockSpec((d, tq), lambda i, j: (0, i)),
                  pl.BlockSpec((ek, d), lambda i, j: (j, 0)),
                  pl.BlockSpec((d, ek), lambda i, j: (0, j)),
                  tab, tab, tab, tab,
                  pl.BlockSpec((tq, d), lambda i, j: (i, 0)),
                  mod_spec, _full((1, d)), _full((1, d))],
        out_specs=pl.BlockSpec((tq, d), lambda i, j: (i, 0)),
        out_shape=jax.ShapeDtypeStruct((t, d), F32),
        scratch_shapes=[pltpu.VMEM((ek, tq), F32), pltpu.VMEM((ek, tq), BF16), pltpu.VMEM((d, tq), F32)],
        compiler_params=_params("arbitrary", "arbitrary"),
        name="peer_sample" if per_token else "peer_prompt",
    )(h2t, u_b, vt_b, r2, e2, n1, e1, x1, mod, vec(ln_g), vec(ln_b))


def _transpose_kernel(x_ref, o_ref):
    o_ref[...] = x_ref[...].T.astype(o_ref.dtype)


def _transpose_bf16(x, *, rows):
    n, d = x.shape
    return pl.pallas_call(
        _transpose_kernel,
        grid=(n // rows,),
        in_specs=[pl.BlockSpec((rows, d), lambda i: (i, 0))],
        out_specs=pl.BlockSpec((d, rows), lambda i: (0, i)),
        out_shape=jax.ShapeDtypeStruct((d, n), BF16),
        compiler_params=_params("arbitrary"),
        name="transpose_v",
    )(x)


def kernel(x_prompt, x_sample, cache_k, cache_v, state_conv, page_table, c_prompt, c_sample, w_ada, b_ada, w_in, b_in, conv_w, conv_b, conv_ln_g, conv_ln_b, lambda_q1, lambda_k1, lambda_q2, lambda_k2, attn_subln_g, rel_bias, w_out, b_out, ln1_g, ln1_b, peer_w_query, peer_sub_keys, peer_u, peer_v, ln2_g, ln2_b):
    batch, seq, d = x_prompt.shape
    nb, dec_seq, _ = x_sample.shape
    depth = w_ada.shape[0]
    assert depth == 1 and dec_seq == 1
    heads = cache_k.shape[3]
    hd = cache_k.shape[4]
    assert hd == 2 * ATT_HEAD_DIM
    d_attn = heads * hd
    d_conv = conv_w.shape[2]
    hist = conv_w.shape[1] - 1
    assert hist <= CONV_PAD - 1
    page = cache_k.shape[2]
    n_keys = peer_sub_keys.shape[3]
    ne = peer_u.shape[1]
    alpha = (2 * depth) ** 0.25
    lambda_init = 0.8 - 0.6 * math.exp(-0.3 * 0)
    t_p = batch * seq
    ts = min(TOKEN_TILE, seq)
    tq_peer = min(PEER_TOKEN_TILE, seq)
    ek = min(PEER_EXPERT_TILE, ne)
    assert seq % ts == 0 and nb % LANES == 0 and page >= MAX_DISTANCE

    l = 0
    n_c = batch + nb
    n_c_pad = -(-n_c // 8) * 8
    c_all = jnp.concatenate([c_prompt, c_sample, jnp.zeros((n_c_pad - n_c, d), F32)], axis=0)
    mod = _ada(c_all, w_ada[l], b_ada[l])
    mod_p = mod[:, :batch].reshape(6, batch, 1, d)
    mod_s = mod[:, batch:n_c]

    w_in_b = w_in[l].astype(BF16)
    b_in_r = b_in[l].reshape(1, -1)
    w_out_b = w_out[l].astype(BF16)
    wq_b = peer_w_query[l].astype(BF16)
    sk_b = peer_sub_keys[l].reshape(-1, n_keys, peer_sub_keys.shape[-1]).astype(BF16)
    u_b = peer_u[l].astype(BF16)
    vt_b = _transpose_bf16(peer_v[l], rows=min(512, ne))
    lams = [a[l].reshape(1, -1) for a in (lambda_q1, lambda_k1, lambda_q2, lambda_k2)]
    subln_g = attn_subln_g[l]

    xp = x_prompt.reshape(t_p, d)
    glu_p, k_p, v_p, kb_p, qt_p, vt_p = _inproj(xp, mod_p, w_in_b, b_in_r, tm=ts, rows_per_mod=seq,
                                                d_conv=d_conv, d_attn=d_attn)
    cy_p = _conv_prompt(glu_p, conv_w[l], conv_b[l], conv_ln_g[l], conv_ln_b[l], batch=batch, seq=seq, ts=ts)
    bias = _bias_tiles(rel_bias, heads=heads, ts=ts)
    ay_p = _attn_prompt(qt_p, kb_p, vt_p, bias, subln_g, lams, batch=batch, seq=seq, heads=heads, ts=ts,
                        lambda_init=lambda_init)
    x1_p, h2t_p, st_p = _mix(xp, cy_p, ay_p, mod_p, w_out_b, b_out[l], ln1_g[l], ln1_b[l], wq_b, sk_b,
                             tm=ts, rows_per_mod=seq, alpha=alpha)
    r2, e2, n1, e1 = _route(st_p, tl=min(ROUTE_TOKEN_TILE, t_p))
    y_p = _peer(h2t_p, u_b, vt_b, r2, e2, n1, e1, x1_p, mod_p, ln2_g[l], ln2_b[l],
                tq=tq_peer, ek=ek, rows_per_mod=seq, alpha=alpha)

    xs = x_sample.reshape(nb, d)
    glu_s, k_s, v_s, q_s = _inproj(xs, mod_s, w_in_b, b_in_r, tm=nb, rows_per_mod=None,
                                   d_conv=d_conv, d_attn=d_attn)
    state = state_conv[l]
    cy_s = _conv_sample(jnp.swapaxes(state, 0, 1), glu_s, conv_w[l], conv_b[l], conv_ln_g[l], conv_ln_b[l])
    n_pool = cache_k.shape[1]
    pool_rows = lambda a: a.reshape(depth * n_pool, page * heads, hd)
    ay_s = _attn_sample(page_table + l * n_pool, rel_bias, q_s, k_s, v_s, pool_rows(cache_k), pool_rows(cache_v),
                        subln_g, lams, heads=heads, lambda_init=lambda_init)
    x1_s, h2t_s, st_s = _mix(xs, cy_s, ay_s, mod_s, w_out_b, b_out[l], ln1_g[l], ln1_b[l], wq_b, sk_b,
                             tm=nb, rows_per_mod=None, alpha=alpha)
    r2s, e2s, n1s, e1s = _route(st_s, tl=nb)
    y_s = _peer(h2t_s, u_b, vt_b, r2s, e2s, n1s, e1s, x1_s, mod_s, ln2_g[l], ln2_b[l],
                tq=nb, ek=ek, rows_per_mod=None, alpha=alpha)

    kv_p = (depth, batch, seq, heads, hd)
    kv_s = (depth, nb, dec_seq, heads, hd)
    conv_p = glu_p.reshape(batch, seq, d_conv)[:, seq - hist:][None]
    conv_s = jnp.concatenate([state[:, 1:], glu_s[:, None, :]], axis=1)[None]
    return (y_p.reshape(batch, seq, d), y_s.reshape(nb, dec_seq, d),
            k_p.reshape(kv_p), v_p.reshape(kv_p), conv_p,
            k_s.reshape(kv_s), v_s.reshape(kv_s), conv_s)
```

```python
import functools
import math

import jax
import jax.numpy as jnp
from jax import lax
from jax.experimental import pallas as pl
from jax.experimental.pallas import tpu as pltpu

F32 = jnp.float32
BF16 = jnp.bfloat16

LN_EPS = 1e-5
NEG_INF = -1e30
ATT_HEAD_DIM = 64
N_BUCKETS = 32
MAX_DISTANCE = 128
PEER_TOPK = 16
LANES = 128
VMEM_LIMIT_BYTES = 56 * 1024 * 1024
TOKEN_TILE = 512
PEER_TOKEN_TILE = 512
PEER_EXPERT_TILE = 2048
PEER_EXPERT_SLAB = 256
ROUTE_TOKEN_TILE = 256


def _params(*sem):
    return pltpu.CompilerParams(dimension_semantics=sem, vmem_limit_bytes=VMEM_LIMIT_BYTES)


def _ln(x):
    mu = jnp.mean(x, axis=-1, keepdims=True)
    xc = x - mu
    var = jnp.mean(xc * xc, axis=-1, keepdims=True)
    return xc * lax.rsqrt(var + LN_EPS)


def _full(shape):
    return pl.BlockSpec(shape, lambda *_: (0,) * len(shape))


def _ada_kernel(c_ref, w_ref, b_ref, o_ref):
    c = c_ref[...]
    s = c * jax.nn.sigmoid(c)
    o_ref[0] = jnp.dot(s.astype(BF16), w_ref[...].astype(BF16), preferred_element_type=F32) + b_ref[0]


def _ada(c_all, w_ada, b_ada):
    nc, d = c_all.shape
    return pl.pallas_call(
        _ada_kernel,
        grid=(6,),
        in_specs=[_full((nc, d)),
                  pl.BlockSpec((d, d), lambda k: (0, k)),
                  pl.BlockSpec((1, 1, d), lambda k: (k, 0, 0))],
        out_specs=pl.BlockSpec((1, nc, d), lambda k: (k, 0, 0)),
        out_shape=jax.ShapeDtypeStruct((6, nc, d), F32),
        compiler_params=_params("arbitrary"),
        name="ada",
    )(c_all, w_ada, b_ada.reshape(6, 1, d))


def _inproj_kernel(x_ref, mod_ref, w_ref, b_ref, *outs, per_token_mod, d_conv, d_attn, transposed):
    x = x_ref[...]
    if per_token_mod:
        sh1, sc1 = mod_ref[0], mod_ref[1]
    else:
        sh1, sc1 = mod_ref[0, 0], mod_ref[1, 0]
    h = _ln(x) * (1.0 + sc1) + sh1
    z = jnp.dot(h.astype(BF16), w_ref[...], preferred_element_type=F32) + b_ref[...]
    ga = z[:, :d_conv]
    gb = z[:, d_conv:2 * d_conv]
    o = 2 * d_conv
    q = z[:, o:o + d_attn]
    k = z[:, o + d_attn:o + 2 * d_attn]
    v = z[:, o + 2 * d_attn:o + 3 * d_attn]
    glu = ga * jax.nn.sigmoid(gb)
    if transposed:
        glu_ref, k_ref, v_ref, kb_ref, qt_ref, vt_ref = outs
        kb_ref[...] = k.astype(BF16)
        qt_ref[0] = (q * (ATT_HEAD_DIM ** -0.5)).T.astype(BF16)
        vt_ref[0] = v.T.astype(BF16)
    else:
        glu_ref, k_ref, v_ref, q_ref = outs
        q_ref[...] = q
    glu_ref[...] = glu
    heads = d_attn // (2 * ATT_HEAD_DIM)
    for h in range(heads):
        cols = slice(h * 2 * ATT_HEAD_DIM, (h + 1) * 2 * ATT_HEAD_DIM)
        k_ref[pl.ds(h, x.shape[0], stride=heads), :] = k[:, cols]
        v_ref[pl.ds(h, x.shape[0], stride=heads), :] = v[:, cols]


def _inproj(x, mod, w_in, b_in, *, tm, rows_per_mod, d_conv, d_attn):
    t, d = x.shape
    n_in = w_in.shape[1]
    nt = t // tm
    per_token = rows_per_mod is None
    if per_token:
        mod_spec = pl.BlockSpec((6, tm, d), lambda i: (0, i, 0))
    else:
        mod_spec = pl.BlockSpec((6, 1, 1, d), lambda i: (0, (i * tm) // rows_per_mod, 0, 0))
    row = lambda n, dt: (jax.ShapeDtypeStruct((t, n), dt), pl.BlockSpec((tm, n), lambda i: (i, 0)))
    hd = 2 * ATT_HEAD_DIM
    heads = d_attn // hd
    head_rows = (jax.ShapeDtypeStruct((t * heads, hd), F32), pl.BlockSpec((tm * heads, hd), lambda i: (i, 0)))
    outs = [row(d_conv, F32), head_rows, head_rows]
    if per_token:
        outs.append(row(d_attn, F32))
    else:
        outs.append(row(d_attn, BF16))
        tr = (jax.ShapeDtypeStruct((nt, d_attn, tm), BF16), pl.BlockSpec((1, d_attn, tm), lambda i: (i, 0, 0)))
        outs += [tr, tr]
    return pl.pallas_call(
        functools.partial(_inproj_kernel, per_token_mod=per_token, d_conv=d_conv, d_attn=d_attn,
                          transposed=not per_token),
        grid=(nt,),
        in_specs=[pl.BlockSpec((tm, d), lambda i: (i, 0)), mod_spec, _full((d, n_in)), _full((1, n_in))],
        out_specs=[o[1] for o in outs],
        out_shape=[o[0] for o in outs],
        compiler_params=_params("arbitrary"),
        name="inproj_sample" if per_token else "inproj_prompt",
    )(x, mod, w_in, b_in)


CONV_PAD = 32


def _conv_post(acc, cb_ref, g_ref, b_ref):
    y = _ln(acc + cb_ref[...]) * g_ref[...] + b_ref[...]
    return y * jax.nn.sigmoid(y)


def _conv_prompt_kernel(glu_ref, cw_ref, cb_ref, g_ref, b_ref, y_ref, full_ref, *, ts, width, chunk):
    s = pl.program_id(1)
    hist = width - 1

    @pl.when(s == 0)
    def _():
        full_ref[0:CONV_PAD, :] = jnp.zeros((CONV_PAD, full_ref.shape[1]), F32)

    @pl.when(s > 0)
    def _():
        full_ref[0:CONV_PAD, :] = full_ref[ts:ts + CONV_PAD, :]

    full_ref[CONV_PAD:CONV_PAD + ts, :] = glu_ref[...]
    base = CONV_PAD - hist
    for c in range(ts // chunk):
        r0 = c * chunk
        acc = full_ref[base + r0:base + r0 + chunk, :] * cw_ref[0:1, :]
        for w in range(1, width):
            acc = acc + full_ref[base + r0 + w:base + r0 + w + chunk, :] * cw_ref[w:w + 1, :]
        y_ref[r0:r0 + chunk, :] = _conv_post(acc, cb_ref, g_ref, b_ref).astype(y_ref.dtype)


def _conv_prompt(glu, conv_w, conv_b, ln_g, ln_b, *, batch, seq, ts):
    t, dc = glu.shape
    width = conv_w.shape[0]
    ns = seq // ts
    vec = lambda a: a.reshape(1, dc)
    return pl.pallas_call(
        functools.partial(_conv_prompt_kernel, ts=ts, width=width, chunk=min(64, ts)),
        grid=(batch, ns),
        in_specs=[pl.BlockSpec((ts, dc), lambda b, s: (b * ns + s, 0)),
                  _full((width, dc)), _full((1, dc)), _full((1, dc)), _full((1, dc))],
        out_specs=pl.BlockSpec((ts, dc), lambda b, s: (b * ns + s, 0)),
        out_shape=jax.ShapeDtypeStruct((t, dc), BF16),
        scratch_shapes=[pltpu.VMEM((CONV_PAD + ts, dc), F32)],
        compiler_params=_params("arbitrary", "arbitrary"),
        name="conv_prompt",
    )(glu, conv_w, vec(conv_b), vec(ln_g), vec(ln_b))


def _conv_sample_kernel(st_ref, glu_ref, cw_ref, cb_ref, g_ref, b_ref, y_ref, *, width):
    hist = width - 1
    acc = glu_ref[...] * cw_ref[hist:hist + 1, :]
    for w in range(hist):
        acc = acc + st_ref[w] * cw_ref[w:w + 1, :]
    y_ref[...] = _conv_post(acc, cb_ref, g_ref, b_ref).astype(y_ref.dtype)


def _conv_sample(state_t, glu, conv_w, conv_b, ln_g, ln_b):
    hist, nb, dc = state_t.shape
    width = conv_w.shape[0]
    vec = lambda a: a.reshape(1, dc)
    return pl.pallas_call(
        functools.partial(_conv_sample_kernel, width=width),
        grid=(1,),
        in_specs=[_full((hist, nb, dc)), _full((nb, dc)), _full((width, dc)),
                  _full((1, dc)), _full((1, dc)), _full((1, dc))],
        out_specs=_full((nb, dc)),
        out_shape=jax.ShapeDtypeStruct((nb, dc), BF16),
        compiler_params=_params("arbitrary"),
        name="conv_sample",
    )(state_t, glu, conv_w, vec(conv_b), vec(ln_g), vec(ln_b))


def _bucket(n):
    max_exact = N_BUCKETS // 2
    nf = jnp.maximum(n, 1).astype(F32)
    large = max_exact + (jnp.log(nf / max_exact) / math.log(MAX_DISTANCE / max_exact)
                         * (N_BUCKETS - max_exact)).astype(jnp.int32)
    large = jnp.minimum(large, N_BUCKETS - 1)
    return jnp.where(n < max_exact, n, large)


def _bias_of(n, rb_ref, h):
    bucket = _bucket(n)
    far = jnp.full(n.shape, rb_ref[N_BUCKETS - 1, h], F32)
    out = far
    for j in range(N_BUCKETS - 1):
        out = jnp.where(bucket == j, rb_ref[j, h], out)
    return out - far


def _bias_kernel(rb_ref, o_ref, *, ts):
    h = pl.program_id(0)
    d = pl.program_id(1)
    ik = lax.broadcasted_iota(jnp.int32, (ts, ts), 0)
    jq = lax.broadcasted_iota(jnp.int32, (ts, ts), 1)
    n = jq - ik + ts * (1 - d)
    b = _bias_of(jnp.maximum(n, 0), rb_ref, h)
    o_ref[0, 0] = jnp.where(n >= 0, b, NEG_INF)


def _bias_tiles(rel_bias, *, heads, ts):
    return pl.pallas_call(
        functools.partial(_bias_kernel, ts=ts),
        grid=(heads, 2),
        in_specs=[pl.BlockSpec(memory_space=pltpu.SMEM)],
        out_specs=pl.BlockSpec((1, 1, ts, ts), lambda h, d: (h, d, 0, 0)),
        out_shape=jax.ShapeDtypeStruct((heads, 2, ts, ts), F32),
        compiler_params=_params("arbitrary", "arbitrary"),
        name="bias_tiles",
    )(rel_bias)


def _lambda(lq1, lk1, lq2, lk2, lambda_init):
    s1 = jnp.sum(lq1[...] * lk1[...], axis=-1, keepdims=True)
    s2 = jnp.sum(lq2[...] * lk2[...], axis=-1, keepdims=True)
    return jnp.exp(s1) - jnp.exp(s2) + lambda_init


def _attn_kernel(qt_ref, k_ref, vt_ref, bias_ref, g_ref, lq1, lk1, lq2, lk2, o_ref,
                 qs_ref, m_ref, l_ref, acc_ref, sa_ref, sb_ref, *, ts, lambda_init):
    qi = pl.program_id(2)
    d = ATT_HEAD_DIM
    qt = qt_ref[0]
    row = lax.broadcasted_iota(jnp.int32, qt.shape, 0)
    zero = jnp.zeros_like(qt)
    qs_ref[:, :ts] = jnp.where(row < d, qt, zero)
    qs_ref[:, ts:] = jnp.where(row >= d, qt, zero)
    m_ref[...] = jnp.full(m_ref.shape, NEG_INF, F32)
    l_ref[...] = jnp.zeros(l_ref.shape, F32)
    acc_ref[...] = jnp.zeros(acc_ref.shape, F32)

    def scores(ki, bias):
        kblk = k_ref[pl.ds(pl.multiple_of(ki * ts, ts), ts), :]
        s = jnp.dot(kblk, qs_ref[...], preferred_element_type=F32)
        if bias is not None:
            s = s + jnp.concatenate([bias, bias], axis=1)
        return s

    def update(s_ref, ki):
        s = s_ref[...]
        m_prev = m_ref[...]
        m_new = jnp.maximum(m_prev, jnp.max(s, axis=0, keepdims=True))
        alpha = jnp.exp(m_prev - m_new)
        p = jnp.exp(s - m_new)
        l_ref[...] = alpha * l_ref[...] + jnp.sum(p, axis=0, keepdims=True)
        acc_ref[...] = alpha * acc_ref[...] + jnp.dot(vt_ref[ki], p.astype(BF16), preferred_element_type=F32)
        m_ref[...] = m_new

    n_far = jnp.maximum(qi - 1, 0)
    sa_ref[...] = scores(qi, bias_ref[0, 1])

    @pl.when(qi > 0)
    def _():
        sb_ref[...] = scores(qi - 1, bias_ref[0, 0])
        update(sa_ref, qi)

    def far_pair(jj, carry):
        ka = qi - 2 - 2 * jj
        sa_ref[...] = scores(ka, None)
        update(sb_ref, ka + 1)
        sb_ref[...] = scores(ka - 1, None)
        update(sa_ref, ka)
        return carry

    lax.fori_loop(0, n_far // 2, far_pair, 0)

    @pl.when(n_far % 2 == 1)
    def _():
        sa_ref[...] = scores(0, None)
        update(sb_ref, 1)

    @pl.when(qi % 2 == 0)
    def _():
        update(sa_ref, 0)

    @pl.when(qi % 2 == 1)
    def _():
        update(sb_ref, 0)

    lam = _lambda(lq1, lk1, lq2, lk2, lambda_init)
    inv_l = 1.0 / l_ref[...]
    acc = acc_ref[...]
    o = acc[:, :ts] * inv_l[:, :ts] - lam * (acc[:, ts:] * inv_l[:, ts:])
    o = o * lax.rsqrt(jnp.mean(o * o, axis=0, keepdims=True) + LN_EPS)
    o = o * g_ref[...] * (1.0 - lambda_init)
    o_ref[...] = o.T.astype(o_ref.dtype)


def _attn_prompt(qt, kb, vt, bias, subln_g, lams, *, batch, seq, heads, ts, lambda_init):
    t, d_attn = kb.shape
    hd = 2 * ATT_HEAD_DIM
    nq = seq // ts
    lam_spec = _full((1, ATT_HEAD_DIM))
    return pl.pallas_call(
        functools.partial(_attn_kernel, ts=ts, lambda_init=lambda_init),
        grid=(batch, heads, nq),
        in_specs=[pl.BlockSpec((1, hd, ts), lambda b, h, q: (b * nq + q, h, 0)),
                  pl.BlockSpec((seq, hd), lambda b, h, q: (b, h)),
                  pl.BlockSpec((nq, hd, ts), lambda b, h, q: (b, h, 0)),
                  pl.BlockSpec((1, 2, ts, ts), lambda b, h, q: (h, 0, 0, 0)),
                  _full((hd, 1)), lam_spec, lam_spec, lam_spec, lam_spec],
        out_specs=pl.BlockSpec((ts, hd), lambda b, h, q: (b * nq + q, h)),
        out_shape=jax.ShapeDtypeStruct((t, d_attn), BF16),
        scratch_shapes=[pltpu.VMEM((hd, 2 * ts), BF16), pltpu.VMEM((1, 2 * ts), F32),
                        pltpu.VMEM((1, 2 * ts), F32), pltpu.VMEM((hd, 2 * ts), F32),
                        pltpu.VMEM((ts, 2 * ts), F32), pltpu.VMEM((ts, 2 * ts), F32)],
        compiler_params=_params("arbitrary", "arbitrary", "arbitrary"),
        name="attn_prompt",
    )(qt, kb, vt, bias, subln_g.reshape(hd, 1), *lams)


def _decode_kernel(pt_ref, rb_ref, q_ref, kn_ref, vn_ref, g_ref, lq1, lk1, lq2, lk2, *rest,
                   n_pages, page, heads, lambda_init):
    k_refs = rest[:n_pages]
    v_refs = rest[n_pages:2 * n_pages]
    o_ref = rest[2 * n_pages]
    s_ref, p_ref, bias_ref = rest[2 * n_pages + 1:]
    d = ATT_HEAD_DIM
    hd = 2 * d
    rows = page * heads
    past = n_pages * rows
    half2 = LANES // 2
    log_heads = int(math.log2(heads))

    q4 = q_ref[0] * (d ** -0.5)
    r = lax.broadcasted_iota(jnp.int32, (LANES, hd), 0)
    c = lax.broadcasted_iota(jnp.int32, (LANES, hd), 1)
    qrows = jnp.zeros((LANES, hd), F32)
    for h in range(heads):
        qh = jnp.broadcast_to(q4[h:h + 1, :], (LANES, hd))
        qrows = jnp.where((r == h) & (c < d), qh, qrows)
        qrows = jnp.where((r == half2 + h) & (c >= d), qh, qrows)
    qcols = qrows.T.astype(BF16)

    def lane_head(shape):
        lane = lax.broadcasted_iota(jnp.int32, shape, 1)
        return jnp.where(lane < half2, lane, lane - half2)

    def row_head(shape):
        return lax.broadcasted_iota(jnp.int32, shape, 0) & (heads - 1)

    @pl.when(pl.program_id(0) == 0)
    def _():
        key = lax.shift_right_logical(lax.broadcasted_iota(jnp.int32, (rows, LANES), 0), log_heads)
        lh = lane_head((rows, LANES))
        lh8 = lane_head((8, LANES))
        bias = jnp.zeros((rows, LANES), F32)
        bias_new = jnp.zeros((8, LANES), F32)
        for h in range(heads):
            bias = jnp.where(lh == h, _bias_of(page - key, rb_ref, h), bias)
            bias_new = jnp.where(lh8 == h, _bias_of(jnp.zeros((8, LANES), jnp.int32), rb_ref, h), bias_new)
        bias_ref[0:rows, :] = bias
        bias_ref[rows:rows + 8, :] = bias_new

    own = row_head((rows, LANES)) == lane_head((rows, LANES))
    for j in range(n_pages):
        sj = jnp.dot(k_refs[j][0].astype(BF16), qcols, preferred_element_type=F32)
        if j == n_pages - 1:
            sj = sj + bias_ref[0:rows, :]
        s_ref[j * rows:(j + 1) * rows, :] = jnp.where(own, sj, NEG_INF)
    kn = jnp.concatenate([kn_ref[0], jnp.zeros((8 - heads, hd), F32)], axis=0).astype(BF16)
    s_new = jnp.dot(kn, qcols, preferred_element_type=F32)
    r8 = lax.broadcasted_iota(jnp.int32, (8, LANES), 0)
    s_ref[past:past + 8, :] = jnp.where(r8 == lane_head((8, LANES)), s_new + bias_ref[rows:rows + 8, :], NEG_INF)

    s = s_ref[...]
    m = jnp.max(s, axis=0, keepdims=True)
    p = jnp.exp(s - m)
    l = jnp.sum(p, axis=0, keepdims=True)
    lam = _lambda(lq1, lk1, lq2, lk2, lambda_init)
    lane1 = lax.broadcasted_iota(jnp.int32, (1, LANES), 1)
    live = (lane1 < heads) | ((lane1 >= half2) & (lane1 < half2 + heads))
    coef = jnp.where(live, jnp.where(lane1 < half2, 1.0 / l, -lam / l), 0.0)
    p_ref[...] = (p * coef).astype(BF16)

    ones = jnp.ones((LANES, hd), BF16)
    acc = jnp.zeros((rows, hd), F32)
    for j in range(n_pages):
        wrep = jnp.dot(p_ref[j * rows:(j + 1) * rows, :], ones, preferred_element_type=F32)
        acc = acc + wrep * v_refs[j][0]
    acc8 = acc[0:8, :]
    for i in range(1, rows // 8):
        acc8 = acc8 + acc[i * 8:(i + 1) * 8, :]
    w_new = jnp.dot(p_ref[past:past + 8, :], ones, preferred_element_type=F32)
    vn = jnp.concatenate([vn_ref[0], jnp.zeros((8 - heads, hd), F32)], axis=0)
    o = acc8[0:heads, :] + acc8[heads:2 * heads, :] + (w_new * vn)[0:heads, :]
    o = o * lax.rsqrt(jnp.mean(o * o, axis=-1, keepdims=True) + LN_EPS)
    o_ref[0] = (o * g_ref[...] * (1.0 - lambda_init)).astype(o_ref.dtype)


def _attn_sample(page_table, rel_bias, q, k_new, v_new, cache_k, cache_v, subln_g, lams, *, heads, lambda_init):
    nb, n_pages = page_table.shape
    n_pool, rows, hd = cache_k.shape
    page = rows // heads
    assert heads & (heads - 1) == 0 and 2 * heads <= 8
    tok = lambda a: a.reshape(nb, heads, hd)
    tok_spec = pl.BlockSpec((1, heads, hd), lambda b, pt: (b, 0, 0))
    lam_spec = pl.BlockSpec((1, ATT_HEAD_DIM), lambda b, pt: (0, 0))
    page_specs = [pl.BlockSpec((1, rows, hd), lambda b, pt, j=j: (pt[b, j], 0, 0)) for j in range(n_pages)]
    grid_spec = pltpu.PrefetchScalarGridSpec(
        num_scalar_prefetch=1,
        grid=(nb,),
        in_specs=[pl.BlockSpec(memory_space=pltpu.SMEM), tok_spec, tok_spec, tok_spec,
                  pl.BlockSpec((1, hd), lambda b, pt: (0, 0)),
                  lam_spec, lam_spec, lam_spec, lam_spec] + page_specs + page_specs,
        out_specs=tok_spec,
        scratch_shapes=[pltpu.VMEM((n_pages * rows + 8, LANES), F32),
                        pltpu.VMEM((n_pages * rows + 8, LANES), BF16),
                        pltpu.VMEM((rows + 8, LANES), F32)],
    )
    out = pl.pallas_call(
        functools.partial(_decode_kernel, n_pages=n_pages, page=page, heads=heads, lambda_init=lambda_init),
        grid_spec=grid_spec,
        out_shape=jax.ShapeDtypeStruct((nb, heads, hd), BF16),
        compiler_params=_params("arbitrary"),
        name="attn_sample",
    )(page_table, rel_bias, tok(q), tok(k_new), tok(v_new), subln_g.reshape(1, hd), *lams,
      *([cache_k] * n_pages), *([cache_v] * n_pages))
    return out.reshape(nb, heads * hd)


def _mix_kernel(x_ref, cy_ref, ay_ref, mod_ref, wo_ref, bo_ref, g1_ref, b1_ref, wq_ref, sk_ref,
                x1_ref, h2t_ref, st_ref, *, per_token_mod, alpha, d_conv):
    if per_token_mod:
        gate1, sh2, sc2 = mod_ref[2], mod_ref[3], mod_ref[4]
    else:
        gate1, sh2, sc2 = mod_ref[2, 0], mod_ref[3, 0], mod_ref[4, 0]
    mix = (jnp.dot(cy_ref[...], wo_ref[:d_conv, :], preferred_element_type=F32)
           + jnp.dot(ay_ref[...], wo_ref[d_conv:, :], preferred_element_type=F32) + bo_ref[...])
    x1 = _ln(alpha * x_ref[...] + gate1 * mix) * g1_ref[...] + b1_ref[...]
    x1_ref[...] = x1
    h2 = _ln(x1) * (1.0 + sc2) + sh2
    h2b = h2.astype(BF16)
    h2t_ref[...] = h2.T.astype(BF16)
    qh = jnp.dot(h2b, wq_ref[...], preferred_element_type=F32).astype(BF16)
    nk = sk_ref.shape[2]
    for hc in range(sk_ref.shape[0]):
        st_ref[hc] = lax.dot_general(sk_ref[hc], qh[:, hc * nk:(hc + 1) * nk], (((1,), (1,)), ((), ())),
                                     preferred_element_type=F32)


def _mix(x, cy, ay, mod, w_out, b_out, ln_g, ln_b, w_query, sub_keys, *, tm, rows_per_mod, alpha):
    t, d = x.shape
    d_conv = cy.shape[1]
    d_attn = ay.shape[1]
    n_hc, n_keys, half = sub_keys.shape
    per_token = rows_per_mod is None
    if per_token:
        mod_spec = pl.BlockSpec((6, tm, d), lambda i: (0, i, 0))
    else:
        mod_spec = pl.BlockSpec((6, 1, 1, d), lambda i: (0, (i * tm) // rows_per_mod, 0, 0))
    vec = lambda a: a.reshape(1, d)
    return pl.pallas_call(
        functools.partial(_mix_kernel, per_token_mod=per_token, alpha=alpha, d_conv=d_conv),
        grid=(t // tm,),
        in_specs=[pl.BlockSpec((tm, d), lambda i: (i, 0)),
                  pl.BlockSpec((tm, d_conv), lambda i: (i, 0)),
                  pl.BlockSpec((tm, d_attn), lambda i: (i, 0)),
                  mod_spec, _full(w_out.shape), _full((1, d)), _full((1, d)), _full((1, d)),
                  _full(w_query.shape), _full(sub_keys.shape)],
        out_specs=[pl.BlockSpec((tm, d), lambda i: (i, 0)),
                   pl.BlockSpec((d, tm), lambda i: (0, i)),
                   pl.BlockSpec((n_hc, n_keys, tm), lambda i: (0, 0, i))],
        out_shape=[jax.ShapeDtypeStruct((t, d), F32),
                   jax.ShapeDtypeStruct((d, t), BF16),
                   jax.ShapeDtypeStruct((n_hc, n_keys, t), F32)],
        compiler_params=_params("arbitrary"),
        name="mix_sample" if per_token else "mix_prompt",
    )(x, cy, ay, mod, w_out, vec(b_out), vec(ln_g), vec(ln_b), w_query, sub_keys)


def _top_ranks(s):
    work = s
    rank = jnp.full(s.shape, float(PEER_TOPK), F32)
    vals = []
    for r in range(PEER_TOPK):
        m = jnp.max(work, axis=0, keepdims=True)
        hit = work == m
        rank = jnp.where(hit, float(r), rank)
        work = jnp.where(hit, -jnp.inf, work)
        vals.append(m)
    return rank, vals


def _route_kernel(s_ref, r2_ref, e2_ref, n1_ref, e1_ref):
    k = PEER_TOPK
    s1 = s_ref[0]
    s2 = s_ref[1]
    rank1, v1 = _top_ranks(s1)
    rank2, v2 = _top_ranks(s2)
    rowk = lax.broadcasted_iota(jnp.int32, (k,) + s1.shape[1:], 0)

    def stack(vals):
        out = jnp.zeros(rowk.shape, F32)
        for r in range(k):
            out = jnp.where(rowk == r, vals[r], out)
        return out

    v1m = stack(v1)
    v2m = stack(v2)
    row8 = lax.broadcasted_iota(jnp.int32, (8,) + s1.shape[1:], 0)
    cands = [v1[0] + v2m]
    for a in range(1, 8):
        cands.append(jnp.where(row8 < k // (a + 1), v1[a] + v2m[0:8], -jnp.inf))
    cands.append(v1m[8:k] + v2[0])
    work = jnp.concatenate(cands, axis=0)
    thr = None
    for _ in range(k):
        thr = jnp.max(work, axis=0, keepdims=True)
        work = jnp.where(work == thr, -jnp.inf, work)
    e2top = jnp.exp(v2m - v2[0])
    z = jnp.zeros_like(thr)
    n1 = jnp.zeros(s1.shape, F32)
    for a in range(k):
        sel = (v1[a] + v2m) >= thr
        cnt = jnp.sum(jnp.where(sel, 1.0, 0.0), axis=0, keepdims=True)
        z = z + jnp.exp(v1[a] - v1[0]) * jnp.sum(jnp.where(sel, e2top, 0.0), axis=0, keepdims=True)
        n1 = jnp.where(rank1 == float(a), cnt, n1)
    e1 = jnp.where(rank1 < float(k), jnp.exp(s1 - v1[0]) / z, 0.0)
    e2 = jnp.where(rank2 < float(k), jnp.exp(s2 - v2[0]), 0.0)
    r2_ref[0] = rank2.astype(r2_ref.dtype)
    e2_ref[0] = e2.astype(e2_ref.dtype)
    n1_ref[0] = _bf16_pair(n1)
    e1_ref[0] = _bf16_pair(e1)


def _bf16_pair(x):
    bits = pltpu.bitcast(x.astype(BF16).astype(F32), jnp.uint32)
    return bits | lax.shift_right_logical(bits, jnp.uint32(16))


def _route(st, *, tl):
    n_hc, n_keys, t = st.shape
    heads = n_hc // 2
    out = lambda dt: jax.ShapeDtypeStruct((heads, n_keys, t), dt)
    spec = pl.BlockSpec((1, n_keys, tl), lambda i, h: (h, 0, i))
    return pl.pallas_call(
        _route_kernel,
        grid=(t // tl, heads),
        in_specs=[pl.BlockSpec((2, n_keys, tl), lambda i, h: (h, 0, i))],
        out_specs=[spec] * 4,
        out_shape=[out(BF16), out(BF16), out(jnp.uint32), out(jnp.uint32)],
        compiler_params=_params("arbitrary", "arbitrary"),
        name="route",
    )(st)


GELU_C1 = math.sqrt(2.0 / math.pi)
GELU_C2 = GELU_C1 * 0.044715


def _gelu_tanh(x):
    hx = 0.5 * x
    return hx + hx * jnp.tanh(x * (GELU_C1 + GELU_C2 * (x * x)))


def _packed_rows(words):
    return pltpu.bitcast(jnp.broadcast_to(words, (8, words.shape[1])), BF16)


def _peer_kernel(h2t_ref, u_ref, vt_ref, r2_ref, e2_ref, n1_ref, e1_ref, x1_ref, mod_ref, g_ref, b_ref, y_ref,
                 at_a, at_b, zt_a, zt_b, acc_ref, *, per_token_mod, alpha, n_keys):
    j = pl.program_id(1)
    slab, tq = at_a.shape
    ek = u_ref.shape[0]
    heads = r2_ref.shape[0]
    sub = 16
    col = min(256, tq)
    groups = slab // n_keys

    @pl.when(j == 0)
    def _():
        acc_ref[...] = jnp.zeros(acc_ref.shape, F32)

    def hidden(s, at_ref):
        rows = pl.ds(pl.multiple_of(s * slab, slab), slab)
        at_ref[...] = jnp.dot(u_ref[rows, :], h2t_ref[...], preferred_element_type=F32)

    def mask(s, at_ref, zt_ref):
        for g in range(groups):
            i1 = j * (ek // n_keys) + s * groups + g
            n1w = [n1_ref[h, pl.ds(i1, 1), :] for h in range(heads)]
            e1w = [e1_ref[h, pl.ds(i1, 1), :] for h in range(heads)]
            for c in range(tq // col):
                tok = slice(c * col, (c + 1) * col)
                n1b = [_packed_rows(n1w[h][:, tok]) for h in range(heads)]
                e1b = [_packed_rows(e1w[h][:, tok]) for h in range(heads)]
                for k in range(n_keys // sub):
                    r0 = g * n_keys + k * sub
                    keys = slice(k * sub, (k + 1) * sub)
                    act = _gelu_tanh(at_ref[r0:r0 + sub, tok].astype(BF16))
                    w = jnp.zeros((sub, col), BF16)
                    for h in range(heads):
                        w = jnp.where(r2_ref[h, keys, tok] < n1b[h], w + e2_ref[h, keys, tok] * e1b[h], w)
                    zt_ref[r0:r0 + sub, tok] = w * act

    def project(s, zt_ref):
        acc_ref[...] += jnp.dot(vt_ref[s], zt_ref[...], preferred_element_type=F32)

    def pair(p, carry):
        a = 2 * p
        hidden(a, at_a)
        hidden(a + 1, at_b)
        mask(a, at_a, zt_a)
        project(a, zt_a)
        mask(a + 1, at_b, zt_b)
        project(a + 1, zt_b)
        return carry

    lax.fori_loop(0, ek // (2 * slab), pair, 0)

    @pl.when(j == pl.num_programs(1) - 1)
    def _():
        gate2 = mod_ref[5] if per_token_mod else mod_ref[5, 0]
        ff = acc_ref[...].T
        y_ref[...] = _ln(alpha * x1_ref[...] + gate2 * ff) * g_ref[...] + b_ref[...]


def _peer(h2t, u_b, vt_b, r2, e2, n1, e1, x1, mod, ln_g, ln_b, *, tq, ek, rows_per_mod, alpha):
    d, t = h2t.shape
    ne = u_b.shape[0]
    heads, n_keys, _ = r2.shape
    slab = vt_b.shape[2]
    per_token = rows_per_mod is None
    if per_token:
        mod_spec = pl.BlockSpec((6, tq, d), lambda i, j: (0, i, 0))
    else:
        mod_spec = pl.BlockSpec((6, 1, 1, d), lambda i, j: (0, (i * tq) // rows_per_mod, 0, 0))
    tab = pl.BlockSpec((heads, n_keys, tq), lambda i, j: (0, 0, i))
    vec = lambda a: a.reshape(1, d)
    return pl.pallas_call(
        functools.partial(_peer_kernel, per_token_mod=per_token, alpha=alpha, n_keys=n_keys),
        grid=(t // tq, ne // ek),
        in_specs=[pl.BlockSpec((d, tq), lambda i, j: (0, i)),
                  pl.BlockSpec((ek, d), lambda i, j: (j, 0)),
                  pl.BlockSpec((ek // slab, d, slab), lambda i, j: (j, 0, 0)),
                  tab, tab, tab, tab,
                  pl.BlockSpec((tq, d), lambda i, j: (i, 0)),
                  mod_spec, _full((1, d)), _full((1, d))],
        out_specs=pl.BlockSpec((tq, d), lambda i, j: (i, 0)),
        out_shape=jax.ShapeDtypeStruct((t, d), F32),
        scratch_shapes=[pltpu.VMEM((slab, tq), F32), pltpu.VMEM((slab, tq), F32),
                        pltpu.VMEM((slab, tq), BF16), pltpu.VMEM((slab, tq), BF16), pltpu.VMEM((d, tq), F32)],
        compiler_params=_params("arbitrary", "arbitrary"),
        name="peer_sample" if per_token else "peer_prompt",
    )(h2t, u_b, vt_b, r2, e2, n1, e1, x1, mod, vec(ln_g), vec(ln_b))


def _transpose_kernel(x_ref, o_ref):
    slab = o_ref.shape[2]
    for c in range(o_ref.shape[0]):
        o_ref[c] = x_ref[c * slab:(c + 1) * slab, :].T.astype(o_ref.dtype)


def _transpose_slabs(x, *, rows, slab):
    n, d = x.shape
    return pl.pallas_call(
        _transpose_kernel,
        grid=(n // rows,),
        in_specs=[pl.BlockSpec((rows, d), lambda i: (i, 0))],
        out_specs=pl.BlockSpec((rows // slab, d, slab), lambda i: (i, 0, 0)),
        out_shape=jax.ShapeDtypeStruct((n // slab, d, slab), BF16),
        compiler_params=_params("arbitrary"),
        name="transpose_v",
    )(x)


def kernel(x_prompt, x_sample, cache_k, cache_v, state_conv, page_table, c_prompt, c_sample, w_ada, b_ada, w_in, b_in, conv_w, conv_b, conv_ln_g, conv_ln_b, lambda_q1, lambda_k1, lambda_q2, lambda_k2, attn_subln_g, rel_bias, w_out, b_out, ln1_g, ln1_b, peer_w_query, peer_sub_keys, peer_u, peer_v, ln2_g, ln2_b):
    batch, seq, d = x_prompt.shape
    nb, dec_seq, _ = x_sample.shape
    depth = w_ada.shape[0]
    assert depth == 1 and dec_seq == 1
    heads = cache_k.shape[3]
    hd = cache_k.shape[4]
    assert hd == 2 * ATT_HEAD_DIM
    d_attn = heads * hd
    d_conv = conv_w.shape[2]
    hist = conv_w.shape[1] - 1
    assert hist <= CONV_PAD - 1
    page = cache_k.shape[2]
    n_keys = peer_sub_keys.shape[3]
    ne = peer_u.shape[1]
    alpha = (2 * depth) ** 0.25
    lambda_init = 0.8 - 0.6 * math.exp(-0.3 * 0)
    t_p = batch * seq
    ts = min(TOKEN_TILE, seq)
    tq_peer = min(PEER_TOKEN_TILE, seq)
    ek = min(PEER_EXPERT_TILE, ne)
    assert seq % ts == 0 and nb % LANES == 0 and page >= MAX_DISTANCE

    l = 0
    n_c = batch + nb
    n_c_pad = -(-n_c // 8) * 8
    c_all = jnp.concatenate([c_prompt, c_sample, jnp.zeros((n_c_pad - n_c, d), F32)], axis=0)
    mod = _ada(c_all, w_ada[l], b_ada[l])
    mod_p = mod[:, :batch].reshape(6, batch, 1, d)
    mod_s = mod[:, batch:n_c]

    w_in_b = w_in[l].astype(BF16)
    b_in_r = b_in[l].reshape(1, -1)
    w_out_b = w_out[l].astype(BF16)
    wq_b = peer_w_query[l].astype(BF16)
    sk_b = peer_sub_keys[l].reshape(-1, n_keys, peer_sub_keys.shape[-1]).astype(BF16)
    u_b = peer_u[l].astype(BF16)
    vt_b = _transpose_slabs(peer_v[l], rows=min(512, ne), slab=min(PEER_EXPERT_SLAB, ne))
    lams = [a[l].reshape(1, -1) for a in (lambda_q1, lambda_k1, lambda_q2, lambda_k2)]
    subln_g = attn_subln_g[l]

    xp = x_prompt.reshape(t_p, d)
    glu_p, k_p, v_p, kb_p, qt_p, vt_p = _inproj(xp, mod_p, w_in_b, b_in_r, tm=ts, rows_per_mod=seq,
                                                d_conv=d_conv, d_attn=d_attn)
    cy_p = _conv_prompt(glu_p, conv_w[l], conv_b[l], conv_ln_g[l], conv_ln_b[l], batch=batch, seq=seq, ts=ts)
    bias = _bias_tiles(rel_bias, heads=heads, ts=ts)
    ay_p = _attn_prompt(qt_p, kb_p, vt_p, bias, subln_g, lams, batch=batch, seq=seq, heads=heads, ts=ts,
                        lambda_init=lambda_init)
    x1_p, h2t_p, st_p = _mix(xp, cy_p, ay_p, mod_p, w_out_b, b_out[l], ln1_g[l], ln1_b[l], wq_b, sk_b,
                             tm=ts, rows_per_mod=seq, alpha=alpha)
    r2, e2, n1, e1 = _route(st_p, tl=min(ROUTE_TOKEN_TILE, t_p))
    y_p = _peer(h2t_p, u_b, vt_b, r2, e2, n1, e1, x1_p, mod_p, ln2_g[l], ln2_b[l],
                tq=tq_peer, ek=ek, rows_per_mod=seq, alpha=alpha)

    xs = x_sample.reshape(nb, d)
    glu_s, k_s, v_s, q_s = _inproj(xs, mod_s, w_in_b, b_in_r, tm=nb, rows_per_mod=None,
                                   d_conv=d_conv, d_attn=d_attn)
    state = state_conv[l]
    cy_s = _conv_sample(jnp.swapaxes(state, 0, 1), glu_s, conv_w[l], conv_b[l], conv_ln_g[l], conv_ln_b[l])
    n_pool = cache_k.shape[1]
    pool_rows = lambda a: a.reshape(depth * n_pool, page * heads, hd)
    ay_s = _attn_sample(page_table + l * n_pool, rel_bias, q_s, k_s, v_s, pool_rows(cache_k), pool_rows(cache_v),
                        subln_g, lams, heads=heads, lambda_init=lambda_init)
    x1_s, h2t_s, st_s = _mix(xs, cy_s, ay_s, mod_s, w_out_b, b_out[l], ln1_g[l], ln1_b[l], wq_b, sk_b,
                             tm=nb, rows_per_mod=None, alpha=alpha)
    r2s, e2s, n1s, e1s = _route(st_s, tl=nb)
    y_s = _peer(h2t_s, u_b, vt_b, r2s, e2s, n1s, e1s, x1_s, mod_s, ln2_g[l], ln2_b[l],
                tq=nb, ek=ek, rows_per_mod=None, alpha=alpha)

    kv_p = (depth, batch, seq, heads, hd)
    kv_s = (depth, nb, dec_seq, heads, hd)
    conv_p = glu_p.reshape(batch, seq, d_conv)[:, seq - hist:][None]
    conv_s = jnp.concatenate([state[:, 1:], glu_s[:, None, :]], axis=1)[None]
    return (y_p.reshape(batch, seq, d), y_s.reshape(nb, dec_seq, d),
            k_p.reshape(kv_p), v_p.reshape(kv_p), conv_p,
            k_s.reshape(kv_s), v_s.reshape(kv_s), conv_s)
```

```python
import functools
import math

import jax
import jax.numpy as jnp
from jax import lax
from jax.experimental import pallas as pl
from jax.experimental.pallas import tpu as pltpu

F32 = jnp.float32
BF16 = jnp.bfloat16

LN_EPS = 1e-5
NEG_INF = -1e30
ATT_HEAD_DIM = 64
N_BUCKETS = 32
MAX_DISTANCE = 128
PEER_TOPK = 16
LANES = 128
VMEM_LIMIT_BYTES = 56 * 1024 * 1024
TOKEN_TILE = 512
PEER_TOKEN_TILE = 512
PEER_EXPERT_TILE = 2048
PEER_EXPERT_SLAB = 256
ROUTE_TOKEN_TILE = 256


def _params(*sem):
    return pltpu.CompilerParams(dimension_semantics=sem, vmem_limit_bytes=VMEM_LIMIT_BYTES)


def _ln(x):
    mu = jnp.mean(x, axis=-1, keepdims=True)
    xc = x - mu
    var = jnp.mean(xc * xc, axis=-1, keepdims=True)
    return xc * lax.rsqrt(var + LN_EPS)


def _full(shape):
    return pl.BlockSpec(shape, lambda *_: (0,) * len(shape))


def _ada_kernel(c_ref, w_ref, b_ref, o_ref):
    c = c_ref[...]
    s = c * jax.nn.sigmoid(c)
    o_ref[0] = jnp.dot(s.astype(BF16), w_ref[...].astype(BF16), preferred_element_type=F32) + b_ref[0]


def _ada(c_all, w_ada, b_ada):
    nc, d = c_all.shape
    return pl.pallas_call(
        _ada_kernel,
        grid=(6,),
        in_specs=[_full((nc, d)),
                  pl.BlockSpec((d, d), lambda k: (0, k)),
                  pl.BlockSpec((1, 1, d), lambda k: (k, 0, 0))],
        out_specs=pl.BlockSpec((1, nc, d), lambda k: (k, 0, 0)),
        out_shape=jax.ShapeDtypeStruct((6, nc, d), F32),
        compiler_params=_params("arbitrary"),
        name="ada",
    )(c_all, w_ada, b_ada.reshape(6, 1, d))


def _inproj_kernel(x_ref, mod_ref, w_ref, b_ref, *outs, per_token_mod, d_conv, d_attn, transposed):
    x = x_ref[...]
    if per_token_mod:
        sh1, sc1 = mod_ref[0], mod_ref[1]
    else:
        sh1, sc1 = mod_ref[0, 0], mod_ref[1, 0]
    h = _ln(x) * (1.0 + sc1) + sh1
    z = jnp.dot(h.astype(BF16), w_ref[...], preferred_element_type=F32) + b_ref[...]
    ga = z[:, :d_conv]
    gb = z[:, d_conv:2 * d_conv]
    o = 2 * d_conv
    q = z[:, o:o + d_attn]
    k = z[:, o + d_attn:o + 2 * d_attn]
    v = z[:, o + 2 * d_attn:o + 3 * d_attn]
    glu = ga * jax.nn.sigmoid(gb)
    if transposed:
        glu_ref, k_ref, v_ref, kb_ref, qt_ref, vt_ref = outs
        kb_ref[...] = k.astype(BF16)
        qt_ref[0] = (q * (ATT_HEAD_DIM ** -0.5)).T.astype(BF16)
        vt_ref[0] = v.T.astype(BF16)
    else:
        glu_ref, k_ref, v_ref, q_ref = outs
        q_ref[...] = q
    glu_ref[...] = glu
    heads = d_attn // (2 * ATT_HEAD_DIM)
    for h in range(heads):
        cols = slice(h * 2 * ATT_HEAD_DIM, (h + 1) * 2 * ATT_HEAD_DIM)
        k_ref[pl.ds(h, x.shape[0], stride=heads), :] = k[:, cols]
        v_ref[pl.ds(h, x.shape[0], stride=heads), :] = v[:, cols]


def _inproj(x, mod, w_in, b_in, *, tm, rows_per_mod, d_conv, d_attn):
    t, d = x.shape
    n_in = w_in.shape[1]
    nt = t // tm
    per_token = rows_per_mod is None
    if per_token:
        mod_spec = pl.BlockSpec((6, tm, d), lambda i: (0, i, 0))
    else:
        mod_spec = pl.BlockSpec((6, 1, 1, d), lambda i: (0, (i * tm) // rows_per_mod, 0, 0))
    row = lambda n, dt: (jax.ShapeDtypeStruct((t, n), dt), pl.BlockSpec((tm, n), lambda i: (i, 0)))
    hd = 2 * ATT_HEAD_DIM
    heads = d_attn // hd
    head_rows = (jax.ShapeDtypeStruct((t * heads, hd), F32), pl.BlockSpec((tm * heads, hd), lambda i: (i, 0)))
    outs = [row(d_conv, F32), head_rows, head_rows]
    if per_token:
        outs.append(row(d_attn, F32))
    else:
        outs.append(row(d_attn, BF16))
        tr = (jax.ShapeDtypeStruct((nt, d_attn, tm), BF16), pl.BlockSpec((1, d_attn, tm), lambda i: (i, 0, 0)))
        outs += [tr, tr]
    return pl.pallas_call(
        functools.partial(_inproj_kernel, per_token_mod=per_token, d_conv=d_conv, d_attn=d_attn,
                          transposed=not per_token),
        grid=(nt,),
        in_specs=[pl.BlockSpec((tm, d), lambda i: (i, 0)), mod_spec, _full((d, n_in)), _full((1, n_in))],
        out_specs=[o[1] for o in outs],
        out_shape=[o[0] for o in outs],
        compiler_params=_params("arbitrary"),
        name="inproj_sample" if per_token else "inproj_prompt",
    )(x, mod, w_in, b_in)


CONV_PAD = 32


def _conv_post(acc, cb_ref, g_ref, b_ref):
    y = _ln(acc + cb_ref[...]) * g_ref[...] + b_ref[...]
    return y * jax.nn.sigmoid(y)


def _conv_prompt_kernel(glu_ref, cw_ref, cb_ref, g_ref, b_ref, y_ref, full_ref, *, ts, width, chunk):
    s = pl.program_id(1)
    hist = width - 1

    @pl.when(s == 0)
    def _():
        full_ref[0:CONV_PAD, :] = jnp.zeros((CONV_PAD, full_ref.shape[1]), F32)

    @pl.when(s > 0)
    def _():
        full_ref[0:CONV_PAD, :] = full_ref[ts:ts + CONV_PAD, :]

    full_ref[CONV_PAD:CONV_PAD + ts, :] = glu_ref[...]
    base = CONV_PAD - hist
    for c in range(ts // chunk):
        r0 = c * chunk
        acc = full_ref[base + r0:base + r0 + chunk, :] * cw_ref[0:1, :]
        for w in range(1, width):
            acc = acc + full_ref[base + r0 + w:base + r0 + w + chunk, :] * cw_ref[w:w + 1, :]
        y_ref[r0:r0 + chunk, :] = _conv_post(acc, cb_ref, g_ref, b_ref).astype(y_ref.dtype)


def _conv_prompt(glu, conv_w, conv_b, ln_g, ln_b, *, batch, seq, ts):
    t, dc = glu.shape
    width = conv_w.shape[0]
    ns = seq // ts
    vec = lambda a: a.reshape(1, dc)
    return pl.pallas_call(
        functools.partial(_conv_prompt_kernel, ts=ts, width=width, chunk=min(64, ts)),
        grid=(batch, ns),
        in_specs=[pl.BlockSpec((ts, dc), lambda b, s: (b * ns + s, 0)),
                  _full((width, dc)), _full((1, dc)), _full((1, dc)), _full((1, dc))],
        out_specs=pl.BlockSpec((ts, dc), lambda b, s: (b * ns + s, 0)),
        out_shape=jax.ShapeDtypeStruct((t, dc), BF16),
        scratch_shapes=[pltpu.VMEM((CONV_PAD + ts, dc), F32)],
        compiler_params=_params("arbitrary", "arbitrary"),
        name="conv_prompt",
    )(glu, conv_w, vec(conv_b), vec(ln_g), vec(ln_b))


def _conv_sample_kernel(st_ref, glu_ref, cw_ref, cb_ref, g_ref, b_ref, y_ref, *, width):
    hist = width - 1
    acc = glu_ref[...] * cw_ref[hist:hist + 1, :]
    for w in range(hist):
        acc = acc + st_ref[w] * cw_ref[w:w + 1, :]
    y_ref[...] = _conv_post(acc, cb_ref, g_ref, b_ref).astype(y_ref.dtype)


def _conv_sample(state_t, glu, conv_w, conv_b, ln_g, ln_b):
    hist, nb, dc = state_t.shape
    width = conv_w.shape[0]
    vec = lambda a: a.reshape(1, dc)
    return pl.pallas_call(
        functools.partial(_conv_sample_kernel, width=width),
        grid=(1,),
        in_specs=[_full((hist, nb, dc)), _full((nb, dc)), _full((width, dc)),
                  _full((1, dc)), _full((1, dc)), _full((1, dc))],
        out_specs=_full((nb, dc)),
        out_shape=jax.ShapeDtypeStruct((nb, dc), BF16),
        compiler_params=_params("arbitrary"),
        name="conv_sample",
    )(state_t, glu, conv_w, vec(conv_b), vec(ln_g), vec(ln_b))


def _bucket(n):
    max_exact = N_BUCKETS // 2
    nf = jnp.maximum(n, 1).astype(F32)
    large = max_exact + (jnp.log(nf / max_exact) / math.log(MAX_DISTANCE / max_exact)
                         * (N_BUCKETS - max_exact)).astype(jnp.int32)
    large = jnp.minimum(large, N_BUCKETS - 1)
    return jnp.where(n < max_exact, n, large)


def _bias_of(n, rb_ref, h):
    bucket = _bucket(n)
    far = jnp.full(n.shape, rb_ref[N_BUCKETS - 1, h], F32)
    out = far
    for j in range(N_BUCKETS - 1):
        out = jnp.where(bucket == j, rb_ref[j, h], out)
    return out - far


def _bias_kernel(rb_ref, o_ref, *, ts):
    h = pl.program_id(0)
    d = pl.program_id(1)
    ik = lax.broadcasted_iota(jnp.int32, (ts, ts), 0)
    jq = lax.broadcasted_iota(jnp.int32, (ts, ts), 1)
    n = jq - ik + ts * (1 - d)
    b = _bias_of(jnp.maximum(n, 0), rb_ref, h)
    o_ref[0, 0] = jnp.where(n >= 0, b, NEG_INF)


def _bias_tiles(rel_bias, *, heads, ts):
    return pl.pallas_call(
        functools.partial(_bias_kernel, ts=ts),
        grid=(heads, 2),
        in_specs=[pl.BlockSpec(memory_space=pltpu.SMEM)],
        out_specs=pl.BlockSpec((1, 1, ts, ts), lambda h, d: (h, d, 0, 0)),
        out_shape=jax.ShapeDtypeStruct((heads, 2, ts, ts), F32),
        compiler_params=_params("arbitrary", "arbitrary"),
        name="bias_tiles",
    )(rel_bias)


def _lambda(lq1, lk1, lq2, lk2, lambda_init):
    s1 = jnp.sum(lq1[...] * lk1[...], axis=-1, keepdims=True)
    s2 = jnp.sum(lq2[...] * lk2[...], axis=-1, keepdims=True)
    return jnp.exp(s1) - jnp.exp(s2) + lambda_init


def _attn_kernel(qt_ref, k_ref, vt_ref, bias_ref, g_ref, lq1, lk1, lq2, lk2, o_ref,
                 qs_ref, m_ref, l_ref, acc_ref, sa_ref, sb_ref, *, ts, lambda_init):
    qi = pl.program_id(2)
    d = ATT_HEAD_DIM
    qt = qt_ref[0]
    row = lax.broadcasted_iota(jnp.int32, qt.shape, 0)
    zero = jnp.zeros_like(qt)
    qs_ref[:, :ts] = jnp.where(row < d, qt, zero)
    qs_ref[:, ts:] = jnp.where(row >= d, qt, zero)
    m_ref[...] = jnp.full(m_ref.shape, NEG_INF, F32)
    l_ref[...] = jnp.zeros(l_ref.shape, F32)
    acc_ref[...] = jnp.zeros(acc_ref.shape, F32)

    def scores(ki, bias):
        kblk = k_ref[pl.ds(pl.multiple_of(ki * ts, ts), ts), :]
        s = jnp.dot(kblk, qs_ref[...], preferred_element_type=F32)
        if bias is not None:
            s = s + jnp.concatenate([bias, bias], axis=1)
        return s

    def update(s_ref, ki):
        s = s_ref[...]
        m_prev = m_ref[...]
        m_new = jnp.maximum(m_prev, jnp.max(s, axis=0, keepdims=True))
        alpha = jnp.exp(m_prev - m_new)
        p = jnp.exp(s - m_new)
        l_ref[...] = alpha * l_ref[...] + jnp.sum(p, axis=0, keepdims=True)
        acc_ref[...] = alpha * acc_ref[...] + jnp.dot(vt_ref[ki], p.astype(BF16), preferred_element_type=F32)
        m_ref[...] = m_new

    n_far = jnp.maximum(qi - 1, 0)
    sa_ref[...] = scores(qi, bias_ref[0, 1])

    @pl.when(qi > 0)
    def _():
        sb_ref[...] = scores(qi - 1, bias_ref[0, 0])
        update(sa_ref, qi)

    def far_pair(jj, carry):
        ka = qi - 2 - 2 * jj
        sa_ref[...] = scores(ka, None)
        update(sb_ref, ka + 1)
        sb_ref[...] = scores(ka - 1, None)
        update(sa_ref, ka)
        return carry

    lax.fori_loop(0, n_far // 2, far_pair, 0)

    @pl.when(n_far % 2 == 1)
    def _():
        sa_ref[...] = scores(0, None)
        update(sb_ref, 1)

    @pl.when(qi % 2 == 0)
    def _():
        update(sa_ref, 0)

    @pl.when(qi % 2 == 1)
    def _():
        update(sb_ref, 0)

    lam = _lambda(lq1, lk1, lq2, lk2, lambda_init)
    inv_l = 1.0 / l_ref[...]
    acc = acc_ref[...]
    o = acc[:, :ts] * inv_l[:, :ts] - lam * (acc[:, ts:] * inv_l[:, ts:])
    o = o * lax.rsqrt(jnp.mean(o * o, axis=0, keepdims=True) + LN_EPS)
    o = o * g_ref[...] * (1.0 - lambda_init)
    o_ref[...] = o.T.astype(o_ref.dtype)


def _attn_prompt(qt, kb, vt, bias, subln_g, lams, *, batch, seq, heads, ts, lambda_init):
    t, d_attn = kb.shape
    hd = 2 * ATT_HEAD_DIM
    nq = seq // ts
    lam_spec = _full((1, ATT_HEAD_DIM))
    return pl.pallas_call(
        functools.partial(_attn_kernel, ts=ts, lambda_init=lambda_init),
        grid=(batch, heads, nq),
        in_specs=[pl.BlockSpec((1, hd, ts), lambda b, h, q: (b * nq + q, h, 0)),
                  pl.BlockSpec((seq, hd), lambda b, h, q: (b, h)),
                  pl.BlockSpec((nq, hd, ts), lambda b, h, q: (b, h, 0)),
                  pl.BlockSpec((1, 2, ts, ts), lambda b, h, q: (h, 0, 0, 0)),
                  _full((hd, 1)), lam_spec, lam_spec, lam_spec, lam_spec],
        out_specs=pl.BlockSpec((ts, hd), lambda b, h, q: (b * nq + q, h)),
        out_shape=jax.ShapeDtypeStruct((t, d_attn), BF16),
        scratch_shapes=[pltpu.VMEM((hd, 2 * ts), BF16), pltpu.VMEM((1, 2 * ts), F32),
                        pltpu.VMEM((1, 2 * ts), F32), pltpu.VMEM((hd, 2 * ts), F32),
                        pltpu.VMEM((ts, 2 * ts), F32), pltpu.VMEM((ts, 2 * ts), F32)],
        compiler_params=_params("arbitrary", "arbitrary", "arbitrary"),
        name="attn_prompt",
    )(qt, kb, vt, bias, subln_g.reshape(hd, 1), *lams)


def _decode_kernel(pt_ref, rb_ref, q_ref, kn_ref, vn_ref, g_ref, lq1, lk1, lq2, lk2, *rest,
                   n_pages, page, heads, lambda_init):
    k_refs = rest[:n_pages]
    v_refs = rest[n_pages:2 * n_pages]
    o_ref = rest[2 * n_pages]
    s_ref, bias_ref = rest[2 * n_pages + 1:]
    d = ATT_HEAD_DIM
    hd = 2 * d
    rows = page * heads
    past = n_pages * rows
    nr = 2 * heads
    log_heads = int(math.log2(heads))
    nt = (((1,), (1,)), ((), ()))

    q4 = q_ref[0] * (d ** -0.5)
    lane = lax.broadcasted_iota(jnp.int32, (nr, hd), 1)
    top = lax.broadcasted_iota(jnp.int32, (nr, hd), 0) < heads
    q8 = jnp.concatenate([q4, q4], axis=0)
    q8 = jnp.where(top == (lane < d), q8, 0.0).astype(BF16)

    def own(n_cols):
        r = lax.broadcasted_iota(jnp.int32, (nr, n_cols), 0)
        c = lax.broadcasted_iota(jnp.int32, (nr, n_cols), 1)
        return (r & (heads - 1)) == (c & (heads - 1))

    @pl.when(pl.program_id(0) == 0)
    def _():
        key = lax.shift_right_logical(lax.broadcasted_iota(jnp.int32, (nr, rows), 1), log_heads)
        rh = lax.broadcasted_iota(jnp.int32, (nr, rows), 0) & (heads - 1)
        rh_tail = lax.broadcasted_iota(jnp.int32, (nr, LANES), 0) & (heads - 1)
        bias = jnp.zeros((nr, rows), F32)
        bias_new = jnp.zeros((nr, LANES), F32)
        for h in range(heads):
            bias = jnp.where(rh == h, _bias_of(page - key, rb_ref, h), bias)
            bias_new = jnp.where(rh_tail == h, _bias_of(jnp.zeros((nr, LANES), jnp.int32), rb_ref, h), bias_new)
        bias_ref[:, 0:rows] = bias
        bias_ref[:, rows:rows + LANES] = bias_new

    own_page = own(rows)
    for j in range(n_pages):
        sj = lax.dot_general(q8, k_refs[j][0].astype(BF16), nt, preferred_element_type=F32)
        if j == n_pages - 1:
            sj = sj + bias_ref[:, 0:rows]
        s_ref[:, j * rows:(j + 1) * rows] = jnp.where(own_page, sj, NEG_INF)
    kn = jnp.concatenate([kn_ref[0], jnp.zeros((LANES - heads, hd), F32)], axis=0).astype(BF16)
    s_new = lax.dot_general(q8, kn, nt, preferred_element_type=F32)
    tail_col = lax.broadcasted_iota(jnp.int32, (nr, LANES), 1)
    s_ref[:, past:past + LANES] = jnp.where(own(LANES) & (tail_col < heads),
                                            s_new + bias_ref[:, rows:rows + LANES], NEG_INF)

    s = s_ref[...]
    m = jnp.max(s, axis=-1, keepdims=True)
    p = jnp.exp(s - m)
    l = jnp.sum(p, axis=-1, keepdims=True)
    pb = p.astype(BF16)
    out = jnp.zeros((nr, hd), F32)
    for j in range(n_pages):
        out = out + jnp.dot(pb[:, j * rows:(j + 1) * rows], v_refs[j][0].astype(BF16), preferred_element_type=F32)
    vn = jnp.concatenate([vn_ref[0], jnp.zeros((LANES - heads, hd), F32)], axis=0).astype(BF16)
    out = (out + jnp.dot(pb[:, past:past + LANES], vn, preferred_element_type=F32)) / l
    lam = _lambda(lq1, lk1, lq2, lk2, lambda_init)
    o = out[0:heads, :] - lam * out[heads:nr, :]
    o = o * lax.rsqrt(jnp.mean(o * o, axis=-1, keepdims=True) + LN_EPS)
    o_ref[0] = (o * g_ref[...] * (1.0 - lambda_init)).astype(o_ref.dtype)


def _attn_sample(page_table, rel_bias, q, k_new, v_new, cache_k, cache_v, subln_g, lams, *, heads, lambda_init):
    nb, n_pages = page_table.shape
    n_pool, rows, hd = cache_k.shape
    page = rows // heads
    assert heads & (heads - 1) == 0 and 2 * heads <= 8
    tok = lambda a: a.reshape(nb, heads, hd)
    tok_spec = pl.BlockSpec((1, heads, hd), lambda b, pt: (b, 0, 0))
    lam_spec = pl.BlockSpec((1, ATT_HEAD_DIM), lambda b, pt: (0, 0))
    page_specs = [pl.BlockSpec((1, rows, hd), lambda b, pt, j=j: (pt[b, j], 0, 0)) for j in range(n_pages)]
    grid_spec = pltpu.PrefetchScalarGridSpec(
        num_scalar_prefetch=1,
        grid=(nb,),
        in_specs=[pl.BlockSpec(memory_space=pltpu.SMEM), tok_spec, tok_spec, tok_spec,
                  pl.BlockSpec((1, hd), lambda b, pt: (0, 0)),
                  lam_spec, lam_spec, lam_spec, lam_spec] + page_specs + page_specs,
        out_specs=tok_spec,
        scratch_shapes=[pltpu.VMEM((2 * heads, n_pages * rows + LANES), F32),
                        pltpu.VMEM((2 * heads, rows + LANES), F32)],
    )
    out = pl.pallas_call(
        functools.partial(_decode_kernel, n_pages=n_pages, page=page, heads=heads, lambda_init=lambda_init),
        grid_spec=grid_spec,
        out_shape=jax.ShapeDtypeStruct((nb, heads, hd), BF16),
        compiler_params=_params("arbitrary"),
        name="attn_sample",
    )(page_table, rel_bias, tok(q), tok(k_new), tok(v_new), subln_g.reshape(1, hd), *lams,
      *([cache_k] * n_pages), *([cache_v] * n_pages))
    return out.reshape(nb, heads * hd)


def _mix_kernel(x_ref, cy_ref, ay_ref, mod_ref, wo_ref, bo_ref, g1_ref, b1_ref, wq_ref, sk_ref,
                x1_ref, h2t_ref, st_ref, *, per_token_mod, alpha, d_conv):
    if per_token_mod:
        gate1, sh2, sc2 = mod_ref[2], mod_ref[3], mod_ref[4]
    else:
        gate1, sh2, sc2 = mod_ref[2, 0], mod_ref[3, 0], mod_ref[4, 0]
    mix = (jnp.dot(cy_ref[...], wo_ref[:d_conv, :], preferred_element_type=F32)
           + jnp.dot(ay_ref[...], wo_ref[d_conv:, :], preferred_element_type=F32) + bo_ref[...])
    x1 = _ln(alpha * x_ref[...] + gate1 * mix) * g1_ref[...] + b1_ref[...]
    x1_ref[...] = x1
    h2 = _ln(x1) * (1.0 + sc2) + sh2
    h2b = h2.astype(BF16)
    h2t_ref[...] = h2.T.astype(BF16)
    qh = jnp.dot(h2b, wq_ref[...], preferred_element_type=F32).astype(BF16)
    nk = sk_ref.shape[2]
    for hc in range(sk_ref.shape[0]):
        st_ref[hc] = lax.dot_general(sk_ref[hc], qh[:, hc * nk:(hc + 1) * nk], (((1,), (1,)), ((), ())),
                                     preferred_element_type=F32)


def _mix(x, cy, ay, mod, w_out, b_out, ln_g, ln_b, w_query, sub_keys, *, tm, rows_per_mod, alpha):
    t, d = x.shape
    d_conv = cy.shape[1]
    d_attn = ay.shape[1]
    n_hc, n_keys, half = sub_keys.shape
    per_token = rows_per_mod is None
    if per_token:
        mod_spec = pl.BlockSpec((6, tm, d), lambda i: (0, i, 0))
    else:
        mod_spec = pl.BlockSpec((6, 1, 1, d), lambda i: (0, (i * tm) // rows_per_mod, 0, 0))
    vec = lambda a: a.reshape(1, d)
    return pl.pallas_call(
        functools.partial(_mix_kernel, per_token_mod=per_token, alpha=alpha, d_conv=d_conv),
        grid=(t // tm,),
        in_specs=[pl.BlockSpec((tm, d), lambda i: (i, 0)),
                  pl.BlockSpec((tm, d_conv), lambda i: (i, 0)),
                  pl.BlockSpec((tm, d_attn), lambda i: (i, 0)),
                  mod_spec, _full(w_out.shape), _full((1, d)), _full((1, d)), _full((1, d)),
                  _full(w_query.shape), _full(sub_keys.shape)],
        out_specs=[pl.BlockSpec((tm, d), lambda i: (i, 0)),
                   pl.BlockSpec((d, tm), lambda i: (0, i)),
                   pl.BlockSpec((n_hc, n_keys, tm), lambda i: (0, 0, i))],
        out_shape=[jax.ShapeDtypeStruct((t, d), F32),
                   jax.ShapeDtypeStruct((d, t), BF16),
                   jax.ShapeDtypeStruct((n_hc, n_keys, t), F32)],
        compiler_params=_params("arbitrary"),
        name="mix_sample" if per_token else "mix_prompt",
    )(x, cy, ay, mod, w_out, vec(b_out), vec(ln_g), vec(ln_b), w_query, sub_keys)


def _top_ranks(s):
    work = s
    rank = jnp.full(s.shape, float(PEER_TOPK), F32)
    vals = []
    for r in range(PEER_TOPK):
        m = jnp.max(work, axis=0, keepdims=True)
        hit = work == m
        rank = jnp.where(hit, float(r), rank)
        work = jnp.where(hit, -jnp.inf, work)
        vals.append(m)
    return rank, vals


def _route_kernel(s_ref, r2_ref, e2_ref, n1_ref, e1_ref):
    k = PEER_TOPK
    s1 = s_ref[0]
    s2 = s_ref[1]
    rank1, v1 = _top_ranks(s1)
    rank2, v2 = _top_ranks(s2)
    rowk = lax.broadcasted_iota(jnp.int32, (k,) + s1.shape[1:], 0)

    def stack(vals):
        out = jnp.zeros(rowk.shape, F32)
        for r in range(k):
            out = jnp.where(rowk == r, vals[r], out)
        return out

    v1m = stack(v1)
    v2m = stack(v2)
    row8 = lax.broadcasted_iota(jnp.int32, (8,) + s1.shape[1:], 0)
    cands = [v1[0] + v2m]
    for a in range(1, 8):
        cands.append(jnp.where(row8 < k // (a + 1), v1[a] + v2m[0:8], -jnp.inf))
    cands.append(v1m[8:k] + v2[0])
    work = jnp.concatenate(cands, axis=0)
    thr = None
    for _ in range(k):
        thr = jnp.max(work, axis=0, keepdims=True)
        work = jnp.where(work == thr, -jnp.inf, work)
    e2top = jnp.exp(v2m - v2[0])
    z = jnp.zeros_like(thr)
    n1 = jnp.zeros(s1.shape, F32)
    for a in range(k):
        sel = (v1[a] + v2m) >= thr
        cnt = jnp.sum(jnp.where(sel, 1.0, 0.0), axis=0, keepdims=True)
        z = z + jnp.exp(v1[a] - v1[0]) * jnp.sum(jnp.where(sel, e2top, 0.0), axis=0, keepdims=True)
        n1 = jnp.where(rank1 == float(a), cnt, n1)
    e1 = jnp.where(rank1 < float(k), jnp.exp(s1 - v1[0]) / z, 0.0)
    e2 = jnp.where(rank2 < float(k), jnp.exp(s2 - v2[0]), 0.0)
    r2_ref[0] = rank2.astype(r2_ref.dtype)
    e2_ref[0] = e2.astype(e2_ref.dtype)
    n1_ref[0] = _bf16_pair(n1)
    e1_ref[0] = _bf16_pair(e1)


def _bf16_pair(x):
    bits = pltpu.bitcast(x.astype(BF16).astype(F32), jnp.uint32)
    return bits | lax.shift_right_logical(bits, jnp.uint32(16))


def _route(st, *, tl):
    n_hc, n_keys, t = st.shape
    heads = n_hc // 2
    out = lambda dt: jax.ShapeDtypeStruct((heads, n_keys, t), dt)
    spec = pl.BlockSpec((1, n_keys, tl), lambda i, h: (h, 0, i))
    return pl.pallas_call(
        _route_kernel,
        grid=(t // tl, heads),
        in_specs=[pl.BlockSpec((2, n_keys, tl), lambda i, h: (h, 0, i))],
        out_specs=[spec] * 4,
        out_shape=[out(BF16), out(BF16), out(jnp.uint32), out(jnp.uint32)],
        compiler_params=_params("arbitrary", "arbitrary"),
        name="route",
    )(st)


GELU_C1 = math.sqrt(2.0 / math.pi)
GELU_C2 = GELU_C1 * 0.044715


def _gelu_tanh(x):
    hx = 0.5 * x
    return hx + hx * jnp.tanh(x * (GELU_C1 + GELU_C2 * (x * x)))


def _packed_rows(words):
    return pltpu.bitcast(jnp.broadcast_to(words, (8, words.shape[1])), BF16)


def _peer_kernel(h2t_ref, u_ref, vt_ref, r2_ref, e2_ref, n1_ref, e1_ref, x1_ref, mod_ref, g_ref, b_ref, y_ref,
                 at_a, at_b, zt_a, zt_b, acc_ref, *, per_token_mod, alpha, n_keys):
    j = pl.program_id(1)
    slab, tq = at_a.shape
    ek = u_ref.shape[0]
    heads = r2_ref.shape[0]
    sub = 16
    col = min(256, tq)
    groups = slab // n_keys

    @pl.when(j == 0)
    def _():
        acc_ref[...] = jnp.zeros(acc_ref.shape, F32)

    def hidden(s, at_ref):
        rows = pl.ds(pl.multiple_of(s * slab, slab), slab)
        at_ref[...] = jnp.dot(u_ref[rows, :], h2t_ref[...], preferred_element_type=F32)

    def mask(s, at_ref, zt_ref):
        for g in range(groups):
            i1 = j * (ek // n_keys) + s * groups + g
            n1w = [n1_ref[h, pl.ds(i1, 1), :] for h in range(heads)]
            e1w = [e1_ref[h, pl.ds(i1, 1), :] for h in range(heads)]
            for c in range(tq // col):
                tok = slice(c * col, (c + 1) * col)
                n1b = [_packed_rows(n1w[h][:, tok]) for h in range(heads)]
                e1b = [_packed_rows(e1w[h][:, tok]) for h in range(heads)]
                for k in range(n_keys // sub):
                    r0 = g * n_keys + k * sub
                    keys = slice(k * sub, (k + 1) * sub)
                    act = _gelu_tanh(at_ref[r0:r0 + sub, tok].astype(BF16))
                    w = jnp.zeros((sub, col), BF16)
                    for h in range(heads):
                        w = jnp.where(r2_ref[h, keys, tok] < n1b[h], w + e2_ref[h, keys, tok] * e1b[h], w)
                    zt_ref[r0:r0 + sub, tok] = w * act

    def project(s, zt_ref):
        acc_ref[...] += jnp.dot(vt_ref[s], zt_ref[...], preferred_element_type=F32)

    def pair(p, carry):
        a = 2 * p
        hidden(a, at_a)
        hidden(a + 1, at_b)
        mask(a, at_a, zt_a)
        project(a, zt_a)
        mask(a + 1, at_b, zt_b)
        project(a + 1, zt_b)
        return carry

    lax.fori_loop(0, ek // (2 * slab), pair, 0)

    @pl.when(j == pl.num_programs(1) - 1)
    def _():
        gate2 = mod_ref[5] if per_token_mod else mod_ref[5, 0]
        ff = acc_ref[...].T
        y_ref[...] = _ln(alpha * x1_ref[...] + gate2 * ff) * g_ref[...] + b_ref[...]


def _peer(h2t, u_b, vt_b, r2, e2, n1, e1, x1, mod, ln_g, ln_b, *, tq, ek, rows_per_mod, alpha):
    d, t = h2t.shape
    ne = u_b.shape[0]
    heads, n_keys, _ = r2.shape
    slab = vt_b.shape[2]
    per_token = rows_per_mod is None
    if per_token:
        mod_spec = pl.BlockSpec((6, tq, d), lambda i, j: (0, i, 0))
    else:
        mod_spec = pl.BlockSpec((6, 1, 1, d), lambda i, j: (0, (i * tq) // rows_per_mod, 0, 0))
    tab = pl.BlockSpec((heads, n_keys, tq), lambda i, j: (0, 0, i))
    vec = lambda a: a.reshape(1, d)
    return pl.pallas_call(
        functools.partial(_peer_kernel, per_token_mod=per_token, alpha=alpha, n_keys=n_keys),
        grid=(t // tq, ne // ek),
        in_specs=[pl.BlockSpec((d, tq), lambda i, j: (0, i)),
                  pl.BlockSpec((ek, d), lambda i, j: (j, 0)),
                  pl.BlockSpec((ek // slab, d, slab), lambda i, j: (j, 0, 0)),
                  tab, tab, tab, tab,
                  pl.BlockSpec((tq, d), lambda i, j: (i, 0)),
                  mod_spec, _full((1, d)), _full((1, d))],
        out_specs=pl.BlockSpec((tq, d), lambda i, j: (i, 0)),
        out_shape=jax.ShapeDtypeStruct((t, d), F32),
        scratch_shapes=[pltpu.VMEM((slab, tq), F32), pltpu.VMEM((slab, tq), F32),
                        pltpu.VMEM((slab, tq), BF16), pltpu.VMEM((slab, tq), BF16), pltpu.VMEM((d, tq), F32)],
        compiler_params=_params("arbitrary", "arbitrary"),
        name="peer_sample" if per_token else "peer_prompt",
    )(h2t, u_b, vt_b, r2, e2, n1, e1, x1, mod, vec(ln_g), vec(ln_b))


def _transpose_kernel(x_ref, o_ref):
    slab = o_ref.shape[2]
    for c in range(o_ref.shape[0]):
        o_ref[c] = x_ref[c * slab:(c + 1) * slab, :].T.astype(o_ref.dtype)


def _transpose_slabs(x, *, rows, slab):
    n, d = x.shape
    return pl.pallas_call(
        _transpose_kernel,
        grid=(n // rows,),
        in_specs=[pl.BlockSpec((rows, d), lambda i: (i, 0))],
        out_specs=pl.BlockSpec((rows // slab, d, slab), lambda i: (i, 0, 0)),
        out_shape=jax.ShapeDtypeStruct((n // slab, d, slab), BF16),
        compiler_params=_params("arbitrary"),
        name="transpose_v",
    )(x)


def kernel(x_prompt, x_sample, cache_k, cache_v, state_conv, page_table, c_prompt, c_sample, w_ada, b_ada, w_in, b_in, conv_w, conv_b, conv_ln_g, conv_ln_b, lambda_q1, lambda_k1, lambda_q2, lambda_k2, attn_subln_g, rel_bias, w_out, b_out, ln1_g, ln1_b, peer_w_query, peer_sub_keys, peer_u, peer_v, ln2_g, ln2_b):
    batch, seq, d = x_prompt.shape
    nb, dec_seq, _ = x_sample.shape
    depth = w_ada.shape[0]
    assert depth == 1 and dec_seq == 1
    heads = cache_k.shape[3]
    hd = cache_k.shape[4]
    assert hd == 2 * ATT_HEAD_DIM
    d_attn = heads * hd
    d_conv = conv_w.shape[2]
    hist = conv_w.shape[1] - 1
    assert hist <= CONV_PAD - 1
    page = cache_k.shape[2]
    n_keys = peer_sub_keys.shape[3]
    ne = peer_u.shape[1]
    alpha = (2 * depth) ** 0.25
    lambda_init = 0.8 - 0.6 * math.exp(-0.3 * 0)
    t_p = batch * seq
    ts = min(TOKEN_TILE, seq)
    tq_peer = min(PEER_TOKEN_TILE, seq)
    ek = min(PEER_EXPERT_TILE, ne)
    assert seq % ts == 0 and nb % LANES == 0 and page >= MAX_DISTANCE

    l = 0
    n_c = batch + nb
    n_c_pad = -(-n_c // 8) * 8
    c_all = jnp.concatenate([c_prompt, c_sample, jnp.zeros((n_c_pad - n_c, d), F32)], axis=0)
    mod = _ada(c_all, w_ada[l], b_ada[l])
    mod_p = mod[:, :batch].reshape(6, batch, 1, d)
    mod_s = mod[:, batch:n_c]

    w_in_b = w_in[l].astype(BF16)
    b_in_r = b_in[l].reshape(1, -1)
    w_out_b = w_out[l].astype(BF16)
    wq_b = peer_w_query[l].astype(BF16)
    sk_b = peer_sub_keys[l].reshape(-1, n_keys, peer_sub_keys.shape[-1]).astype(BF16)
    u_b = peer_u[l].astype(BF16)
    vt_b = _transpose_slabs(peer_v[l], rows=min(512, ne), slab=min(PEER_EXPERT_SLAB, ne))
    lams = [a[l].reshape(1, -1) for a in (lambda_q1, lambda_k1, lambda_q2, lambda_k2)]
    subln_g = attn_subln_g[l]

    xp = x_prompt.reshape(t_p, d)
    glu_p, k_p, v_p, kb_p, qt_p, vt_p = _inproj(xp, mod_p, w_in_b, b_in_r, tm=ts, rows_per_mod=seq,
                                                d_conv=d_conv, d_attn=d_attn)
    cy_p = _conv_prompt(glu_p, conv_w[l], conv_b[l], conv_ln_g[l], conv_ln_b[l], batch=batch, seq=seq, ts=ts)
    bias = _bias_tiles(rel_bias, heads=heads, ts=ts)
    ay_p = _attn_prompt(qt_p, kb_p, vt_p, bias, subln_g, lams, batch=batch, seq=seq, heads=heads, ts=ts,
                        lambda_init=lambda_init)
    x1_p, h2t_p, st_p = _mix(xp, cy_p, ay_p, mod_p, w_out_b, b_out[l], ln1_g[l], ln1_b[l], wq_b, sk_b,
                             tm=ts, rows_per_mod=seq, alpha=alpha)
    r2, e2, n1, e1 = _route(st_p, tl=min(ROUTE_TOKEN_TILE, t_p))
    y_p = _peer(h2t_p, u_b, vt_b, r2, e2, n1, e1, x1_p, mod_p, ln2_g[l], ln2_b[l],
                tq=tq_peer, ek=ek, rows_per_mod=seq, alpha=alpha)

    xs = x_sample.reshape(nb, d)
    glu_s, k_s, v_s, q_s = _inproj(xs, mod_s, w_in_b, b_in_r, tm=nb, rows_per_mod=None,
                                   d_conv=d_conv, d_attn=d_attn)
    state = state_conv[l]
    cy_s = _conv_sample(jnp.swapaxes(state, 0, 1), glu_s, conv_w[l], conv_b[l], conv_ln_g[l], conv_ln_b[l])
    n_pool = cache_k.shape[1]
    pool_rows = lambda a: a.reshape(depth * n_pool, page * heads, hd)
    ay_s = _attn_sample(page_table + l * n_pool, rel_bias, q_s, k_s, v_s, pool_rows(cache_k), pool_rows(cache_v),
                        subln_g, lams, heads=heads, lambda_init=lambda_init)
    x1_s, h2t_s, st_s = _mix(xs, cy_s, ay_s, mod_s, w_out_b, b_out[l], ln1_g[l], ln1_b[l], wq_b, sk_b,
                             tm=nb, rows_per_mod=None, alpha=alpha)
    r2s, e2s, n1s, e1s = _route(st_s, tl=nb)
    y_s = _peer(h2t_s, u_b, vt_b, r2s, e2s, n1s, e1s, x1_s, mod_s, ln2_g[l], ln2_b[l],
                tq=nb, ek=ek, rows_per_mod=None, alpha=alpha)

    kv_p = (depth, batch, seq, heads, hd)
    kv_s = (depth, nb, dec_seq, heads, hd)
    conv_p = glu_p.reshape(batch, seq, d_conv)[:, seq - hist:][None]
    conv_s = jnp.concatenate([state[:, 1:], glu_s[:, None, :]], axis=1)[None]
    return (y_p.reshape(batch, seq, d), y_s.reshape(nb, dec_seq, d),
            k_p.reshape(kv_p), v_p.reshape(kv_p), conv_p,
            k_s.reshape(kv_s), v_s.reshape(kv_s), conv_s)
```

```python
import functools
import math

import jax
import jax.numpy as jnp
from jax import lax
from jax.experimental import pallas as pl
from jax.experimental.pallas import tpu as pltpu

F32 = jnp.float32
BF16 = jnp.bfloat16

LN_EPS = 1e-5
NEG_INF = -1e30
ATT_HEAD_DIM = 64
N_BUCKETS = 32
MAX_DISTANCE = 128
PEER_TOPK = 16
LANES = 128
VMEM_LIMIT_BYTES = 56 * 1024 * 1024
TOKEN_TILE = 512
PEER_TOKEN_TILE = 512
PEER_EXPERT_TILE = 1024
PEER_EXPERT_SLAB = 256
ROUTE_TOKEN_TILE = 256


def _params(*sem):
    return pltpu.CompilerParams(dimension_semantics=sem, vmem_limit_bytes=VMEM_LIMIT_BYTES)


def _ln(x):
    mu = jnp.mean(x, axis=-1, keepdims=True)
    xc = x - mu
    var = jnp.mean(xc * xc, axis=-1, keepdims=True)
    return xc * lax.rsqrt(var + LN_EPS)


def _full(shape):
    return pl.BlockSpec(shape, lambda *_: (0,) * len(shape))


def _ada_kernel(c_ref, w_ref, b_ref, o_ref):
    c = c_ref[...]
    s = c * jax.nn.sigmoid(c)
    o_ref[0] = jnp.dot(s.astype(BF16), w_ref[...].astype(BF16), preferred_element_type=F32) + b_ref[0]


def _ada(c_all, w_ada, b_ada):
    nc, d = c_all.shape
    return pl.pallas_call(
        _ada_kernel,
        grid=(6,),
        in_specs=[_full((nc, d)),
                  pl.BlockSpec((d, d), lambda k: (0, k)),
                  pl.BlockSpec((1, 1, d), lambda k: (k, 0, 0))],
        out_specs=pl.BlockSpec((1, nc, d), lambda k: (k, 0, 0)),
        out_shape=jax.ShapeDtypeStruct((6, nc, d), F32),
        compiler_params=_params("arbitrary"),
        name="ada",
    )(c_all, w_ada, b_ada.reshape(6, 1, d))


def _inproj_kernel(x_ref, mod_ref, w_ref, b_ref, *outs, per_token_mod, d_conv, d_attn, transposed):
    x = x_ref[...]
    if per_token_mod:
        sh1, sc1 = mod_ref[0], mod_ref[1]
    else:
        sh1, sc1 = mod_ref[0, 0], mod_ref[1, 0]
    h = _ln(x) * (1.0 + sc1) + sh1
    z = jnp.dot(h.astype(BF16), w_ref[...], preferred_element_type=F32) + b_ref[...]
    ga = z[:, :d_conv]
    gb = z[:, d_conv:2 * d_conv]
    o = 2 * d_conv
    q = z[:, o:o + d_attn]
    k = z[:, o + d_attn:o + 2 * d_attn]
    v = z[:, o + 2 * d_attn:o + 3 * d_attn]
    glu = ga * jax.nn.sigmoid(gb)
    if transposed:
        glu_ref, k_ref, v_ref, kb_ref, qt_ref, vt_ref = outs
        kb_ref[...] = k.astype(BF16)
        qt_ref[0] = (q * (ATT_HEAD_DIM ** -0.5)).T.astype(BF16)
        vt_ref[0] = v.T.astype(BF16)
    else:
        glu_ref, k_ref, v_ref, q_ref = outs
        q_ref[...] = q
    glu_ref[...] = glu
    heads = d_attn // (2 * ATT_HEAD_DIM)
    for h in range(heads):
        cols = slice(h * 2 * ATT_HEAD_DIM, (h + 1) * 2 * ATT_HEAD_DIM)
        k_ref[pl.ds(h, x.shape[0], stride=heads), :] = k[:, cols]
        v_ref[pl.ds(h, x.shape[0], stride=heads), :] = v[:, cols]


def _inproj(x, mod, w_in, b_in, *, tm, rows_per_mod, d_conv, d_attn):
    t, d = x.shape
    n_in = w_in.shape[1]
    nt = t // tm
    per_token = rows_per_mod is None
    if per_token:
        mod_spec = pl.BlockSpec((6, tm, d), lambda i: (0, i, 0))
    else:
        mod_spec = pl.BlockSpec((6, 1, 1, d), lambda i: (0, (i * tm) // rows_per_mod, 0, 0))
    row = lambda n, dt: (jax.ShapeDtypeStruct((t, n), dt), pl.BlockSpec((tm, n), lambda i: (i, 0)))
    hd = 2 * ATT_HEAD_DIM
    heads = d_attn // hd
    head_rows = (jax.ShapeDtypeStruct((t * heads, hd), F32), pl.BlockSpec((tm * heads, hd), lambda i: (i, 0)))
    outs = [row(d_conv, F32), head_rows, head_rows]
    if per_token:
        outs.append(row(d_attn, F32))
    else:
        outs.append(row(d_attn, BF16))
        tr = (jax.ShapeDtypeStruct((nt, d_attn, tm), BF16), pl.BlockSpec((1, d_attn, tm), lambda i: (i, 0, 0)))
        outs += [tr, tr]
    return pl.pallas_call(
        functools.partial(_inproj_kernel, per_token_mod=per_token, d_conv=d_conv, d_attn=d_attn,
                          transposed=not per_token),
        grid=(nt,),
        in_specs=[pl.BlockSpec((tm, d), lambda i: (i, 0)), mod_spec, _full((d, n_in)), _full((1, n_in))],
        out_specs=[o[1] for o in outs],
        out_shape=[o[0] for o in outs],
        compiler_params=_params("arbitrary"),
        name="inproj_sample" if per_token else "inproj_prompt",
    )(x, mod, w_in, b_in)


CONV_PAD = 32


def _conv_post(acc, cb_ref, g_ref, b_ref):
    y = _ln(acc + cb_ref[...]) * g_ref[...] + b_ref[...]
    return y * jax.nn.sigmoid(y)


def _conv_prompt_kernel(glu_ref, cw_ref, cb_ref, g_ref, b_ref, y_ref, full_ref, *, ts, width, chunk):
    s = pl.program_id(1)
    hist = width - 1

    @pl.when(s == 0)
    def _():
        full_ref[0:CONV_PAD, :] = jnp.zeros((CONV_PAD, full_ref.shape[1]), F32)

    @pl.when(s > 0)
    def _():
        full_ref[0:CONV_PAD, :] = full_ref[ts:ts + CONV_PAD, :]

    full_ref[CONV_PAD:CONV_PAD + ts, :] = glu_ref[...]
    base = CONV_PAD - hist
    for c in range(ts // chunk):
        r0 = c * chunk
        acc = full_ref[base + r0:base + r0 + chunk, :] * cw_ref[0:1, :]
        for w in range(1, width):
            acc = acc + full_ref[base + r0 + w:base + r0 + w + chunk, :] * cw_ref[w:w + 1, :]
        y_ref[r0:r0 + chunk, :] = _conv_post(acc, cb_ref, g_ref, b_ref).astype(y_ref.dtype)


def _conv_prompt(glu, conv_w, conv_b, ln_g, ln_b, *, batch, seq, ts):
    t, dc = glu.shape
    width = conv_w.shape[0]
    ns = seq // ts
    vec = lambda a: a.reshape(1, dc)
    return pl.pallas_call(
        functools.partial(_conv_prompt_kernel, ts=ts, width=width, chunk=min(64, ts)),
        grid=(batch, ns),
        in_specs=[pl.BlockSpec((ts, dc), lambda b, s: (b * ns + s, 0)),
                  _full((width, dc)), _full((1, dc)), _full((1, dc)), _full((1, dc))],
        out_specs=pl.BlockSpec((ts, dc), lambda b, s: (b * ns + s, 0)),
        out_shape=jax.ShapeDtypeStruct((t, dc), BF16),
        scratch_shapes=[pltpu.VMEM((CONV_PAD + ts, dc), F32)],
        compiler_params=_params("arbitrary", "arbitrary"),
        name="conv_prompt",
    )(glu, conv_w, vec(conv_b), vec(ln_g), vec(ln_b))


def _conv_sample_kernel(st_ref, glu_ref, cw_ref, cb_ref, g_ref, b_ref, y_ref, *, width):
    hist = width - 1
    acc = glu_ref[...] * cw_ref[hist:hist + 1, :]
    for w in range(hist):
        acc = acc + st_ref[w] * cw_ref[w:w + 1, :]
    y_ref[...] = _conv_post(acc, cb_ref, g_ref, b_ref).astype(y_ref.dtype)


def _conv_sample(state_t, glu, conv_w, conv_b, ln_g, ln_b):
    hist, nb, dc = state_t.shape
    width = conv_w.shape[0]
    vec = lambda a: a.reshape(1, dc)
    return pl.pallas_call(
        functools.partial(_conv_sample_kernel, width=width),
        grid=(1,),
        in_specs=[_full((hist, nb, dc)), _full((nb, dc)), _full((width, dc)),
                  _full((1, dc)), _full((1, dc)), _full((1, dc))],
        out_specs=_full((nb, dc)),
        out_shape=jax.ShapeDtypeStruct((nb, dc), BF16),
        compiler_params=_params("arbitrary"),
        name="conv_sample",
    )(state_t, glu, conv_w, vec(conv_b), vec(ln_g), vec(ln_b))


def _bucket(n):
    max_exact = N_BUCKETS // 2
    nf = jnp.maximum(n, 1).astype(F32)
    large = max_exact + (jnp.log(nf / max_exact) / math.log(MAX_DISTANCE / max_exact)
                         * (N_BUCKETS - max_exact)).astype(jnp.int32)
    large = jnp.minimum(large, N_BUCKETS - 1)
    return jnp.where(n < max_exact, n, large)


def _bias_of(n, rb_ref, h):
    bucket = _bucket(n)
    far = jnp.full(n.shape, rb_ref[N_BUCKETS - 1, h], F32)
    out = far
    for j in range(N_BUCKETS - 1):
        out = jnp.where(bucket == j, rb_ref[j, h], out)
    return out - far


def _bias_kernel(rb_ref, o_ref, *, ts):
    h = pl.program_id(0)
    d = pl.program_id(1)
    ik = lax.broadcasted_iota(jnp.int32, (ts, ts), 0)
    jq = lax.broadcasted_iota(jnp.int32, (ts, ts), 1)
    n = jq - ik + ts * (1 - d)
    b = _bias_of(jnp.maximum(n, 0), rb_ref, h)
    o_ref[0, 0] = jnp.where(n >= 0, b, NEG_INF)


def _bias_tiles(rel_bias, *, heads, ts):
    return pl.pallas_call(
        functools.partial(_bias_kernel, ts=ts),
        grid=(heads, 2),
        in_specs=[pl.BlockSpec(memory_space=pltpu.SMEM)],
        out_specs=pl.BlockSpec((1, 1, ts, ts), lambda h, d: (h, d, 0, 0)),
        out_shape=jax.ShapeDtypeStruct((heads, 2, ts, ts), F32),
        compiler_params=_params("arbitrary", "arbitrary"),
        name="bias_tiles",
    )(rel_bias)


def _lambda(lq1, lk1, lq2, lk2, lambda_init):
    s1 = jnp.sum(lq1[...] * lk1[...], axis=-1, keepdims=True)
    s2 = jnp.sum(lq2[...] * lk2[...], axis=-1, keepdims=True)
    return jnp.exp(s1) - jnp.exp(s2) + lambda_init


def _attn_kernel(qt_ref, k_ref, vt_ref, bias_ref, g_ref, lq1, lk1, lq2, lk2, o_ref,
                 qs_ref, m_ref, l_ref, acc_ref, sa_ref, sb_ref, *, ts, lambda_init):
    qi = pl.program_id(2)
    d = ATT_HEAD_DIM
    qt = qt_ref[0]
    row = lax.broadcasted_iota(jnp.int32, qt.shape, 0)
    zero = jnp.zeros_like(qt)
    qs_ref[:, :ts] = jnp.where(row < d, qt, zero)
    qs_ref[:, ts:] = jnp.where(row >= d, qt, zero)
    m_ref[...] = jnp.full(m_ref.shape, NEG_INF, F32)
    l_ref[...] = jnp.zeros(l_ref.shape, F32)
    acc_ref[...] = jnp.zeros(acc_ref.shape, F32)

    def scores(ki, bias):
        kblk = k_ref[pl.ds(pl.multiple_of(ki * ts, ts), ts), :]
        s = jnp.dot(kblk, qs_ref[...], preferred_element_type=F32)
        if bias is not None:
            s = s + jnp.concatenate([bias, bias], axis=1)
        return s

    def update(s_ref, ki):
        s = s_ref[...]
        m_prev = m_ref[...]
        m_new = jnp.maximum(m_prev, jnp.max(s, axis=0, keepdims=True))
        alpha = jnp.exp(m_prev - m_new)
        p = jnp.exp(s - m_new)
        l_ref[...] = alpha * l_ref[...] + jnp.sum(p, axis=0, keepdims=True)
        acc_ref[...] = alpha * acc_ref[...] + jnp.dot(vt_ref[ki], p.astype(BF16), preferred_element_type=F32)
        m_ref[...] = m_new

    n_far = jnp.maximum(qi - 1, 0)
    sa_ref[...] = scores(qi, bias_ref[0, 1])

    @pl.when(qi > 0)
    def _():
        sb_ref[...] = scores(qi - 1, bias_ref[0, 0])
        update(sa_ref, qi)

    def far_pair(jj, carry):
        ka = qi - 2 - 2 * jj
        sa_ref[...] = scores(ka, None)
        update(sb_ref, ka + 1)
        sb_ref[...] = scores(ka - 1, None)
        update(sa_ref, ka)
        return carry

    lax.fori_loop(0, n_far // 2, far_pair, 0)

    @pl.when(n_far % 2 == 1)
    def _():
        sa_ref[...] = scores(0, None)
        update(sb_ref, 1)

    @pl.when(qi % 2 == 0)
    def _():
        update(sa_ref, 0)

    @pl.when(qi % 2 == 1)
    def _():
        update(sb_ref, 0)

    lam = _lambda(lq1, lk1, lq2, lk2, lambda_init)
    inv_l = 1.0 / l_ref[...]
    acc = acc_ref[...]
    o = acc[:, :ts] * inv_l[:, :ts] - lam * (acc[:, ts:] * inv_l[:, ts:])
    o = o * lax.rsqrt(jnp.mean(o * o, axis=0, keepdims=True) + LN_EPS)
    o = o * g_ref[...] * (1.0 - lambda_init)
    o_ref[...] = o.T.astype(o_ref.dtype)


def _attn_prompt(qt, kb, vt, bias, subln_g, lams, *, batch, seq, heads, ts, lambda_init):
    t, d_attn = kb.shape
    hd = 2 * ATT_HEAD_DIM
    nq = seq // ts
    lam_spec = _full((1, ATT_HEAD_DIM))
    return pl.pallas_call(
        functools.partial(_attn_kernel, ts=ts, lambda_init=lambda_init),
        grid=(batch, heads, nq),
        in_specs=[pl.BlockSpec((1, hd, ts), lambda b, h, q: (b * nq + q, h, 0)),
                  pl.BlockSpec((seq, hd), lambda b, h, q: (b, h)),
                  pl.BlockSpec((nq, hd, ts), lambda b, h, q: (b, h, 0)),
                  pl.BlockSpec((1, 2, ts, ts), lambda b, h, q: (h, 0, 0, 0)),
                  _full((hd, 1)), lam_spec, lam_spec, lam_spec, lam_spec],
        out_specs=pl.BlockSpec((ts, hd), lambda b, h, q: (b * nq + q, h)),
        out_shape=jax.ShapeDtypeStruct((t, d_attn), BF16),
        scratch_shapes=[pltpu.VMEM((hd, 2 * ts), BF16), pltpu.VMEM((1, 2 * ts), F32),
                        pltpu.VMEM((1, 2 * ts), F32), pltpu.VMEM((hd, 2 * ts), F32),
                        pltpu.VMEM((ts, 2 * ts), F32), pltpu.VMEM((ts, 2 * ts), F32)],
        compiler_params=_params("arbitrary", "arbitrary", "arbitrary"),
        name="attn_prompt",
    )(qt, kb, vt, bias, subln_g.reshape(hd, 1), *lams)


def _decode_kernel(pt_ref, rb_ref, q_ref, kn_ref, vn_ref, g_ref, lq1, lk1, lq2, lk2, *rest,
                   n_pages, page, heads, lambda_init):
    k_refs = rest[:n_pages]
    v_refs = rest[n_pages:2 * n_pages]
    o_ref = rest[2 * n_pages]
    s_ref, bias_ref = rest[2 * n_pages + 1:]
    d = ATT_HEAD_DIM
    hd = 2 * d
    rows = page * heads
    past = n_pages * rows
    nr = 2 * heads
    log_heads = int(math.log2(heads))
    nt = (((1,), (1,)), ((), ()))

    q4 = q_ref[0] * (d ** -0.5)
    lane = lax.broadcasted_iota(jnp.int32, (nr, hd), 1)
    top = lax.broadcasted_iota(jnp.int32, (nr, hd), 0) < heads
    q8 = jnp.concatenate([q4, q4], axis=0)
    q8 = jnp.where(top == (lane < d), q8, 0.0).astype(BF16)

    def own(n_cols):
        r = lax.broadcasted_iota(jnp.int32, (nr, n_cols), 0)
        c = lax.broadcasted_iota(jnp.int32, (nr, n_cols), 1)
        return (r & (heads - 1)) == (c & (heads - 1))

    @pl.when(pl.program_id(0) == 0)
    def _():
        key = lax.shift_right_logical(lax.broadcasted_iota(jnp.int32, (nr, rows), 1), log_heads)
        rh = lax.broadcasted_iota(jnp.int32, (nr, rows), 0) & (heads - 1)
        rh_tail = lax.broadcasted_iota(jnp.int32, (nr, LANES), 0) & (heads - 1)
        bias = jnp.zeros((nr, rows), F32)
        bias_new = jnp.zeros((nr, LANES), F32)
        for h in range(heads):
            bias = jnp.where(rh == h, _bias_of(page - key, rb_ref, h), bias)
            bias_new = jnp.where(rh_tail == h, _bias_of(jnp.zeros((nr, LANES), jnp.int32), rb_ref, h), bias_new)
        bias_ref[:, 0:rows] = bias
        bias_ref[:, rows:rows + LANES] = bias_new

    own_page = own(rows)
    for j in range(n_pages):
        sj = lax.dot_general(q8, k_refs[j][0].astype(BF16), nt, preferred_element_type=F32)
        if j == n_pages - 1:
            sj = sj + bias_ref[:, 0:rows]
        s_ref[:, j * rows:(j + 1) * rows] = jnp.where(own_page, sj, NEG_INF)
    kn = jnp.concatenate([kn_ref[0], jnp.zeros((LANES - heads, hd), F32)], axis=0).astype(BF16)
    s_new = lax.dot_general(q8, kn, nt, preferred_element_type=F32)
    tail_col = lax.broadcasted_iota(jnp.int32, (nr, LANES), 1)
    s_ref[:, past:past + LANES] = jnp.where(own(LANES) & (tail_col < heads),
                                            s_new + bias_ref[:, rows:rows + LANES], NEG_INF)

    s = s_ref[...]
    m = jnp.max(s, axis=-1, keepdims=True)
    p = jnp.exp(s - m)
    l = jnp.sum(p, axis=-1, keepdims=True)
    pb = p.astype(BF16)
    out = jnp.zeros((nr, hd), F32)
    for j in range(n_pages):
        out = out + jnp.dot(pb[:, j * rows:(j + 1) * rows], v_refs[j][0].astype(BF16), preferred_element_type=F32)
    vn = jnp.concatenate([vn_ref[0], jnp.zeros((LANES - heads, hd), F32)], axis=0).astype(BF16)
    out = (out + jnp.dot(pb[:, past:past + LANES], vn, preferred_element_type=F32)) / l
    lam = _lambda(lq1, lk1, lq2, lk2, lambda_init)
    o = out[0:heads, :] - lam * out[heads:nr, :]
    o = o * lax.rsqrt(jnp.mean(o * o, axis=-1, keepdims=True) + LN_EPS)
    o_ref[0] = (o * g_ref[...] * (1.0 - lambda_init)).astype(o_ref.dtype)


def _attn_sample(page_table, rel_bias, q, k_new, v_new, cache_k, cache_v, subln_g, lams, *, heads, lambda_init):
    nb, n_pages = page_table.shape
    n_pool, rows, hd = cache_k.shape
    page = rows // heads
    assert heads & (heads - 1) == 0 and 2 * heads <= 8
    tok = lambda a: a.reshape(nb, heads, hd)
    tok_spec = pl.BlockSpec((1, heads, hd), lambda b, pt: (b, 0, 0))
    lam_spec = pl.BlockSpec((1, ATT_HEAD_DIM), lambda b, pt: (0, 0))
    page_specs = [pl.BlockSpec((1, rows, hd), lambda b, pt, j=j: (pt[b, j], 0, 0)) for j in range(n_pages)]
    grid_spec = pltpu.PrefetchScalarGridSpec(
        num_scalar_prefetch=1,
        grid=(nb,),
        in_specs=[pl.BlockSpec(memory_space=pltpu.SMEM), tok_spec, tok_spec, tok_spec,
                  pl.BlockSpec((1, hd), lambda b, pt: (0, 0)),
                  lam_spec, lam_spec, lam_spec, lam_spec] + page_specs + page_specs,
        out_specs=tok_spec,
        scratch_shapes=[pltpu.VMEM((2 * heads, n_pages * rows + LANES), F32),
                        pltpu.VMEM((2 * heads, rows + LANES), F32)],
    )
    out = pl.pallas_call(
        functools.partial(_decode_kernel, n_pages=n_pages, page=page, heads=heads, lambda_init=lambda_init),
        grid_spec=grid_spec,
        out_shape=jax.ShapeDtypeStruct((nb, heads, hd), BF16),
        compiler_params=_params("arbitrary"),
        name="attn_sample",
    )(page_table, rel_bias, tok(q), tok(k_new), tok(v_new), subln_g.reshape(1, hd), *lams,
      *([cache_k] * n_pages), *([cache_v] * n_pages))
    return out.reshape(nb, heads * hd)


def _mix_kernel(x_ref, cy_ref, ay_ref, mod_ref, wo_ref, bo_ref, g1_ref, b1_ref, wq_ref, sk_ref,
                x1_ref, h2t_ref, st_ref, *, per_token_mod, alpha, d_conv):
    if per_token_mod:
        gate1, sh2, sc2 = mod_ref[2], mod_ref[3], mod_ref[4]
    else:
        gate1, sh2, sc2 = mod_ref[2, 0], mod_ref[3, 0], mod_ref[4, 0]
    mix = (jnp.dot(cy_ref[...], wo_ref[:d_conv, :], preferred_element_type=F32)
           + jnp.dot(ay_ref[...], wo_ref[d_conv:, :], preferred_element_type=F32) + bo_ref[...])
    x1 = _ln(alpha * x_ref[...] + gate1 * mix) * g1_ref[...] + b1_ref[...]
    x1_ref[...] = x1
    h2 = _ln(x1) * (1.0 + sc2) + sh2
    h2b = h2.astype(BF16)
    h2t_ref[...] = h2.T.astype(BF16)
    qh = jnp.dot(h2b, wq_ref[...], preferred_element_type=F32).astype(BF16)
    nk = sk_ref.shape[2]
    for hc in range(sk_ref.shape[0]):
        st_ref[hc] = lax.dot_general(sk_ref[hc], qh[:, hc * nk:(hc + 1) * nk], (((1,), (1,)), ((), ())),
                                     preferred_element_type=F32)


def _mix(x, cy, ay, mod, w_out, b_out, ln_g, ln_b, w_query, sub_keys, *, tm, rows_per_mod, alpha):
    t, d = x.shape
    d_conv = cy.shape[1]
    d_attn = ay.shape[1]
    n_hc, n_keys, half = sub_keys.shape
    per_token = rows_per_mod is None
    if per_token:
        mod_spec = pl.BlockSpec((6, tm, d), lambda i: (0, i, 0))
    else:
        mod_spec = pl.BlockSpec((6, 1, 1, d), lambda i: (0, (i * tm) // rows_per_mod, 0, 0))
    vec = lambda a: a.reshape(1, d)
    return pl.pallas_call(
        functools.partial(_mix_kernel, per_token_mod=per_token, alpha=alpha, d_conv=d_conv),
        grid=(t // tm,),
        in_specs=[pl.BlockSpec((tm, d), lambda i: (i, 0)),
                  pl.BlockSpec((tm, d_conv), lambda i: (i, 0)),
                  pl.BlockSpec((tm, d_attn), lambda i: (i, 0)),
                  mod_spec, _full(w_out.shape), _full((1, d)), _full((1, d)), _full((1, d)),
                  _full(w_query.shape), _full(sub_keys.shape)],
        out_specs=[pl.BlockSpec((tm, d), lambda i: (i, 0)),
                   pl.BlockSpec((d, tm), lambda i: (0, i)),
                   pl.BlockSpec((n_hc, n_keys, tm), lambda i: (0, 0, i))],
        out_shape=[jax.ShapeDtypeStruct((t, d), F32),
                   jax.ShapeDtypeStruct((d, t), BF16),
                   jax.ShapeDtypeStruct((n_hc, n_keys, t), F32)],
        compiler_params=_params("arbitrary"),
        name="mix_sample" if per_token else "mix_prompt",
    )(x, cy, ay, mod, w_out, vec(b_out), vec(ln_g), vec(ln_b), w_query, sub_keys)


def _top_ranks(s):
    work = s
    rank = jnp.full(s.shape, float(PEER_TOPK), F32)
    vals = []
    for r in range(PEER_TOPK):
        m = jnp.max(work, axis=0, keepdims=True)
        hit = work == m
        rank = jnp.where(hit, float(r), rank)
        work = jnp.where(hit, -jnp.inf, work)
        vals.append(m)
    return rank, vals


def _route_kernel(s_ref, r2_ref, e2_ref, n1_ref, e1_ref):
    k = PEER_TOPK
    s1 = s_ref[0]
    s2 = s_ref[1]
    rank1, v1 = _top_ranks(s1)
    rank2, v2 = _top_ranks(s2)
    rowk = lax.broadcasted_iota(jnp.int32, (k,) + s1.shape[1:], 0)

    def stack(vals):
        out = jnp.zeros(rowk.shape, F32)
        for r in range(k):
            out = jnp.where(rowk == r, vals[r], out)
        return out

    v1m = stack(v1)
    v2m = stack(v2)
    row8 = lax.broadcasted_iota(jnp.int32, (8,) + s1.shape[1:], 0)
    cands = [v1[0] + v2m]
    for a in range(1, 8):
        cands.append(jnp.where(row8 < k // (a + 1), v1[a] + v2m[0:8], -jnp.inf))
    cands.append(v1m[8:k] + v2[0])
    work = jnp.concatenate(cands, axis=0)
    thr = None
    for _ in range(k):
        thr = jnp.max(work, axis=0, keepdims=True)
        work = jnp.where(work == thr, -jnp.inf, work)
    e2top = jnp.exp(v2m - v2[0])
    z = jnp.zeros_like(thr)
    n1 = jnp.zeros(s1.shape, F32)
    for a in range(k):
        sel = (v1[a] + v2m) >= thr
        cnt = jnp.sum(jnp.where(sel, 1.0, 0.0), axis=0, keepdims=True)
        z = z + jnp.exp(v1[a] - v1[0]) * jnp.sum(jnp.where(sel, e2top, 0.0), axis=0, keepdims=True)
        n1 = jnp.where(rank1 == float(a), cnt, n1)
    e1 = jnp.where(rank1 < float(k), jnp.exp(s1 - v1[0]) / z, 0.0)
    e2 = jnp.where(rank2 < float(k), jnp.exp(s2 - v2[0]), 0.0)
    r2_ref[0] = rank2.astype(r2_ref.dtype)
    e2_ref[0] = e2.astype(e2_ref.dtype)
    n1_ref[0] = _bf16_pair(n1)
    e1_ref[0] = _bf16_pair(e1)


def _bf16_pair(x):
    bits = pltpu.bitcast(x.astype(BF16).astype(F32), jnp.uint32)
    return bits | lax.shift_right_logical(bits, jnp.uint32(16))


def _route(st, *, tl):
    n_hc, n_keys, t = st.shape
    heads = n_hc // 2
    out = lambda dt: jax.ShapeDtypeStruct((heads, n_keys, t), dt)
    spec = pl.BlockSpec((1, n_keys, tl), lambda i, h: (h, 0, i))
    return pl.pallas_call(
        _route_kernel,
        grid=(t // tl, heads),
        in_specs=[pl.BlockSpec((2, n_keys, tl), lambda i, h: (h, 0, i))],
        out_specs=[spec] * 4,
        out_shape=[out(BF16), out(BF16), out(jnp.uint32), out(jnp.uint32)],
        compiler_params=_params("arbitrary", "arbitrary"),
        name="route",
    )(st)


GELU_C1 = math.sqrt(2.0 / math.pi)
GELU_C2 = GELU_C1 * 0.044715


def _gelu_tanh(x):
    hx = 0.5 * x
    return hx + hx * jnp.tanh(x * (GELU_C1 + GELU_C2 * (x * x)))


def _packed_rows(words):
    return pltpu.bitcast(jnp.broadcast_to(words, (8, words.shape[1])), BF16)


def _peer_kernel(h2t_ref, u_ref, vt_ref, r2_ref, e2_ref, n1_ref, e1_ref, x1_ref, mod_ref, g_ref, b_ref, y_ref,
                 at0_ref, at1_ref, zt0_ref, zt1_ref, acc_ref, h2s_ref, *, per_token_mod, alpha, n_keys, slab, n_blocks):
    t = pl.program_id(1)
    ek, tq = at0_ref.shape
    heads = r2_ref.shape[0]
    sub = 16
    col = min(256, tq)
    groups = slab // n_keys
    n_slabs = ek // slab

    def hidden(s, at_ref):
        rows = pl.ds(pl.multiple_of(s * slab, slab), slab)
        u = pltpu.bitcast(u_ref[pl.ds(pl.multiple_of(s * (slab // 2), slab // 2), slab // 2), :], BF16)
        at_ref[rows, :] = jnp.dot(u, h2s_ref[...], preferred_element_type=F32)

    def mask(s, at_ref, zt_ref):
        base = pl.multiple_of(s * slab, slab)
        for g in range(groups):
            i1 = (t - 1) * (ek // n_keys) + s * groups + g
            n1w = [n1_ref[h, pl.ds(i1, 1), :] for h in range(heads)]
            e1w = [e1_ref[h, pl.ds(i1, 1), :] for h in range(heads)]
            for c in range(tq // col):
                tok = slice(c * col, (c + 1) * col)
                n1b = [_packed_rows(n1w[h][:, tok]) for h in range(heads)]
                e1b = [_packed_rows(e1w[h][:, tok]) for h in range(heads)]
                for k in range(n_keys // sub):
                    rows = pl.ds(base + (g * n_keys + k * sub), sub)
                    keys = slice(k * sub, (k + 1) * sub)
                    act = _gelu_tanh(at_ref[rows, tok].astype(BF16))
                    w = jnp.zeros((sub, col), BF16)
                    for h in range(heads):
                        w = jnp.where(r2_ref[h, keys, tok] < n1b[h], w + e2_ref[h, keys, tok] * e1b[h], w)
                    zt_ref[rows, tok] = w * act

    def project(piece, zt_ref):
        rows = pl.ds(pl.multiple_of(piece * slab, slab), slab)
        vt = pltpu.bitcast(vt_ref[pl.ds(pl.multiple_of(piece * (slab // 2), slab // 2), slab // 2), :], BF16)
        acc_ref[rows, :] += jnp.dot(vt, zt_ref[...], preferred_element_type=F32)

    n_pieces = acc_ref.shape[0] // slab
    per_iter = n_pieces // n_slabs

    def full_step(at_cur, at_next, zt_cur, zt_prev):
        for s in range(n_slabs):
            hidden(s, at_next)
            mask(s, at_cur, zt_cur)
            for i in range(per_iter):
                project(s * per_iter + i, zt_prev)

    @pl.when(t == 0)
    def _():
        acc_ref[...] = jnp.zeros(acc_ref.shape, F32)
        zt1_ref[...] = jnp.zeros(zt1_ref.shape, BF16)
        h2s_ref[...] = h2t_ref[...]

        def body(s, carry):
            hidden(s, at0_ref)
            return carry

        lax.fori_loop(0, n_slabs, body, 0)

    @pl.when((t >= 1) & (t <= n_blocks) & (t % 2 == 1))
    def _():
        full_step(at0_ref, at1_ref, zt0_ref, zt1_ref)

    @pl.when((t >= 1) & (t <= n_blocks) & (t % 2 == 0))
    def _():
        full_step(at1_ref, at0_ref, zt1_ref, zt0_ref)

    @pl.when(t == n_blocks + 1)
    def _():
        last = zt0_ref if (n_blocks - 1) % 2 == 0 else zt1_ref

        def body(p, carry):
            project(p, last)
            return carry

        lax.fori_loop(0, n_pieces, body, 0)
        gate2 = mod_ref[5] if per_token_mod else mod_ref[5, 0]
        ff = acc_ref[...].T
        y_ref[...] = _ln(alpha * x1_ref[...] + gate2 * ff) * g_ref[...] + b_ref[...]


def _peer(h2t, u_b, vt_b, r2, e2, n1, e1, x1, mod, ln_g, ln_b, *, tq, ek, slab, rows_per_mod, alpha):
    d, t = h2t.shape
    ne = 2 * u_b.shape[0]
    heads, n_keys, _ = r2.shape
    n_blocks = ne // ek
    per_token = rows_per_mod is None
    if per_token:
        mod_spec = pl.BlockSpec((6, tq, d), lambda i, j: (0, i, 0))
    else:
        mod_spec = pl.BlockSpec((6, 1, 1, d), lambda i, j: (0, (i * tq) // rows_per_mod, 0, 0))
    tab = pl.BlockSpec((heads, n_keys, tq), lambda i, j: (0, 0, i))
    vec = lambda a: a.reshape(1, d)
    clamp = lambda b: jnp.clip(b, 0, n_blocks - 1)
    return pl.pallas_call(
        functools.partial(_peer_kernel, per_token_mod=per_token, alpha=alpha, n_keys=n_keys, slab=slab,
                          n_blocks=n_blocks),
        grid=(t // tq, n_blocks + 2),
        in_specs=[pl.BlockSpec((d, tq), lambda i, j: (0, i)),
                  pl.BlockSpec((ek // 2, d), lambda i, j: (clamp(j), 0)),
                  pl.BlockSpec((d // 2, ek), lambda i, j: (0, clamp(j - 2))),
                  tab, tab, tab, tab,
                  pl.BlockSpec((tq, d), lambda i, j: (i, 0)),
                  mod_spec, _full((1, d)), _full((1, d))],
        out_specs=pl.BlockSpec((tq, d), lambda i, j: (i, 0)),
        out_shape=jax.ShapeDtypeStruct((t, d), F32),
        scratch_shapes=[pltpu.VMEM((ek, tq), F32), pltpu.VMEM((ek, tq), F32),
                        pltpu.VMEM((ek, tq), BF16), pltpu.VMEM((ek, tq), BF16), pltpu.VMEM((d, tq), F32),
                        pltpu.VMEM((d, tq), BF16)],
        compiler_params=_params("arbitrary", "arbitrary"),
        name="peer_sample" if per_token else "peer_prompt",
    )(h2t, u_b, vt_b, r2, e2, n1, e1, x1, mod, vec(ln_g), vec(ln_b))


def _pack_kernel(x_ref, o_ref, *, transpose):
    x = x_ref[...]
    if transpose:
        x = x.T
    o_ref[...] = pltpu.bitcast(x.astype(BF16), jnp.uint32)


def _pack_bf16(x, *, rows, transpose):
    n, d = x.shape
    if transpose:
        out_spec, out_shape = pl.BlockSpec((d // 2, rows), lambda i: (0, i)), (d // 2, n)
    else:
        out_spec, out_shape = pl.BlockSpec((rows // 2, d), lambda i: (i, 0)), (n // 2, d)
    return pl.pallas_call(
        functools.partial(_pack_kernel, transpose=transpose),
        grid=(n // rows,),
        in_specs=[pl.BlockSpec((rows, d), lambda i: (i, 0))],
        out_specs=out_spec,
        out_shape=jax.ShapeDtypeStruct(out_shape, jnp.uint32),
        compiler_params=_params("arbitrary"),
        name="pack_vt" if transpose else "pack_u",
    )(x)


def kernel(x_prompt, x_sample, cache_k, cache_v, state_conv, page_table, c_prompt, c_sample, w_ada, b_ada, w_in, b_in, conv_w, conv_b, conv_ln_g, conv_ln_b, lambda_q1, lambda_k1, lambda_q2, lambda_k2, attn_subln_g, rel_bias, w_out, b_out, ln1_g, ln1_b, peer_w_query, peer_sub_keys, peer_u, peer_v, ln2_g, ln2_b):
    batch, seq, d = x_prompt.shape
    nb, dec_seq, _ = x_sample.shape
    depth = w_ada.shape[0]
    assert depth == 1 and dec_seq == 1
    heads = cache_k.shape[3]
    hd = cache_k.shape[4]
    assert hd == 2 * ATT_HEAD_DIM
    d_attn = heads * hd
    d_conv = conv_w.shape[2]
    hist = conv_w.shape[1] - 1
    assert hist <= CONV_PAD - 1
    page = cache_k.shape[2]
    n_keys = peer_sub_keys.shape[3]
    ne = peer_u.shape[1]
    alpha = (2 * depth) ** 0.25
    lambda_init = 0.8 - 0.6 * math.exp(-0.3 * 0)
    t_p = batch * seq
    ts = min(TOKEN_TILE, seq)
    tq_peer = min(PEER_TOKEN_TILE, seq)
    ek = min(PEER_EXPERT_TILE, ne)
    assert seq % ts == 0 and nb % LANES == 0 and page >= MAX_DISTANCE

    l = 0
    n_c = batch + nb
    n_c_pad = -(-n_c // 8) * 8
    c_all = jnp.concatenate([c_prompt, c_sample, jnp.zeros((n_c_pad - n_c, d), F32)], axis=0)
    mod = _ada(c_all, w_ada[l], b_ada[l])
    mod_p = mod[:, :batch].reshape(6, batch, 1, d)
    mod_s = mod[:, batch:n_c]

    w_in_b = w_in[l].astype(BF16)
    b_in_r = b_in[l].reshape(1, -1)
    w_out_b = w_out[l].astype(BF16)
    wq_b = peer_w_query[l].astype(BF16)
    sk_b = peer_sub_keys[l].reshape(-1, n_keys, peer_sub_keys.shape[-1]).astype(BF16)
    u_b = _pack_bf16(peer_u[l], rows=min(512, ne), transpose=False)
    vt_b = _pack_bf16(peer_v[l], rows=min(512, ne), transpose=True)
    lams = [a[l].reshape(1, -1) for a in (lambda_q1, lambda_k1, lambda_q2, lambda_k2)]
    subln_g = attn_subln_g[l]

    xp = x_prompt.reshape(t_p, d)
    glu_p, k_p, v_p, kb_p, qt_p, vt_p = _inproj(xp, mod_p, w_in_b, b_in_r, tm=ts, rows_per_mod=seq,
                                                d_conv=d_conv, d_attn=d_attn)
    cy_p = _conv_prompt(glu_p, conv_w[l], conv_b[l], conv_ln_g[l], conv_ln_b[l], batch=batch, seq=seq, ts=ts)
    bias = _bias_tiles(rel_bias, heads=heads, ts=ts)
    ay_p = _attn_prompt(qt_p, kb_p, vt_p, bias, subln_g, lams, batch=batch, seq=seq, heads=heads, ts=ts,
                        lambda_init=lambda_init)
    x1_p, h2t_p, st_p = _mix(xp, cy_p, ay_p, mod_p, w_out_b, b_out[l], ln1_g[l], ln1_b[l], wq_b, sk_b,
                             tm=ts, rows_per_mod=seq, alpha=alpha)
    r2, e2, n1, e1 = _route(st_p, tl=min(ROUTE_TOKEN_TILE, t_p))
    y_p = _peer(h2t_p, u_b, vt_b, r2, e2, n1, e1, x1_p, mod_p, ln2_g[l], ln2_b[l],
                tq=tq_peer, ek=ek, slab=min(PEER_EXPERT_SLAB, ek), rows_per_mod=seq, alpha=alpha)

    xs = x_sample.reshape(nb, d)
    glu_s, k_s, v_s, q_s = _inproj(xs, mod_s, w_in_b, b_in_r, tm=nb, rows_per_mod=None,
                                   d_conv=d_conv, d_attn=d_attn)
    state = state_conv[l]
    cy_s = _conv_sample(jnp.swapaxes(state, 0, 1), glu_s, conv_w[l], conv_b[l], conv_ln_g[l], conv_ln_b[l])
    n_pool = cache_k.shape[1]
    pool_rows = lambda a: a.reshape(depth * n_pool, page * heads, hd)
    ay_s = _attn_sample(page_table + l * n_pool, rel_bias, q_s, k_s, v_s, pool_rows(cache_k), pool_rows(cache_v),
                        subln_g, lams, heads=heads, lambda_init=lambda_init)
    x1_s, h2t_s, st_s = _mix(xs, cy_s, ay_s, mod_s, w_out_b, b_out[l], ln1_g[l], ln1_b[l], wq_b, sk_b,
                             tm=nb, rows_per_mod=None, alpha=alpha)
    r2s, e2s, n1s, e1s = _route(st_s, tl=nb)
    y_s = _peer(h2t_s, u_b, vt_b, r2s, e2s, n1s, e1s, x1_s, mod_s, ln2_g[l], ln2_b[l],
                tq=nb, ek=ek, slab=min(PEER_EXPERT_SLAB, ek), rows_per_mod=None, alpha=alpha)

    kv_p = (depth, batch, seq, heads, hd)
    kv_s = (depth, nb, dec_seq, heads, hd)
    conv_p = glu_p.reshape(batch, seq, d_conv)[:, seq - hist:][None]
    conv_s = jnp.concatenate([state[:, 1:], glu_s[:, None, :]], axis=1)[None]
    return (y_p.reshape(batch, seq, d), y_s.reshape(nb, dec_seq, d),
            k_p.reshape(kv_p), v_p.reshape(kv_p), conv_p,
            k_s.reshape(kv_s), v_s.reshape(kv_s), conv_s)
```

```python
import functools
import math

import jax
import jax.numpy as jnp
from jax import lax
from jax.experimental import pallas as pl
from jax.experimental.pallas import tpu as pltpu

F32 = jnp.float32
BF16 = jnp.bfloat16

LN_EPS = 1e-5
NEG_INF = -1e30
ATT_HEAD_DIM = 64
N_BUCKETS = 32
MAX_DISTANCE = 128
PEER_TOPK = 16
LOG2E = math.log2(math.e)
LANES = 128
VMEM_LIMIT_BYTES = 56 * 1024 * 1024
TOKEN_TILE = 512
PEER_TOKEN_TILE = 512
PEER_EXPERT_TILE = 1024
PEER_EXPERT_SLAB = 256
ROUTE_TOKEN_TILE = 256


def _params(*sem):
    return pltpu.CompilerParams(dimension_semantics=sem, vmem_limit_bytes=VMEM_LIMIT_BYTES)


def _ln(x):
    mu = jnp.mean(x, axis=-1, keepdims=True)
    xc = x - mu
    var = jnp.mean(xc * xc, axis=-1, keepdims=True)
    return xc * lax.rsqrt(var + LN_EPS)


def _full(shape):
    return pl.BlockSpec(shape, lambda *_: (0,) * len(shape))


def _ada_kernel(c_ref, w_ref, b_ref, o_ref):
    c = c_ref[...]
    s = c * jax.nn.sigmoid(c)
    o_ref[0] = jnp.dot(s.astype(BF16), w_ref[...].astype(BF16), preferred_element_type=F32) + b_ref[0]


def _ada(c_all, w_ada, b_ada):
    nc, d = c_all.shape
    return pl.pallas_call(
        _ada_kernel,
        grid=(6,),
        in_specs=[_full((nc, d)),
                  pl.BlockSpec((d, d), lambda k: (0, k)),
                  pl.BlockSpec((1, 1, d), lambda k: (k, 0, 0))],
        out_specs=pl.BlockSpec((1, nc, d), lambda k: (k, 0, 0)),
        out_shape=jax.ShapeDtypeStruct((6, nc, d), F32),
        compiler_params=_params("arbitrary"),
        name="ada",
    )(c_all, w_ada, b_ada.reshape(6, 1, d))


def _inproj_kernel(x_ref, mod_ref, w_ref, b_ref, *outs, per_token_mod, d_conv, d_attn, transposed):
    x = x_ref[...]
    if per_token_mod:
        sh1, sc1 = mod_ref[0], mod_ref[1]
    else:
        sh1, sc1 = mod_ref[0, 0], mod_ref[1, 0]
    h = _ln(x) * (1.0 + sc1) + sh1
    z = jnp.dot(h.astype(BF16), w_ref[...], preferred_element_type=F32) + b_ref[...]
    ga = z[:, :d_conv]
    gb = z[:, d_conv:2 * d_conv]
    o = 2 * d_conv
    q = z[:, o:o + d_attn]
    k = z[:, o + d_attn:o + 2 * d_attn]
    v = z[:, o + 2 * d_attn:o + 3 * d_attn]
    glu = ga * jax.nn.sigmoid(gb)
    if transposed:
        glu_ref, k_ref, v_ref, kb_ref, qt_ref, vt_ref = outs
        kb_ref[...] = k.astype(BF16)
        qt_ref[0] = (q * (ATT_HEAD_DIM ** -0.5 * LOG2E)).T.astype(BF16)
        vt_ref[0] = v.T.astype(BF16)
    else:
        glu_ref, k_ref, v_ref, q_ref = outs
        q_ref[...] = q
    glu_ref[...] = glu
    heads = d_attn // (2 * ATT_HEAD_DIM)
    for h in range(heads):
        cols = slice(h * 2 * ATT_HEAD_DIM, (h + 1) * 2 * ATT_HEAD_DIM)
        k_ref[pl.ds(h, x.shape[0], stride=heads), :] = k[:, cols]
        v_ref[pl.ds(h, x.shape[0], stride=heads), :] = v[:, cols]


def _inproj(x, mod, w_in, b_in, *, tm, rows_per_mod, d_conv, d_attn):
    t, d = x.shape
    n_in = w_in.shape[1]
    nt = t // tm
    per_token = rows_per_mod is None
    if per_token:
        mod_spec = pl.BlockSpec((6, tm, d), lambda i: (0, i, 0))
    else:
        mod_spec = pl.BlockSpec((6, 1, 1, d), lambda i: (0, (i * tm) // rows_per_mod, 0, 0))
    row = lambda n, dt: (jax.ShapeDtypeStruct((t, n), dt), pl.BlockSpec((tm, n), lambda i: (i, 0)))
    hd = 2 * ATT_HEAD_DIM
    heads = d_attn // hd
    head_rows = (jax.ShapeDtypeStruct((t * heads, hd), F32), pl.BlockSpec((tm * heads, hd), lambda i: (i, 0)))
    outs = [row(d_conv, F32), head_rows, head_rows]
    if per_token:
        outs.append(row(d_attn, F32))
    else:
        outs.append(row(d_attn, BF16))
        tr = (jax.ShapeDtypeStruct((nt, d_attn, tm), BF16), pl.BlockSpec((1, d_attn, tm), lambda i: (i, 0, 0)))
        outs += [tr, tr]
    return pl.pallas_call(
        functools.partial(_inproj_kernel, per_token_mod=per_token, d_conv=d_conv, d_attn=d_attn,
                          transposed=not per_token),
        grid=(nt,),
        in_specs=[pl.BlockSpec((tm, d), lambda i: (i, 0)), mod_spec, _full((d, n_in)), _full((1, n_in))],
        out_specs=[o[1] for o in outs],
        out_shape=[o[0] for o in outs],
        compiler_params=_params("arbitrary"),
        name="inproj_sample" if per_token else "inproj_prompt",
    )(x, mod, w_in, b_in)


CONV_PAD = 32


def _conv_post(acc, cb_ref, g_ref, b_ref):
    y = _ln(acc + cb_ref[...]) * g_ref[...] + b_ref[...]
    return y * jax.nn.sigmoid(y)


def _conv_prompt_kernel(glu_ref, cw_ref, cb_ref, g_ref, b_ref, y_ref, full_ref, sh_ref, *, ts, width, chunk):
    s = pl.program_id(1)
    hist = width - 1

    @pl.when(s == 0)
    def _():
        full_ref[0:CONV_PAD, :] = jnp.zeros((CONV_PAD, full_ref.shape[1]), F32)

    @pl.when(s > 0)
    def _():
        full_ref[0:CONV_PAD, :] = full_ref[ts:ts + CONV_PAD, :]

    full_ref[CONV_PAD:CONV_PAD + ts, :] = glu_ref[...]
    n_sh = sh_ref.shape[1]
    for r in range(1, 8):
        sh_ref[r - 1] = full_ref[r:r + n_sh, :]
    base = CONV_PAD - hist
    for c in range(ts // chunk):
        r0 = c * chunk
        acc = None
        for w in range(width):
            q, r = divmod(base + w, 8)
            rows = slice(r0 + 8 * q, r0 + 8 * q + chunk)
            tap = (full_ref[rows, :] if r == 0 else sh_ref[r - 1, rows, :]) * cw_ref[w:w + 1, :]
            acc = tap if acc is None else acc + tap
        y_ref[r0:r0 + chunk, :] = _conv_post(acc, cb_ref, g_ref, b_ref).astype(y_ref.dtype)


def _conv_prompt(glu, conv_w, conv_b, ln_g, ln_b, *, batch, seq, ts):
    t, dc = glu.shape
    width = conv_w.shape[0]
    ns = seq // ts
    vec = lambda a: a.reshape(1, dc)
    return pl.pallas_call(
        functools.partial(_conv_prompt_kernel, ts=ts, width=width, chunk=min(64, ts)),
        grid=(batch, ns),
        in_specs=[pl.BlockSpec((ts, dc), lambda b, s: (b * ns + s, 0)),
                  _full((width, dc)), _full((1, dc)), _full((1, dc)), _full((1, dc))],
        out_specs=pl.BlockSpec((ts, dc), lambda b, s: (b * ns + s, 0)),
        out_shape=jax.ShapeDtypeStruct((t, dc), BF16),
        scratch_shapes=[pltpu.VMEM((CONV_PAD + ts, dc), F32), pltpu.VMEM((7, CONV_PAD + ts - 8, dc), F32)],
        compiler_params=_params("arbitrary", "arbitrary"),
        name="conv_prompt",
    )(glu, conv_w, vec(conv_b), vec(ln_g), vec(ln_b))


def _conv_sample_kernel(st_ref, glu_ref, cw_ref, cb_ref, g_ref, b_ref, y_ref, *, width):
    hist = width - 1
    acc = glu_ref[...] * cw_ref[hist:hist + 1, :]
    for w in range(hist):
        acc = acc + st_ref[w] * cw_ref[w:w + 1, :]
    y_ref[...] = _conv_post(acc, cb_ref, g_ref, b_ref).astype(y_ref.dtype)


def _conv_sample(state_t, glu, conv_w, conv_b, ln_g, ln_b):
    hist, nb, dc = state_t.shape
    width = conv_w.shape[0]
    vec = lambda a: a.reshape(1, dc)
    return pl.pallas_call(
        functools.partial(_conv_sample_kernel, width=width),
        grid=(1,),
        in_specs=[_full((hist, nb, dc)), _full((nb, dc)), _full((width, dc)),
                  _full((1, dc)), _full((1, dc)), _full((1, dc))],
        out_specs=_full((nb, dc)),
        out_shape=jax.ShapeDtypeStruct((nb, dc), BF16),
        compiler_params=_params("arbitrary"),
        name="conv_sample",
    )(state_t, glu, conv_w, vec(conv_b), vec(ln_g), vec(ln_b))


def _bucket(n):
    max_exact = N_BUCKETS // 2
    nf = jnp.maximum(n, 1).astype(F32)
    large = max_exact + (jnp.log(nf / max_exact) / math.log(MAX_DISTANCE / max_exact)
                         * (N_BUCKETS - max_exact)).astype(jnp.int32)
    large = jnp.minimum(large, N_BUCKETS - 1)
    return jnp.where(n < max_exact, n, large)


def _bias_of(n, rb_ref, h):
    bucket = _bucket(n)
    far = jnp.full(n.shape, rb_ref[N_BUCKETS - 1, h], F32)
    out = far
    for j in range(N_BUCKETS - 1):
        out = jnp.where(bucket == j, rb_ref[j, h], out)
    return out - far


def _bias_kernel(rb_ref, o_ref, *, ts):
    h = pl.program_id(0)
    d = pl.program_id(1)
    ik = lax.broadcasted_iota(jnp.int32, (ts, ts), 0)
    jq = lax.broadcasted_iota(jnp.int32, (ts, ts), 1)
    n = jq - ik + ts * (1 - d)
    b = _bias_of(jnp.maximum(n, 0), rb_ref, h)
    o_ref[0, 0] = jnp.where(n >= 0, b * LOG2E, NEG_INF)


def _bias_tiles(rel_bias, *, heads, ts):
    return pl.pallas_call(
        functools.partial(_bias_kernel, ts=ts),
        grid=(heads, 2),
        in_specs=[pl.BlockSpec(memory_space=pltpu.SMEM)],
        out_specs=pl.BlockSpec((1, 1, ts, ts), lambda h, d: (h, d, 0, 0)),
        out_shape=jax.ShapeDtypeStruct((heads, 2, ts, ts), F32),
        compiler_params=_params("arbitrary", "arbitrary"),
        name="bias_tiles",
    )(rel_bias)


def _lambda(lq1, lk1, lq2, lk2, lambda_init):
    s1 = jnp.sum(lq1[...] * lk1[...], axis=-1, keepdims=True)
    s2 = jnp.sum(lq2[...] * lk2[...], axis=-1, keepdims=True)
    return jnp.exp(s1) - jnp.exp(s2) + lambda_init


def _attn_kernel(qt_ref, k_ref, vt_ref, bias_ref, g_ref, lq1, lk1, lq2, lk2, o_ref,
                 qs_ref, m_ref, l_ref, acc_ref, sa_ref, sb_ref, *, ts, lambda_init):
    qi = pl.program_id(2)
    d = ATT_HEAD_DIM
    qt = qt_ref[0]
    row = lax.broadcasted_iota(jnp.int32, qt.shape, 0)
    zero = jnp.zeros_like(qt)
    qs_ref[:, :ts] = jnp.where(row < d, qt, zero)
    qs_ref[:, ts:] = jnp.where(row >= d, qt, zero)
    m_ref[...] = jnp.full(m_ref.shape, NEG_INF, F32)
    l_ref[...] = jnp.zeros(l_ref.shape, F32)
    acc_ref[...] = jnp.zeros(acc_ref.shape, F32)

    def scores(ki, bias):
        kblk = k_ref[pl.ds(pl.multiple_of(ki * ts, ts), ts), :]
        s = jnp.dot(kblk, qs_ref[...], preferred_element_type=F32)
        if bias is not None:
            s = s + jnp.concatenate([bias, bias], axis=1)
        return s

    def update(s_ref, ki):
        s = s_ref[...]
        m_prev = m_ref[...]
        m_new = jnp.maximum(m_prev, jnp.max(s, axis=0, keepdims=True))
        alpha = jnp.exp2(m_prev - m_new)
        p = jnp.exp2(s - m_new)
        l_ref[...] = alpha * l_ref[...] + jnp.sum(p, axis=0, keepdims=True)
        acc_ref[...] = alpha * acc_ref[...] + jnp.dot(vt_ref[ki], p.astype(BF16), preferred_element_type=F32)
        m_ref[...] = m_new

    n_far = jnp.maximum(qi - 1, 0)
    sa_ref[...] = scores(qi, bias_ref[0, 1])

    @pl.when(qi > 0)
    def _():
        sb_ref[...] = scores(qi - 1, bias_ref[0, 0])
        update(sa_ref, qi)

    def far_pair(jj, carry):
        ka = qi - 2 - 2 * jj
        sa_ref[...] = scores(ka, None)
        update(sb_ref, ka + 1)
        sb_ref[...] = scores(ka - 1, None)
        update(sa_ref, ka)
        return carry

    lax.fori_loop(0, n_far // 2, far_pair, 0)

    @pl.when(n_far % 2 == 1)
    def _():
        sa_ref[...] = scores(0, None)
        update(sb_ref, 1)

    @pl.when(qi % 2 == 0)
    def _():
        update(sa_ref, 0)

    @pl.when(qi % 2 == 1)
    def _():
        update(sb_ref, 0)

    lam = _lambda(lq1, lk1, lq2, lk2, lambda_init)
    inv_l = 1.0 / l_ref[...]
    acc = acc_ref[...]
    o = acc[:, :ts] * inv_l[:, :ts] - lam * (acc[:, ts:] * inv_l[:, ts:])
    o = o * lax.rsqrt(jnp.mean(o * o, axis=0, keepdims=True) + LN_EPS)
    o = o * g_ref[...] * (1.0 - lambda_init)
    o_ref[...] = o.T.astype(o_ref.dtype)


def _attn_prompt(qt, kb, vt, bias, subln_g, lams, *, batch, seq, heads, ts, lambda_init):
    t, d_attn = kb.shape
    hd = 2 * ATT_HEAD_DIM
    nq = seq // ts
    lam_spec = _full((1, ATT_HEAD_DIM))
    return pl.pallas_call(
        functools.partial(_attn_kernel, ts=ts, lambda_init=lambda_init),
        grid=(batch, heads, nq),
        in_specs=[pl.BlockSpec((1, hd, ts), lambda b, h, q: (b * nq + q, h, 0)),
                  pl.BlockSpec((seq, hd), lambda b, h, q: (b, h)),
                  pl.BlockSpec((nq, hd, ts), lambda b, h, q: (b, h, 0)),
                  pl.BlockSpec((1, 2, ts, ts), lambda b, h, q: (h, 0, 0, 0)),
                  _full((hd, 1)), lam_spec, lam_spec, lam_spec, lam_spec],
        out_specs=pl.BlockSpec((ts, hd), lambda b, h, q: (b * nq + q, h)),
        out_shape=jax.ShapeDtypeStruct((t, d_attn), BF16),
        scratch_shapes=[pltpu.VMEM((hd, 2 * ts), BF16), pltpu.VMEM((1, 2 * ts), F32),
                        pltpu.VMEM((1, 2 * ts), F32), pltpu.VMEM((hd, 2 * ts), F32),
                        pltpu.VMEM((ts, 2 * ts), F32), pltpu.VMEM((ts, 2 * ts), F32)],
        compiler_params=_params("arbitrary", "arbitrary", "arbitrary"),
        name="attn_prompt",
    )(qt, kb, vt, bias, subln_g.reshape(hd, 1), *lams)


def _decode_kernel(pt_ref, rb_ref, q_ref, kn_ref, vn_ref, g_ref, lq1, lk1, lq2, lk2, *rest,
                   n_pages, page, heads, lambda_init):
    k_refs = rest[:n_pages]
    v_refs = rest[n_pages:2 * n_pages]
    o_ref = rest[2 * n_pages]
    s_ref, bias_ref = rest[2 * n_pages + 1:]
    d = ATT_HEAD_DIM
    hd = 2 * d
    rows = page * heads
    past = n_pages * rows
    nr = 2 * heads
    log_heads = int(math.log2(heads))
    nt = (((1,), (1,)), ((), ()))

    q4 = q_ref[0] * (d ** -0.5)
    lane = lax.broadcasted_iota(jnp.int32, (nr, hd), 1)
    top = lax.broadcasted_iota(jnp.int32, (nr, hd), 0) < heads
    q8 = jnp.concatenate([q4, q4], axis=0)
    q8 = jnp.where(top == (lane < d), q8, 0.0).astype(BF16)

    def own(n_cols):
        r = lax.broadcasted_iota(jnp.int32, (nr, n_cols), 0)
        c = lax.broadcasted_iota(jnp.int32, (nr, n_cols), 1)
        return (r & (heads - 1)) == (c & (heads - 1))

    @pl.when(pl.program_id(0) == 0)
    def _():
        key = lax.shift_right_logical(lax.broadcasted_iota(jnp.int32, (nr, rows), 1), log_heads)
        rh = lax.broadcasted_iota(jnp.int32, (nr, rows), 0) & (heads - 1)
        rh_tail = lax.broadcasted_iota(jnp.int32, (nr, LANES), 0) & (heads - 1)
        bias = jnp.zeros((nr, rows), F32)
        bias_new = jnp.zeros((nr, LANES), F32)
        for h in range(heads):
            bias = jnp.where(rh == h, _bias_of(page - key, rb_ref, h), bias)
            bias_new = jnp.where(rh_tail == h, _bias_of(jnp.zeros((nr, LANES), jnp.int32), rb_ref, h), bias_new)
        bias_ref[:, 0:rows] = bias
        bias_ref[:, rows:rows + LANES] = bias_new

    own_page = own(rows)
    for j in range(n_pages):
        sj = lax.dot_general(q8, k_refs[j][0].astype(BF16), nt, preferred_element_type=F32)
        if j == n_pages - 1:
            sj = sj + bias_ref[:, 0:rows]
        s_ref[:, j * rows:(j + 1) * rows] = jnp.where(own_page, sj, NEG_INF)
    kn = jnp.concatenate([kn_ref[0], jnp.zeros((LANES - heads, hd), F32)], axis=0).astype(BF16)
    s_new = lax.dot_general(q8, kn, nt, preferred_element_type=F32)
    tail_col = lax.broadcasted_iota(jnp.int32, (nr, LANES), 1)
    s_ref[:, past:past + LANES] = jnp.where(own(LANES) & (tail_col < heads),
                                            s_new + bias_ref[:, rows:rows + LANES], NEG_INF)

    s = s_ref[...]
    m = jnp.max(s, axis=-1, keepdims=True)
    p = jnp.exp(s - m)
    l = jnp.sum(p, axis=-1, keepdims=True)
    pb = p.astype(BF16)
    out = jnp.zeros((nr, hd), F32)
    for j in range(n_pages):
        out = out + jnp.dot(pb[:, j * rows:(j + 1) * rows], v_refs[j][0].astype(BF16), preferred_element_type=F32)
    vn = jnp.concatenate([vn_ref[0], jnp.zeros((LANES - heads, hd), F32)], axis=0).astype(BF16)
    out = (out + jnp.dot(pb[:, past:past + LANES], vn, preferred_element_type=F32)) / l
    lam = _lambda(lq1, lk1, lq2, lk2, lambda_init)
    o = out[0:heads, :] - lam * out[heads:nr, :]
    o = o * lax.rsqrt(jnp.mean(o * o, axis=-1, keepdims=True) + LN_EPS)
    o_ref[0] = (o * g_ref[...] * (1.0 - lambda_init)).astype(o_ref.dtype)


def _attn_sample(page_table, rel_bias, q, k_new, v_new, cache_k, cache_v, subln_g, lams, *, heads, lambda_init):
    nb, n_pages = page_table.shape
    n_pool, rows, hd = cache_k.shape
    page = rows // heads
    assert heads & (heads - 1) == 0 and 2 * heads <= 8
    tok = lambda a: a.reshape(nb, heads, hd)
    tok_spec = pl.BlockSpec((1, heads, hd), lambda b, pt: (b, 0, 0))
    lam_spec = pl.BlockSpec((1, ATT_HEAD_DIM), lambda b, pt: (0, 0))
    page_specs = [pl.BlockSpec((1, rows, hd), lambda b, pt, j=j: (pt[b, j], 0, 0)) for j in range(n_pages)]
    grid_spec = pltpu.PrefetchScalarGridSpec(
        num_scalar_prefetch=1,
        grid=(nb,),
        in_specs=[pl.BlockSpec(memory_space=pltpu.SMEM), tok_spec, tok_spec, tok_spec,
                  pl.BlockSpec((1, hd), lambda b, pt: (0, 0)),
                  lam_spec, lam_spec, lam_spec, lam_spec] + page_specs + page_specs,
        out_specs=tok_spec,
        scratch_shapes=[pltpu.VMEM((2 * heads, n_pages * rows + LANES), F32),
                        pltpu.VMEM((2 * heads, rows + LANES), F32)],
    )
    out = pl.pallas_call(
        functools.partial(_decode_kernel, n_pages=n_pages, page=page, heads=heads, lambda_init=lambda_init),
        grid_spec=grid_spec,
        out_shape=jax.ShapeDtypeStruct((nb, heads, hd), BF16),
        compiler_params=_params("arbitrary"),
        name="attn_sample",
    )(page_table, rel_bias, tok(q), tok(k_new), tok(v_new), subln_g.reshape(1, hd), *lams,
      *([cache_k] * n_pages), *([cache_v] * n_pages))
    return out.reshape(nb, heads * hd)


def _mix_kernel(x_ref, cy_ref, ay_ref, mod_ref, wo_ref, bo_ref, g1_ref, b1_ref, wq_ref, sk_ref,
                x1_ref, h2t_ref, st_ref, *, per_token_mod, alpha, d_conv):
    if per_token_mod:
        gate1, sh2, sc2 = mod_ref[2], mod_ref[3], mod_ref[4]
    else:
        gate1, sh2, sc2 = mod_ref[2, 0], mod_ref[3, 0], mod_ref[4, 0]
    mix = (jnp.dot(cy_ref[...], wo_ref[:d_conv, :], preferred_element_type=F32)
           + jnp.dot(ay_ref[...], wo_ref[d_conv:, :], preferred_element_type=F32) + bo_ref[...])
    x1 = _ln(alpha * x_ref[...] + gate1 * mix) * g1_ref[...] + b1_ref[...]
    x1_ref[...] = x1
    h2 = _ln(x1) * (1.0 + sc2) + sh2
    h2b = h2.astype(BF16)
    h2t_ref[...] = h2.T.astype(BF16)
    qh = jnp.dot(h2b, wq_ref[...], preferred_element_type=F32).astype(BF16)
    nk = sk_ref.shape[2]
    for hc in range(sk_ref.shape[0]):
        st_ref[hc] = lax.dot_general(sk_ref[hc], qh[:, hc * nk:(hc + 1) * nk], (((1,), (1,)), ((), ())),
                                     preferred_element_type=F32)


def _mix(x, cy, ay, mod, w_out, b_out, ln_g, ln_b, w_query, sub_keys, *, tm, rows_per_mod, alpha):
    t, d = x.shape
    d_conv = cy.shape[1]
    d_attn = ay.shape[1]
    n_hc, n_keys, half = sub_keys.shape
    per_token = rows_per_mod is None
    if per_token:
        mod_spec = pl.BlockSpec((6, tm, d), lambda i: (0, i, 0))
    else:
        mod_spec = pl.BlockSpec((6, 1, 1, d), lambda i: (0, (i * tm) // rows_per_mod, 0, 0))
    vec = lambda a: a.reshape(1, d)
    return pl.pallas_call(
        functools.partial(_mix_kernel, per_token_mod=per_token, alpha=alpha, d_conv=d_conv),
        grid=(t // tm,),
        in_specs=[pl.BlockSpec((tm, d), lambda i: (i, 0)),
                  pl.BlockSpec((tm, d_conv), lambda i: (i, 0)),
                  pl.BlockSpec((tm, d_attn), lambda i: (i, 0)),
                  mod_spec, _full(w_out.shape), _full((1, d)), _full((1, d)), _full((1, d)),
                  _full(w_query.shape), _full(sub_keys.shape)],
        out_specs=[pl.BlockSpec((tm, d), lambda i: (i, 0)),
                   pl.BlockSpec((d, tm), lambda i: (0, i)),
                   pl.BlockSpec((n_hc, n_keys, tm), lambda i: (0, 0, i))],
        out_shape=[jax.ShapeDtypeStruct((t, d), F32),
                   jax.ShapeDtypeStruct((d, t), BF16),
                   jax.ShapeDtypeStruct((n_hc, n_keys, t), F32)],
        compiler_params=_params("arbitrary"),
        name="mix_sample" if per_token else "mix_prompt",
    )(x, cy, ay, mod, w_out, vec(b_out), vec(ln_g), vec(ln_b), w_query, sub_keys)


def _top_ranks(s):
    work = s
    rank = jnp.full(s.shape, float(PEER_TOPK), F32)
    vals = []
    for r in range(PEER_TOPK):
        m = jnp.max(work, axis=0, keepdims=True)
        hit = work == m
        rank = jnp.where(hit, float(r), rank)
        work = jnp.where(hit, -jnp.inf, work)
        vals.append(m)
    return rank, vals


def _route_kernel(s_ref, r2_ref, e2_ref, n1_ref, e1_ref):
    k = PEER_TOPK
    s1 = s_ref[0]
    s2 = s_ref[1]
    rank1, v1 = _top_ranks(s1)
    rank2, v2 = _top_ranks(s2)
    rowk = lax.broadcasted_iota(jnp.int32, (k,) + s1.shape[1:], 0)

    def stack(vals):
        out = jnp.zeros(rowk.shape, F32)
        for r in range(k):
            out = jnp.where(rowk == r, vals[r], out)
        return out

    v1m = stack(v1)
    v2m = stack(v2)
    row8 = lax.broadcasted_iota(jnp.int32, (8,) + s1.shape[1:], 0)
    cands = [v1[0] + v2m]
    for a in range(1, 8):
        cands.append(jnp.where(row8 < k // (a + 1), v1[a] + v2m[0:8], -jnp.inf))
    cands.append(v1m[8:k] + v2[0])
    work = jnp.concatenate(cands, axis=0)
    thr = None
    for _ in range(k):
        thr = jnp.max(work, axis=0, keepdims=True)
        work = jnp.where(work == thr, -jnp.inf, work)
    e2top = jnp.exp(v2m - v2[0])
    z = jnp.zeros_like(thr)
    n1 = jnp.zeros(s1.shape, F32)
    for a in range(k):
        sel = (v1[a] + v2m) >= thr
        cnt = jnp.sum(jnp.where(sel, 1.0, 0.0), axis=0, keepdims=True)
        z = z + jnp.exp(v1[a] - v1[0]) * jnp.sum(jnp.where(sel, e2top, 0.0), axis=0, keepdims=True)
        n1 = jnp.where(rank1 == float(a), cnt, n1)
    e1 = jnp.where(rank1 < float(k), jnp.exp(s1 - v1[0]) / z, 0.0)
    e2 = jnp.where(rank2 < float(k), jnp.exp(s2 - v2[0]), 0.0)
    r2_ref[0] = rank2.astype(r2_ref.dtype)
    e2_ref[0] = e2.astype(e2_ref.dtype)
    n1_ref[0] = _bf16_pair(n1)
    e1_ref[0] = _bf16_pair(e1)


def _bf16_pair(x):
    bits = pltpu.bitcast(x.astype(BF16).astype(F32), jnp.uint32)
    return bits | lax.shift_right_logical(bits, jnp.uint32(16))


def _route(st, *, tl):
    n_hc, n_keys, t = st.shape
    heads = n_hc // 2
    out = lambda dt: jax.ShapeDtypeStruct((heads, n_keys, t), dt)
    spec = pl.BlockSpec((1, n_keys, tl), lambda i, h: (h, 0, i))
    return pl.pallas_call(
        _route_kernel,
        grid=(t // tl, heads),
        in_specs=[pl.BlockSpec((2, n_keys, tl), lambda i, h: (h, 0, i))],
        out_specs=[spec] * 4,
        out_shape=[out(BF16), out(BF16), out(jnp.uint32), out(jnp.uint32)],
        compiler_params=_params("arbitrary", "arbitrary"),
        name="route",
    )(st)


GELU_C1 = math.sqrt(2.0 / math.pi)
GELU_C2 = GELU_C1 * 0.044715


def _gelu_tanh(x):
    hx = 0.5 * x
    return hx + hx * jnp.tanh(x * (GELU_C1 + GELU_C2 * (x * x)))


def _packed_rows(words):
    return pltpu.bitcast(jnp.broadcast_to(words, (8, words.shape[1])), BF16)


def _peer_kernel(h2t_ref, u_ref, vt_ref, r2_ref, e2_ref, n1_ref, e1_ref, x1_ref, mod_ref, g_ref, b_ref, y_ref,
                 at0_ref, at1_ref, zt0_ref, zt1_ref, acc_ref, h2s_ref, *, per_token_mod, alpha, n_keys, slab, n_blocks):
    t = pl.program_id(1)
    ek, tq = at0_ref.shape
    heads = r2_ref.shape[0]
    sub = 16
    col = min(256, tq)
    groups = slab // n_keys
    n_slabs = ek // slab

    def hidden(s, at_ref):
        rows = pl.ds(pl.multiple_of(s * slab, slab), slab)
        u = pltpu.bitcast(u_ref[pl.ds(pl.multiple_of(s * (slab // 2), slab // 2), slab // 2), :], BF16)
        at_ref[rows, :] = jnp.dot(u, h2s_ref[...], preferred_element_type=F32)

    def mask(s, at_ref, zt_ref):
        base = pl.multiple_of(s * slab, slab)
        for g in range(groups):
            i1 = (t - 1) * (ek // n_keys) + s * groups + g
            n1w = [n1_ref[h, pl.ds(i1, 1), :] for h in range(heads)]
            e1w = [e1_ref[h, pl.ds(i1, 1), :] for h in range(heads)]
            for c in range(tq // col):
                tok = slice(c * col, (c + 1) * col)
                n1b = [_packed_rows(n1w[h][:, tok]) for h in range(heads)]
                e1b = [_packed_rows(e1w[h][:, tok]) for h in range(heads)]
                for k in range(n_keys // sub):
                    rows = pl.ds(base + (g * n_keys + k * sub), sub)
                    keys = slice(k * sub, (k + 1) * sub)
                    act = _gelu_tanh(at_ref[rows, tok].astype(BF16))
                    w = jnp.zeros((sub, col), BF16)
                    for h in range(heads):
                        w = jnp.where(r2_ref[h, keys, tok] < n1b[h], w + e2_ref[h, keys, tok] * e1b[h], w)
                    zt_ref[rows, tok] = w * act

    def project(piece, zt_ref):
        rows = pl.ds(pl.multiple_of(piece * slab, slab), slab)
        vt = pltpu.bitcast(vt_ref[pl.ds(pl.multiple_of(piece * (slab // 2), slab // 2), slab // 2), :], BF16)
        acc_ref[rows, :] += jnp.dot(vt, zt_ref[...], preferred_element_type=F32)

    n_pieces = acc_ref.shape[0] // slab
    per_iter = n_pieces // n_slabs

    def full_step(at_cur, at_next, zt_cur, zt_prev):
        for s in range(n_slabs):
            hidden(s, at_next)
            mask(s, at_cur, zt_cur)
            for i in range(per_iter):
                project(s * per_iter + i, zt_prev)

    @pl.when(t == 0)
    def _():
        acc_ref[...] = jnp.zeros(acc_ref.shape, F32)
        zt1_ref[...] = jnp.zeros(zt1_ref.shape, BF16)
        h2s_ref[...] = h2t_ref[...]

        def body(s, carry):
            hidden(s, at0_ref)
            return carry

        lax.fori_loop(0, n_slabs, body, 0)

    @pl.when((t >= 1) & (t <= n_blocks) & (t % 2 == 1))
    def _():
        full_step(at0_ref, at1_ref, zt0_ref, zt1_ref)

    @pl.when((t >= 1) & (t <= n_blocks) & (t % 2 == 0))
    def _():
        full_step(at1_ref, at0_ref, zt1_ref, zt0_ref)

    @pl.when(t == n_blocks + 1)
    def _():
        last = zt0_ref if (n_blocks - 1) % 2 == 0 else zt1_ref

        def body(p, carry):
            project(p, last)
            return carry

        lax.fori_loop(0, n_pieces, body, 0)
        gate2 = mod_ref[5] if per_token_mod else mod_ref[5, 0]
        ff = acc_ref[...].T
        y_ref[...] = _ln(alpha * x1_ref[...] + gate2 * ff) * g_ref[...] + b_ref[...]


def _peer(h2t, u_b, vt_b, r2, e2, n1, e1, x1, mod, ln_g, ln_b, *, tq, ek, slab, rows_per_mod, alpha):
    d, t = h2t.shape
    ne = 2 * u_b.shape[0]
    heads, n_keys, _ = r2.shape
    n_blocks = ne // ek
    per_token = rows_per_mod is None
    if per_token:
        mod_spec = pl.BlockSpec((6, tq, d), lambda i, j: (0, i, 0))
    else:
        mod_spec = pl.BlockSpec((6, 1, 1, d), lambda i, j: (0, (i * tq) // rows_per_mod, 0, 0))
    tab = pl.BlockSpec((heads, n_keys, tq), lambda i, j: (0, 0, i))
    vec = lambda a: a.reshape(1, d)
    clamp = lambda b: jnp.clip(b, 0, n_blocks - 1)
    return pl.pallas_call(
        functools.partial(_peer_kernel, per_token_mod=per_token, alpha=alpha, n_keys=n_keys, slab=slab,
                          n_blocks=n_blocks),
        grid=(t // tq, n_blocks + 2),
        in_specs=[pl.BlockSpec((d, tq), lambda i, j: (0, i)),
                  pl.BlockSpec((ek // 2, d), lambda i, j: (clamp(j), 0)),
                  pl.BlockSpec((d // 2, ek), lambda i, j: (0, clamp(j - 2))),
                  tab, tab, tab, tab,
                  pl.BlockSpec((tq, d), lambda i, j: (i, 0)),
                  mod_spec, _full((1, d)), _full((1, d))],
        out_specs=pl.BlockSpec((tq, d), lambda i, j: (i, 0)),
        out_shape=jax.ShapeDtypeStruct((t, d), F32),
        scratch_shapes=[pltpu.VMEM((ek, tq), F32), pltpu.VMEM((ek, tq), F32),
                        pltpu.VMEM((ek, tq), BF16), pltpu.VMEM((ek, tq), BF16), pltpu.VMEM((d, tq), F32),
                        pltpu.VMEM((d, tq), BF16)],
        compiler_params=_params("arbitrary", "arbitrary"),
        name="peer_sample" if per_token else "peer_prompt",
    )(h2t, u_b, vt_b, r2, e2, n1, e1, x1, mod, vec(ln_g), vec(ln_b))


def _pack_kernel(x_ref, o_ref, *, transpose):
    x = x_ref[...]
    if transpose:
        x = x.T
    o_ref[...] = pltpu.bitcast(x.astype(BF16), jnp.uint32)


def _pack_bf16(x, *, rows, transpose):
    n, d = x.shape
    if transpose:
        out_spec, out_shape = pl.BlockSpec((d // 2, rows), lambda i: (0, i)), (d // 2, n)
    else:
        out_spec, out_shape = pl.BlockSpec((rows // 2, d), lambda i: (i, 0)), (n // 2, d)
    return pl.pallas_call(
        functools.partial(_pack_kernel, transpose=transpose),
        grid=(n // rows,),
        in_specs=[pl.BlockSpec((rows, d), lambda i: (i, 0))],
        out_specs=out_spec,
        out_shape=jax.ShapeDtypeStruct(out_shape, jnp.uint32),
        compiler_params=_params("arbitrary"),
        name="pack_vt" if transpose else "pack_u",
    )(x)


def kernel(x_prompt, x_sample, cache_k, cache_v, state_conv, page_table, c_prompt, c_sample, w_ada, b_ada, w_in, b_in, conv_w, conv_b, conv_ln_g, conv_ln_b, lambda_q1, lambda_k1, lambda_q2, lambda_k2, attn_subln_g, rel_bias, w_out, b_out, ln1_g, ln1_b, peer_w_query, peer_sub_keys, peer_u, peer_v, ln2_g, ln2_b):
    batch, seq, d = x_prompt.shape
    nb, dec_seq, _ = x_sample.shape
    depth = w_ada.shape[0]
    assert depth == 1 and dec_seq == 1
    heads = cache_k.shape[3]
    hd = cache_k.shape[4]
    assert hd == 2 * ATT_HEAD_DIM
    d_attn = heads * hd
    d_conv = conv_w.shape[2]
    hist = conv_w.shape[1] - 1
    assert hist <= CONV_PAD - 1
    page = cache_k.shape[2]
    n_keys = peer_sub_keys.shape[3]
    ne = peer_u.shape[1]
    alpha = (2 * depth) ** 0.25
    lambda_init = 0.8 - 0.6 * math.exp(-0.3 * 0)
    t_p = batch * seq
    ts = min(TOKEN_TILE, seq)
    tq_peer = min(PEER_TOKEN_TILE, seq)
    ek = min(PEER_EXPERT_TILE, ne)
    assert seq % ts == 0 and nb % LANES == 0 and page >= MAX_DISTANCE

    l = 0
    n_c = batch + nb
    n_c_pad = -(-n_c // 8) * 8
    c_all = jnp.concatenate([c_prompt, c_sample, jnp.zeros((n_c_pad - n_c, d), F32)], axis=0)
    mod = _ada(c_all, w_ada[l], b_ada[l])
    mod_p = mod[:, :batch].reshape(6, batch, 1, d)
    mod_s = mod[:, batch:n_c]

    w_in_b = w_in[l].astype(BF16)
    b_in_r = b_in[l].reshape(1, -1)
    w_out_b = w_out[l].astype(BF16)
    wq_b = peer_w_query[l].astype(BF16)
    sk_b = peer_sub_keys[l].reshape(-1, n_keys, peer_sub_keys.shape[-1]).astype(BF16)
    u_b = _pack_bf16(peer_u[l], rows=min(512, ne), transpose=False)
    vt_b = _pack_bf16(peer_v[l], rows=min(512, ne), transpose=True)
    lams = [a[l].reshape(1, -1) for a in (lambda_q1, lambda_k1, lambda_q2, lambda_k2)]
    subln_g = attn_subln_g[l]

    xp = x_prompt.reshape(t_p, d)
    glu_p, k_p, v_p, kb_p, qt_p, vt_p = _inproj(xp, mod_p, w_in_b, b_in_r, tm=ts, rows_per_mod=seq,
                                                d_conv=d_conv, d_attn=d_attn)
    cy_p = _conv_prompt(glu_p, conv_w[l], conv_b[l], conv_ln_g[l], conv_ln_b[l], batch=batch, seq=seq, ts=ts)
    bias = _bias_tiles(rel_bias, heads=heads, ts=ts)
    ay_p = _attn_prompt(qt_p, kb_p, vt_p, bias, subln_g, lams, batch=batch, seq=seq, heads=heads, ts=ts,
                        lambda_init=lambda_init)
    x1_p, h2t_p, st_p = _mix(xp, cy_p, ay_p, mod_p, w_out_b, b_out[l], ln1_g[l], ln1_b[l], wq_b, sk_b,
                             tm=ts, rows_per_mod=seq, alpha=alpha)
    r2, e2, n1, e1 = _route(st_p, tl=min(ROUTE_TOKEN_TILE, t_p))
    y_p = _peer(h2t_p, u_b, vt_b, r2, e2, n1, e1, x1_p, mod_p, ln2_g[l], ln2_b[l],
                tq=tq_peer, ek=ek, slab=min(PEER_EXPERT_SLAB, ek), rows_per_mod=seq, alpha=alpha)

    xs = x_sample.reshape(nb, d)
    glu_s, k_s, v_s, q_s = _inproj(xs, mod_s, w_in_b, b_in_r, tm=nb, rows_per_mod=None,
                                   d_conv=d_conv, d_attn=d_attn)
    state = state_conv[l]
    cy_s = _conv_sample(jnp.swapaxes(state, 0, 1), glu_s, conv_w[l], conv_b[l], conv_ln_g[l], conv_ln_b[l])
    n_pool = cache_k.shape[1]
    pool_rows = lambda a: a.reshape(depth * n_pool, page * heads, hd)
    ay_s = _attn_sample(page_table + l * n_pool, rel_bias, q_s, k_s, v_s, pool_rows(cache_k), pool_rows(cache_v),
                        subln_g, lams, heads=heads, lambda_init=lambda_init)
    x1_s, h2t_s, st_s = _mix(xs, cy_s, ay_s, mod_s, w_out_b, b_out[l], ln1_g[l], ln1_b[l], wq_b, sk_b,
                             tm=nb, rows_per_mod=None, alpha=alpha)
    r2s, e2s, n1s, e1s = _route(st_s, tl=nb)
    y_s = _peer(h2t_s, u_b, vt_b, r2s, e2s, n1s, e1s, x1_s, mod_s, ln2_g[l], ln2_b[l],
                tq=nb, ek=ek, slab=min(PEER_EXPERT_SLAB, ek), rows_per_mod=None, alpha=alpha)

    kv_p = (depth, batch, seq, heads, hd)
    kv_s = (depth, nb, dec_seq, heads, hd)
    conv_p = glu_p.reshape(batch, seq, d_conv)[:, seq - hist:][None]
    conv_s = jnp.concatenate([state[:, 1:], glu_s[:, None, :]], axis=1)[None]
    return (y_p.reshape(batch, seq, d), y_s.reshape(nb, dec_seq, d),
            k_p.reshape(kv_p), v_p.reshape(kv_p), conv_p,
            k_s.reshape(kv_s), v_s.reshape(kv_s), conv_s)
```

```python
import functools
import math

import jax
import jax.numpy as jnp
from jax import lax
from jax.experimental import pallas as pl
from jax.experimental.pallas import tpu as pltpu

F32 = jnp.float32
BF16 = jnp.bfloat16

LN_EPS = 1e-5
NEG_INF = -1e30
ATT_HEAD_DIM = 64
N_BUCKETS = 32
MAX_DISTANCE = 128
PEER_TOPK = 16
LOG2E = math.log2(math.e)
LANES = 128
VMEM_LIMIT_BYTES = 56 * 1024 * 1024
TOKEN_TILE = 512
PEER_TOKEN_TILE = 512
PEER_EXPERT_TILE = 1024
PEER_EXPERT_SLAB = 256
ROUTE_TOKEN_TILE = 256


def _params(*sem):
    return pltpu.CompilerParams(dimension_semantics=sem, vmem_limit_bytes=VMEM_LIMIT_BYTES)


def _ln(x):
    mu = jnp.mean(x, axis=-1, keepdims=True)
    xc = x - mu
    var = jnp.mean(xc * xc, axis=-1, keepdims=True)
    return xc * lax.rsqrt(var + LN_EPS)


def _full(shape):
    return pl.BlockSpec(shape, lambda *_: (0,) * len(shape))


def _ada_kernel(c_ref, w_ref, b_ref, o_ref):
    c = c_ref[...]
    s = c * jax.nn.sigmoid(c)
    o_ref[0] = jnp.dot(s.astype(BF16), w_ref[...].astype(BF16), preferred_element_type=F32) + b_ref[0]


def _ada(c_all, w_ada, b_ada):
    nc, d = c_all.shape
    return pl.pallas_call(
        _ada_kernel,
        grid=(6,),
        in_specs=[_full((nc, d)),
                  pl.BlockSpec((d, d), lambda k: (0, k)),
                  pl.BlockSpec((1, 1, d), lambda k: (k, 0, 0))],
        out_specs=pl.BlockSpec((1, nc, d), lambda k: (k, 0, 0)),
        out_shape=jax.ShapeDtypeStruct((6, nc, d), F32),
        compiler_params=_params("arbitrary"),
        name="ada",
    )(c_all, w_ada, b_ada.reshape(6, 1, d))


def _inproj_kernel(x_ref, mod_ref, w_ref, b_ref, *outs, per_token_mod, d_conv, d_attn, transposed):
    x = x_ref[...]
    if per_token_mod:
        sh1, sc1 = mod_ref[0], mod_ref[1]
    else:
        sh1, sc1 = mod_ref[0, 0], mod_ref[1, 0]
    h = _ln(x) * (1.0 + sc1) + sh1
    z = jnp.dot(h.astype(BF16), w_ref[...], preferred_element_type=F32) + b_ref[...]
    ga = z[:, :d_conv]
    gb = z[:, d_conv:2 * d_conv]
    o = 2 * d_conv
    q = z[:, o:o + d_attn]
    k = z[:, o + d_attn:o + 2 * d_attn]
    v = z[:, o + 2 * d_attn:o + 3 * d_attn]
    glu = ga * jax.nn.sigmoid(gb)
    if transposed:
        glu_ref, k_ref, v_ref, kb_ref, qt_ref, vt_ref = outs
        kb_ref[...] = k.astype(BF16)
        qt_ref[0] = (q * (ATT_HEAD_DIM ** -0.5 * LOG2E)).T.astype(BF16)
        vt_ref[0] = v.T.astype(BF16)
    else:
        glu_ref, k_ref, v_ref, q_ref = outs
        q_ref[...] = q
    glu_ref[...] = glu
    heads = d_attn // (2 * ATT_HEAD_DIM)
    for h in range(heads):
        cols = slice(h * 2 * ATT_HEAD_DIM, (h + 1) * 2 * ATT_HEAD_DIM)
        k_ref[pl.ds(h, x.shape[0], stride=heads), :] = k[:, cols]
        v_ref[pl.ds(h, x.shape[0], stride=heads), :] = v[:, cols]


def _inproj(x, mod, w_in, b_in, *, tm, rows_per_mod, d_conv, d_attn):
    t, d = x.shape
    n_in = w_in.shape[1]
    nt = t // tm
    per_token = rows_per_mod is None
    if per_token:
        mod_spec = pl.BlockSpec((6, tm, d), lambda i: (0, i, 0))
    else:
        mod_spec = pl.BlockSpec((6, 1, 1, d), lambda i: (0, (i * tm) // rows_per_mod, 0, 0))
    row = lambda n, dt: (jax.ShapeDtypeStruct((t, n), dt), pl.BlockSpec((tm, n), lambda i: (i, 0)))
    hd = 2 * ATT_HEAD_DIM
    heads = d_attn // hd
    head_rows = (jax.ShapeDtypeStruct((t * heads, hd), F32), pl.BlockSpec((tm * heads, hd), lambda i: (i, 0)))
    outs = [row(d_conv, F32), head_rows, head_rows]
    if per_token:
        outs.append(row(d_attn, F32))
    else:
        outs.append(row(d_attn, BF16))
        tr = (jax.ShapeDtypeStruct((nt, d_attn, tm), BF16), pl.BlockSpec((1, d_attn, tm), lambda i: (i, 0, 0)))
        outs += [tr, tr]
    return pl.pallas_call(
        functools.partial(_inproj_kernel, per_token_mod=per_token, d_conv=d_conv, d_attn=d_attn,
                          transposed=not per_token),
        grid=(nt,),
        in_specs=[pl.BlockSpec((tm, d), lambda i: (i, 0)), mod_spec, _full((d, n_in)), _full((1, n_in))],
        out_specs=[o[1] for o in outs],
        out_shape=[o[0] for o in outs],
        compiler_params=_params("arbitrary"),
        name="inproj_sample" if per_token else "inproj_prompt",
    )(x, mod, w_in, b_in)


CONV_PAD = 32


def _conv_post(acc, cb_ref, g_ref, b_ref):
    y = _ln(acc + cb_ref[...]) * g_ref[...] + b_ref[...]
    return y * jax.nn.sigmoid(y)


def _conv_prompt_kernel(glu_ref, cw_ref, cb_ref, g_ref, b_ref, y_ref, full_ref, sh_ref, *, ts, width, chunk):
    s = pl.program_id(1)
    hist = width - 1

    @pl.when(s == 0)
    def _():
        full_ref[0:CONV_PAD, :] = jnp.zeros((CONV_PAD, full_ref.shape[1]), F32)

    @pl.when(s > 0)
    def _():
        full_ref[0:CONV_PAD, :] = full_ref[ts:ts + CONV_PAD, :]

    full_ref[CONV_PAD:CONV_PAD + ts, :] = glu_ref[...]
    n_sh = sh_ref.shape[1]
    for r in range(1, 8):
        sh_ref[r - 1] = full_ref[r:r + n_sh, :]
    base = CONV_PAD - hist
    for c in range(ts // chunk):
        r0 = c * chunk
        acc = None
        for w in range(width):
            q, r = divmod(base + w, 8)
            rows = slice(r0 + 8 * q, r0 + 8 * q + chunk)
            tap = (full_ref[rows, :] if r == 0 else sh_ref[r - 1, rows, :]) * cw_ref[w:w + 1, :]
            acc = tap if acc is None else acc + tap
        y_ref[r0:r0 + chunk, :] = _conv_post(acc, cb_ref, g_ref, b_ref).astype(y_ref.dtype)


def _conv_prompt(glu, conv_w, conv_b, ln_g, ln_b, *, batch, seq, ts):
    t, dc = glu.shape
    width = conv_w.shape[0]
    ns = seq // ts
    vec = lambda a: a.reshape(1, dc)
    return pl.pallas_call(
        functools.partial(_conv_prompt_kernel, ts=ts, width=width, chunk=min(64, ts)),
        grid=(batch, ns),
        in_specs=[pl.BlockSpec((ts, dc), lambda b, s: (b * ns + s, 0)),
                  _full((width, dc)), _full((1, dc)), _full((1, dc)), _full((1, dc))],
        out_specs=pl.BlockSpec((ts, dc), lambda b, s: (b * ns + s, 0)),
        out_shape=jax.ShapeDtypeStruct((t, dc), BF16),
        scratch_shapes=[pltpu.VMEM((CONV_PAD + ts, dc), F32), pltpu.VMEM((7, CONV_PAD + ts - 8, dc), F32)],
        compiler_params=_params("arbitrary", "arbitrary"),
        name="conv_prompt",
    )(glu, conv_w, vec(conv_b), vec(ln_g), vec(ln_b))


def _conv_sample_kernel(st_ref, glu_ref, cw_ref, cb_ref, g_ref, b_ref, y_ref, *, width):
    hist = width - 1
    acc = glu_ref[...] * cw_ref[hist:hist + 1, :]
    for w in range(hist):
        acc = acc + st_ref[w] * cw_ref[w:w + 1, :]
    y_ref[...] = _conv_post(acc, cb_ref, g_ref, b_ref).astype(y_ref.dtype)


def _conv_sample(state_t, glu, conv_w, conv_b, ln_g, ln_b):
    hist, nb, dc = state_t.shape
    width = conv_w.shape[0]
    vec = lambda a: a.reshape(1, dc)
    return pl.pallas_call(
        functools.partial(_conv_sample_kernel, width=width),
        grid=(1,),
        in_specs=[_full((hist, nb, dc)), _full((nb, dc)), _full((width, dc)),
                  _full((1, dc)), _full((1, dc)), _full((1, dc))],
        out_specs=_full((nb, dc)),
        out_shape=jax.ShapeDtypeStruct((nb, dc), BF16),
        compiler_params=_params("arbitrary"),
        name="conv_sample",
    )(state_t, glu, conv_w, vec(conv_b), vec(ln_g), vec(ln_b))


def _bucket(n):
    max_exact = N_BUCKETS // 2
    nf = jnp.maximum(n, 1).astype(F32)
    large = max_exact + (jnp.log(nf / max_exact) / math.log(MAX_DISTANCE / max_exact)
                         * (N_BUCKETS - max_exact)).astype(jnp.int32)
    large = jnp.minimum(large, N_BUCKETS - 1)
    return jnp.where(n < max_exact, n, large)


def _bias_of(n, rb_ref, h):
    bucket = _bucket(n)
    far = jnp.full(n.shape, rb_ref[N_BUCKETS - 1, h], F32)
    out = far
    for j in range(N_BUCKETS - 1):
        out = jnp.where(bucket == j, rb_ref[j, h], out)
    return out - far


def _bias_kernel(rb_ref, o_ref, *, ts):
    h = pl.program_id(0)
    d = pl.program_id(1)
    ik = lax.broadcasted_iota(jnp.int32, (ts, ts), 0)
    jq = lax.broadcasted_iota(jnp.int32, (ts, ts), 1)
    n = jq - ik + ts * (1 - d)
    b = _bias_of(jnp.maximum(n, 0), rb_ref, h)
    o_ref[0, 0] = jnp.where(n >= 0, b * LOG2E, NEG_INF)


def _bias_tiles(rel_bias, *, heads, ts):
    return pl.pallas_call(
        functools.partial(_bias_kernel, ts=ts),
        grid=(heads, 2),
        in_specs=[pl.BlockSpec(memory_space=pltpu.SMEM)],
        out_specs=pl.BlockSpec((1, 1, ts, ts), lambda h, d: (h, d, 0, 0)),
        out_shape=jax.ShapeDtypeStruct((heads, 2, ts, ts), F32),
        compiler_params=_params("arbitrary", "arbitrary"),
        name="bias_tiles",
    )(rel_bias)


def _lambda(lq1, lk1, lq2, lk2, lambda_init):
    s1 = jnp.sum(lq1[...] * lk1[...], axis=-1, keepdims=True)
    s2 = jnp.sum(lq2[...] * lk2[...], axis=-1, keepdims=True)
    return jnp.exp(s1) - jnp.exp(s2) + lambda_init


def _attn_kernel(qt_ref, k_ref, vt_ref, bias_ref, g_ref, lq1, lk1, lq2, lk2, o_ref,
                 qs_ref, m_ref, l_ref, acc_ref, sa_ref, sb_ref, *, ts, lambda_init):
    qi = pl.program_id(2)
    d = ATT_HEAD_DIM
    qt = qt_ref[0]
    row = lax.broadcasted_iota(jnp.int32, qt.shape, 0)
    zero = jnp.zeros_like(qt)
    qs_ref[:, :ts] = jnp.where(row < d, qt, zero)
    qs_ref[:, ts:] = jnp.where(row >= d, qt, zero)
    m_ref[...] = jnp.full(m_ref.shape, NEG_INF, F32)
    l_ref[...] = jnp.zeros(l_ref.shape, F32)
    acc_ref[...] = jnp.zeros(acc_ref.shape, F32)

    def scores(ki, bias):
        kblk = k_ref[pl.ds(pl.multiple_of(ki * ts, ts), ts), :]
        s = jnp.dot(kblk, qs_ref[...], preferred_element_type=F32)
        if bias is not None:
            s = s + jnp.concatenate([bias, bias], axis=1)
        return s

    def update(s_ref, ki):
        s = s_ref[...]
        m_prev = m_ref[...]
        m_new = jnp.maximum(m_prev, jnp.max(s, axis=0, keepdims=True))
        alpha = jnp.exp2(m_prev - m_new)
        p = jnp.exp2(s - m_new)
        l_ref[...] = alpha * l_ref[...] + jnp.sum(p, axis=0, keepdims=True)
        acc_ref[...] = alpha * acc_ref[...] + jnp.dot(vt_ref[ki], p.astype(BF16), preferred_element_type=F32)
        m_ref[...] = m_new

    n_far = jnp.maximum(qi - 1, 0)
    sa_ref[...] = scores(qi, bias_ref[0, 1])

    @pl.when(qi > 0)
    def _():
        sb_ref[...] = scores(qi - 1, bias_ref[0, 0])
        update(sa_ref, qi)

    def far_pair(jj, carry):
        ka = qi - 2 - 2 * jj
        sa_ref[...] = scores(ka, None)
        update(sb_ref, ka + 1)
        sb_ref[...] = scores(ka - 1, None)
        update(sa_ref, ka)
        return carry

    lax.fori_loop(0, n_far // 2, far_pair, 0)

    @pl.when(n_far % 2 == 1)
    def _():
        sa_ref[...] = scores(0, None)
        update(sb_ref, 1)

    @pl.when(qi % 2 == 0)
    def _():
        update(sa_ref, 0)

    @pl.when(qi % 2 == 1)
    def _():
        update(sb_ref, 0)

    lam = _lambda(lq1, lk1, lq2, lk2, lambda_init)
    inv_l = 1.0 / l_ref[...]
    acc = acc_ref[...]
    o = acc[:, :ts] * inv_l[:, :ts] - lam * (acc[:, ts:] * inv_l[:, ts:])
    o = o * lax.rsqrt(jnp.mean(o * o, axis=0, keepdims=True) + LN_EPS)
    o = o * g_ref[...] * (1.0 - lambda_init)
    o_ref[...] = o.T.astype(o_ref.dtype)


def _attn_prompt(qt, kb, vt, bias, subln_g, lams, *, batch, seq, heads, ts, lambda_init):
    t, d_attn = kb.shape
    hd = 2 * ATT_HEAD_DIM
    nq = seq // ts
    lam_spec = _full((1, ATT_HEAD_DIM))
    return pl.pallas_call(
        functools.partial(_attn_kernel, ts=ts, lambda_init=lambda_init),
        grid=(batch, heads, nq),
        in_specs=[pl.BlockSpec((1, hd, ts), lambda b, h, q: (b * nq + q, h, 0)),
                  pl.BlockSpec((seq, hd), lambda b, h, q: (b, h)),
                  pl.BlockSpec((nq, hd, ts), lambda b, h, q: (b, h, 0)),
                  pl.BlockSpec((1, 2, ts, ts), lambda b, h, q: (h, 0, 0, 0)),
                  _full((hd, 1)), lam_spec, lam_spec, lam_spec, lam_spec],
        out_specs=pl.BlockSpec((ts, hd), lambda b, h, q: (b * nq + q, h)),
        out_shape=jax.ShapeDtypeStruct((t, d_attn), BF16),
        scratch_shapes=[pltpu.VMEM((hd, 2 * ts), BF16), pltpu.VMEM((1, 2 * ts), F32),
                        pltpu.VMEM((1, 2 * ts), F32), pltpu.VMEM((hd, 2 * ts), F32),
                        pltpu.VMEM((ts, 2 * ts), F32), pltpu.VMEM((ts, 2 * ts), F32)],
        compiler_params=_params("arbitrary", "arbitrary", "arbitrary"),
        name="attn_prompt",
    )(qt, kb, vt, bias, subln_g.reshape(hd, 1), *lams)


def _decode_kernel(pt_ref, rb_ref, q_ref, kn_ref, vn_ref, g_ref, lq1, lk1, lq2, lk2, *rest,
                   n_pages, page, heads, lambda_init):
    k_refs = rest[:n_pages]
    v_refs = rest[n_pages:2 * n_pages]
    o_ref = rest[2 * n_pages]
    s_ref, bias_ref = rest[2 * n_pages + 1:]
    d = ATT_HEAD_DIM
    hd = 2 * d
    rows = page * heads
    past = n_pages * rows
    nr = 2 * heads
    log_heads = int(math.log2(heads))
    nt = (((1,), (1,)), ((), ()))

    q4 = q_ref[0] * (d ** -0.5)
    lane = lax.broadcasted_iota(jnp.int32, (nr, hd), 1)
    top = lax.broadcasted_iota(jnp.int32, (nr, hd), 0) < heads
    q8 = jnp.concatenate([q4, q4], axis=0)
    q8 = jnp.where(top == (lane < d), q8, 0.0).astype(BF16)

    def own(n_cols):
        r = lax.broadcasted_iota(jnp.int32, (nr, n_cols), 0)
        c = lax.broadcasted_iota(jnp.int32, (nr, n_cols), 1)
        return (r & (heads - 1)) == (c & (heads - 1))

    @pl.when(pl.program_id(0) == 0)
    def _():
        key = lax.shift_right_logical(lax.broadcasted_iota(jnp.int32, (nr, rows), 1), log_heads)
        rh = lax.broadcasted_iota(jnp.int32, (nr, rows), 0) & (heads - 1)
        rh_tail = lax.broadcasted_iota(jnp.int32, (nr, LANES), 0) & (heads - 1)
        bias = jnp.zeros((nr, rows), F32)
        bias_new = jnp.zeros((nr, LANES), F32)
        for h in range(heads):
            bias = jnp.where(rh == h, _bias_of(page - key, rb_ref, h), bias)
            bias_new = jnp.where(rh_tail == h, _bias_of(jnp.zeros((nr, LANES), jnp.int32), rb_ref, h), bias_new)
        bias_ref[:, 0:rows] = bias
        bias_ref[:, rows:rows + LANES] = bias_new

    own_page = own(rows)
    for j in range(n_pages):
        sj = lax.dot_general(q8, k_refs[j][0].astype(BF16), nt, preferred_element_type=F32)
        if j == n_pages - 1:
            sj = sj + bias_ref[:, 0:rows]
        s_ref[:, j * rows:(j + 1) * rows] = jnp.where(own_page, sj, NEG_INF)
    kn = jnp.concatenate([kn_ref[0], jnp.zeros((LANES - heads, hd), F32)], axis=0).astype(BF16)
    s_new = lax.dot_general(q8, kn, nt, preferred_element_type=F32)
    tail_col = lax.broadcasted_iota(jnp.int32, (nr, LANES), 1)
    s_ref[:, past:past + LANES] = jnp.where(own(LANES) & (tail_col < heads),
                                            s_new + bias_ref[:, rows:rows + LANES], NEG_INF)

    s = s_ref[...]
    m = jnp.max(s, axis=-1, keepdims=True)
    p = jnp.exp(s - m)
    l = jnp.sum(p, axis=-1, keepdims=True)
    pb = p.astype(BF16)
    out = jnp.zeros((nr, hd), F32)
    for j in range(n_pages):
        out = out + jnp.dot(pb[:, j * rows:(j + 1) * rows], v_refs[j][0].astype(BF16), preferred_element_type=F32)
    vn = jnp.concatenate([vn_ref[0], jnp.zeros((LANES - heads, hd), F32)], axis=0).astype(BF16)
    out = (out + jnp.dot(pb[:, past:past + LANES], vn, preferred_element_type=F32)) / l
    lam = _lambda(lq1, lk1, lq2, lk2, lambda_init)
    o = out[0:heads, :] - lam * out[heads:nr, :]
    o = o * lax.rsqrt(jnp.mean(o * o, axis=-1, keepdims=True) + LN_EPS)
    o_ref[0] = (o * g_ref[...] * (1.0 - lambda_init)).astype(o_ref.dtype)


def _attn_sample(page_table, rel_bias, q, k_new, v_new, cache_k, cache_v, subln_g, lams, *, heads, lambda_init):
    nb, n_pages = page_table.shape
    n_pool, rows, hd = cache_k.shape
    page = rows // heads
    assert heads & (heads - 1) == 0 and 2 * heads <= 8
    tok = lambda a: a.reshape(nb, heads, hd)
    tok_spec = pl.BlockSpec((1, heads, hd), lambda b, pt: (b, 0, 0))
    lam_spec = pl.BlockSpec((1, ATT_HEAD_DIM), lambda b, pt: (0, 0))
    page_specs = [pl.BlockSpec((1, rows, hd), lambda b, pt, j=j: (pt[b, j], 0, 0)) for j in range(n_pages)]
    grid_spec = pltpu.PrefetchScalarGridSpec(
        num_scalar_prefetch=1,
        grid=(nb,),
        in_specs=[pl.BlockSpec(memory_space=pltpu.SMEM), tok_spec, tok_spec, tok_spec,
                  pl.BlockSpec((1, hd), lambda b, pt: (0, 0)),
                  lam_spec, lam_spec, lam_spec, lam_spec] + page_specs + page_specs,
        out_specs=tok_spec,
        scratch_shapes=[pltpu.VMEM((2 * heads, n_pages * rows + LANES), F32),
                        pltpu.VMEM((2 * heads, rows + LANES), F32)],
    )
    out = pl.pallas_call(
        functools.partial(_decode_kernel, n_pages=n_pages, page=page, heads=heads, lambda_init=lambda_init),
        grid_spec=grid_spec,
        out_shape=jax.ShapeDtypeStruct((nb, heads, hd), BF16),
        compiler_params=_params("arbitrary"),
        name="attn_sample",
    )(page_table, rel_bias, tok(q), tok(k_new), tok(v_new), subln_g.reshape(1, hd), *lams,
      *([cache_k] * n_pages), *([cache_v] * n_pages))
    return out.reshape(nb, heads * hd)


def _mix_kernel(x_ref, cy_ref, ay_ref, mod_ref, wo_ref, bo_ref, g1_ref, b1_ref, wq_ref, sk_ref,
                x1_ref, h2t_ref, st_ref, *, per_token_mod, alpha, d_conv):
    if per_token_mod:
        gate1, sh2, sc2 = mod_ref[2], mod_ref[3], mod_ref[4]
    else:
        gate1, sh2, sc2 = mod_ref[2, 0], mod_ref[3, 0], mod_ref[4, 0]
    mix = (jnp.dot(cy_ref[...], wo_ref[:d_conv, :], preferred_element_type=F32)
           + jnp.dot(ay_ref[...], wo_ref[d_conv:, :], preferred_element_type=F32) + bo_ref[...])
    x1 = _ln(alpha * x_ref[...] + gate1 * mix) * g1_ref[...] + b1_ref[...]
    x1_ref[...] = x1
    h2 = _ln(x1) * (1.0 + sc2) + sh2
    h2b = h2.astype(BF16)
    h2t_ref[...] = h2.T.astype(BF16)
    qh = jnp.dot(h2b, wq_ref[...], preferred_element_type=F32).astype(BF16)
    nk = sk_ref.shape[2]
    for hc in range(sk_ref.shape[0]):
        st_ref[hc] = lax.dot_general(sk_ref[hc], qh[:, hc * nk:(hc + 1) * nk], (((1,), (1,)), ((), ())),
                                     preferred_element_type=F32)


def _mix(x, cy, ay, mod, w_out, b_out, ln_g, ln_b, w_query, sub_keys, *, tm, rows_per_mod, alpha):
    t, d = x.shape
    d_conv = cy.shape[1]
    d_attn = ay.shape[1]
    n_hc, n_keys, half = sub_keys.shape
    per_token = rows_per_mod is None
    if per_token:
        mod_spec = pl.BlockSpec((6, tm, d), lambda i: (0, i, 0))
    else:
        mod_spec = pl.BlockSpec((6, 1, 1, d), lambda i: (0, (i * tm) // rows_per_mod, 0, 0))
    vec = lambda a: a.reshape(1, d)
    return pl.pallas_call(
        functools.partial(_mix_kernel, per_token_mod=per_token, alpha=alpha, d_conv=d_conv),
        grid=(t // tm,),
        in_specs=[pl.BlockSpec((tm, d), lambda i: (i, 0)),
                  pl.BlockSpec((tm, d_conv), lambda i: (i, 0)),
                  pl.BlockSpec((tm, d_attn), lambda i: (i, 0)),
                  mod_spec, _full(w_out.shape), _full((1, d)), _full((1, d)), _full((1, d)),
                  _full(w_query.shape), _full(sub_keys.shape)],
        out_specs=[pl.BlockSpec((tm, d), lambda i: (i, 0)),
                   pl.BlockSpec((d, tm), lambda i: (0, i)),
                   pl.BlockSpec((n_hc, n_keys, tm), lambda i: (0, 0, i))],
        out_shape=[jax.ShapeDtypeStruct((t, d), F32),
                   jax.ShapeDtypeStruct((d, t), BF16),
                   jax.ShapeDtypeStruct((n_hc, n_keys, t), F32)],
        compiler_params=_params("arbitrary"),
        name="mix_sample" if per_token else "mix_prompt",
    )(x, cy, ay, mod, w_out, vec(b_out), vec(ln_g), vec(ln_b), w_query, sub_keys)


def _top_values(s, with_rank):
    work = s
    rank = jnp.full(s.shape, float(PEER_TOPK), F32) if with_rank else None
    vals = []
    for r in range(PEER_TOPK):
        m = jnp.max(work, axis=0, keepdims=True)
        hit = work == m
        if with_rank:
            rank = jnp.where(hit, float(r), rank)
        work = jnp.where(hit, -jnp.inf, work)
        vals.append(m)
    return rank, vals


def _route_kernel(s_ref, r2_ref, e2_ref, n1_ref, e1_ref):
    k = PEER_TOPK
    s1 = s_ref[0]
    s2 = s_ref[1]
    _, v1 = _top_values(s1, with_rank=False)
    rank2, v2 = _top_values(s2, with_rank=True)
    rowk = lax.broadcasted_iota(jnp.int32, (k,) + s1.shape[1:], 0)

    def stack(vals):
        out = jnp.zeros(rowk.shape, F32)
        for r in range(k):
            out = jnp.where(rowk == r, vals[r], out)
        return out

    v1m = stack(v1)
    v2m = stack(v2)
    row8 = lax.broadcasted_iota(jnp.int32, (8,) + s1.shape[1:], 0)
    cands = [v1[0] + v2m]
    for a in range(1, 8):
        cands.append(jnp.where(row8 < k // (a + 1), v1[a] + v2m[0:8], -jnp.inf))
    cands.append(v1m[8:k] + v2[0])
    work = jnp.concatenate(cands, axis=0)
    thr = None
    for _ in range(k):
        thr = jnp.max(work, axis=0, keepdims=True)
        work = jnp.where(work == thr, -jnp.inf, work)
    e2top = jnp.exp(v2m - v2[0])
    z = jnp.zeros_like(thr)
    n1 = jnp.zeros(s1.shape, F32)
    for a in range(k):
        sel = (v1[a] + v2m) >= thr
        cnt = jnp.sum(jnp.where(sel, 1.0, 0.0), axis=0, keepdims=True)
        z = z + jnp.exp(v1[a] - v1[0]) * jnp.sum(jnp.where(sel, e2top, 0.0), axis=0, keepdims=True)
        n1 = jnp.where(s1 == v1[a], cnt, n1)
    e1 = jnp.where(s1 >= v1[k - 1], jnp.exp(s1 - v1[0]) / z, 0.0)
    e2 = jnp.where(rank2 < float(k), jnp.exp(s2 - v2[0]), 0.0)
    r2_ref[0] = rank2.astype(r2_ref.dtype)
    e2_ref[0] = e2.astype(e2_ref.dtype)
    n1_ref[0] = _bf16_pair(n1)
    e1_ref[0] = _bf16_pair(e1)


def _bf16_pair(x):
    bits = pltpu.bitcast(x.astype(BF16).astype(F32), jnp.uint32)
    return bits | lax.shift_right_logical(bits, jnp.uint32(16))


def _route(st, *, tl):
    n_hc, n_keys, t = st.shape
    heads = n_hc // 2
    out = lambda dt: jax.ShapeDtypeStruct((heads, n_keys, t), dt)
    spec = pl.BlockSpec((1, n_keys, tl), lambda i, h: (h, 0, i))
    return pl.pallas_call(
        _route_kernel,
        grid=(t // tl, heads),
        in_specs=[pl.BlockSpec((2, n_keys, tl), lambda i, h: (h, 0, i))],
        out_specs=[spec] * 4,
        out_shape=[out(BF16), out(BF16), out(jnp.uint32), out(jnp.uint32)],
        compiler_params=_params("arbitrary", "arbitrary"),
        name="route",
    )(st)


GELU_C1 = math.sqrt(2.0 / math.pi)
GELU_C2 = GELU_C1 * 0.044715


def _gelu_tanh(x):
    hx = 0.5 * x
    return hx + hx * jnp.tanh(x * (GELU_C1 + GELU_C2 * (x * x)))


def _packed_rows(words):
    return pltpu.bitcast(jnp.broadcast_to(words, (8, words.shape[1])), BF16)


def _peer_kernel(h2t_ref, u_ref, vt_ref, r2_ref, e2_ref, n1_ref, e1_ref, x1_ref, mod_ref, g_ref, b_ref, y_ref,
                 at0_ref, at1_ref, zt0_ref, zt1_ref, acc_ref, h2s_ref, *, per_token_mod, alpha, n_keys, slab, n_blocks):
    t = pl.program_id(1)
    ek, tq = at0_ref.shape
    heads = r2_ref.shape[0]
    sub = 16
    col = min(256, tq)
    groups = slab // n_keys
    n_slabs = ek // slab

    def hidden(s, at_ref):
        rows = pl.ds(pl.multiple_of(s * slab, slab), slab)
        u = pltpu.bitcast(u_ref[pl.ds(pl.multiple_of(s * (slab // 2), slab // 2), slab // 2), :], BF16)
        at_ref[rows, :] = jnp.dot(u, h2s_ref[...], preferred_element_type=F32)

    def mask(s, at_ref, zt_ref):
        base = pl.multiple_of(s * slab, slab)
        for g in range(groups):
            i1 = (t - 1) * (ek // n_keys) + s * groups + g
            n1w = [n1_ref[h, pl.ds(i1, 1), :] for h in range(heads)]
            e1w = [e1_ref[h, pl.ds(i1, 1), :] for h in range(heads)]
            for c in range(tq // col):
                tok = slice(c * col, (c + 1) * col)
                n1b = [_packed_rows(n1w[h][:, tok]) for h in range(heads)]
                e1b = [_packed_rows(e1w[h][:, tok]) for h in range(heads)]
                for k in range(n_keys // sub):
                    rows = pl.ds(base + (g * n_keys + k * sub), sub)
                    keys = slice(k * sub, (k + 1) * sub)
                    act = _gelu_tanh(at_ref[rows, tok].astype(BF16))
                    w = jnp.zeros((sub, col), BF16)
                    for h in range(heads):
                        w = jnp.where(r2_ref[h, keys, tok] < n1b[h], w + e2_ref[h, keys, tok] * e1b[h], w)
                    zt_ref[rows, tok] = w * act

    def project(piece, zt_ref):
        rows = pl.ds(pl.multiple_of(piece * slab, slab), slab)
        vt = pltpu.bitcast(vt_ref[pl.ds(pl.multiple_of(piece * (slab // 2), slab // 2), slab // 2), :], BF16)
        acc_ref[rows, :] += jnp.dot(vt, zt_ref[...], preferred_element_type=F32)

    n_pieces = acc_ref.shape[0] // slab
    per_iter = n_pieces // n_slabs

    def step(parity, with_hidden=True, with_project=True):
        at = (at0_ref, at1_ref)
        zt = (zt0_ref, zt1_ref)
        cur, other = parity, 1 - parity
        for s in range(n_slabs):
            if with_hidden:
                hidden(s, at[other])
            mask(s, at[cur], zt[cur])
            if with_project:
                for i in range(per_iter):
                    project(s * per_iter + i, zt[other])

    @pl.when(t == 0)
    def _():
        acc_ref[...] = jnp.zeros(acc_ref.shape, F32)
        h2s_ref[...] = h2t_ref[...]

        def body(s, carry):
            hidden(s, at0_ref)
            return carry

        lax.fori_loop(0, n_slabs, body, 0)

    @pl.when(t == 1)
    def _():
        step(0, with_project=False)

    @pl.when((t >= 2) & (t < n_blocks) & (t % 2 == 1))
    def _():
        step(0)

    @pl.when((t >= 2) & (t < n_blocks) & (t % 2 == 0))
    def _():
        step(1)

    @pl.when(t == n_blocks)
    def _():
        step((n_blocks - 1) % 2, with_hidden=False)

    @pl.when(t == n_blocks + 1)
    def _():
        last = zt0_ref if (n_blocks - 1) % 2 == 0 else zt1_ref

        def body(p, carry):
            project(p, last)
            return carry

        lax.fori_loop(0, n_pieces, body, 0)
        gate2 = mod_ref[5] if per_token_mod else mod_ref[5, 0]
        ff = acc_ref[...].T
        y_ref[...] = _ln(alpha * x1_ref[...] + gate2 * ff) * g_ref[...] + b_ref[...]


def _peer(h2t, u_b, vt_b, r2, e2, n1, e1, x1, mod, ln_g, ln_b, *, tq, ek, slab, rows_per_mod, alpha):
    d, t = h2t.shape
    ne = 2 * u_b.shape[0]
    heads, n_keys, _ = r2.shape
    n_blocks = ne // ek
    assert n_blocks >= 2
    per_token = rows_per_mod is None
    if per_token:
        mod_spec = pl.BlockSpec((6, tq, d), lambda i, j: (0, i, 0))
    else:
        mod_spec = pl.BlockSpec((6, 1, 1, d), lambda i, j: (0, (i * tq) // rows_per_mod, 0, 0))
    tab = pl.BlockSpec((heads, n_keys, tq), lambda i, j: (0, 0, i))
    vec = lambda a: a.reshape(1, d)
    clamp = lambda b: jnp.clip(b, 0, n_blocks - 1)
    return pl.pallas_call(
        functools.partial(_peer_kernel, per_token_mod=per_token, alpha=alpha, n_keys=n_keys, slab=slab,
                          n_blocks=n_blocks),
        grid=(t // tq, n_blocks + 2),
        in_specs=[pl.BlockSpec((d, tq), lambda i, j: (0, i)),
                  pl.BlockSpec((ek // 2, d), lambda i, j: (clamp(j), 0)),
                  pl.BlockSpec((d // 2, ek), lambda i, j: (0, clamp(j - 2))),
                  tab, tab, tab, tab,
                  pl.BlockSpec((tq, d), lambda i, j: (i, 0)),
                  mod_spec, _full((1, d)), _full((1, d))],
        out_specs=pl.BlockSpec((tq, d), lambda i, j: (i, 0)),
        out_shape=jax.ShapeDtypeStruct((t, d), F32),
        scratch_shapes=[pltpu.VMEM((ek, tq), F32), pltpu.VMEM((ek, tq), F32),
                        pltpu.VMEM((ek, tq), BF16), pltpu.VMEM((ek, tq), BF16), pltpu.VMEM((d, tq), F32),
                        pltpu.VMEM((d, tq), BF16)],
        compiler_params=_params("arbitrary", "arbitrary"),
        name="peer_sample" if per_token else "peer_prompt",
    )(h2t, u_b, vt_b, r2, e2, n1, e1, x1, mod, vec(ln_g), vec(ln_b))


def _pack_kernel(x_ref, o_ref, *, transpose):
    x = x_ref[...]
    if transpose:
        x = x.T
    o_ref[...] = pltpu.bitcast(x.astype(BF16), jnp.uint32)


def _pack_bf16(x, *, rows, transpose):
    n, d = x.shape
    if transpose:
        out_spec, out_shape = pl.BlockSpec((d // 2, rows), lambda i: (0, i)), (d // 2, n)
    else:
        out_spec, out_shape = pl.BlockSpec((rows // 2, d), lambda i: (i, 0)), (n // 2, d)
    return pl.pallas_call(
        functools.partial(_pack_kernel, transpose=transpose),
        grid=(n // rows,),
        in_specs=[pl.BlockSpec((rows, d), lambda i: (i, 0))],
        out_specs=out_spec,
        out_shape=jax.ShapeDtypeStruct(out_shape, jnp.uint32),
        compiler_params=_params("arbitrary"),
        name="pack_vt" if transpose else "pack_u",
    )(x)


def kernel(x_prompt, x_sample, cache_k, cache_v, state_conv, page_table, c_prompt, c_sample, w_ada, b_ada, w_in, b_in, conv_w, conv_b, conv_ln_g, conv_ln_b, lambda_q1, lambda_k1, lambda_q2, lambda_k2, attn_subln_g, rel_bias, w_out, b_out, ln1_g, ln1_b, peer_w_query, peer_sub_keys, peer_u, peer_v, ln2_g, ln2_b):
    batch, seq, d = x_prompt.shape
    nb, dec_seq, _ = x_sample.shape
    depth = w_ada.shape[0]
    assert depth == 1 and dec_seq == 1
    heads = cache_k.shape[3]
    hd = cache_k.shape[4]
    assert hd == 2 * ATT_HEAD_DIM
    d_attn = heads * hd
    d_conv = conv_w.shape[2]
    hist = conv_w.shape[1] - 1
    assert hist <= CONV_PAD - 1
    page = cache_k.shape[2]
    n_keys = peer_sub_keys.shape[3]
    ne = peer_u.shape[1]
    alpha = (2 * depth) ** 0.25
    lambda_init = 0.8 - 0.6 * math.exp(-0.3 * 0)
    t_p = batch * seq
    ts = min(TOKEN_TILE, seq)
    tq_peer = min(PEER_TOKEN_TILE, seq)
    ek = min(PEER_EXPERT_TILE, ne)
    assert seq % ts == 0 and nb % LANES == 0 and page >= MAX_DISTANCE

    l = 0
    n_c = batch + nb
    n_c_pad = -(-n_c // 8) * 8
    c_all = jnp.concatenate([c_prompt, c_sample, jnp.zeros((n_c_pad - n_c, d), F32)], axis=0)
    mod = _ada(c_all, w_ada[l], b_ada[l])
    mod_p = mod[:, :batch].reshape(6, batch, 1, d)
    mod_s = mod[:, batch:n_c]

    w_in_b = w_in[l].astype(BF16)
    b_in_r = b_in[l].reshape(1, -1)
    w_out_b = w_out[l].astype(BF16)
    wq_b = peer_w_query[l].astype(BF16)
    sk_b = peer_sub_keys[l].reshape(-1, n_keys, peer_sub_keys.shape[-1]).astype(BF16)
    u_b = _pack_bf16(peer_u[l], rows=min(512, ne), transpose=False)
    vt_b = _pack_bf16(peer_v[l], rows=min(512, ne), transpose=True)
    lams = [a[l].reshape(1, -1) for a in (lambda_q1, lambda_k1, lambda_q2, lambda_k2)]
    subln_g = attn_subln_g[l]

    xp = x_prompt.reshape(t_p, d)
    glu_p, k_p, v_p, kb_p, qt_p, vt_p = _inproj(xp, mod_p, w_in_b, b_in_r, tm=ts, rows_per_mod=seq,
                                                d_conv=d_conv, d_attn=d_attn)
    cy_p = _conv_prompt(glu_p, conv_w[l], conv_b[l], conv_ln_g[l], conv_ln_b[l], batch=batch, seq=seq, ts=ts)
    bias = _bias_tiles(rel_bias, heads=heads, ts=ts)
    ay_p = _attn_prompt(qt_p, kb_p, vt_p, bias, subln_g, lams, batch=batch, seq=seq, heads=heads, ts=ts,
                        lambda_init=lambda_init)
    x1_p, h2t_p, st_p = _mix(xp, cy_p, ay_p, mod_p, w_out_b, b_out[l], ln1_g[l], ln1_b[l], wq_b, sk_b,
                             tm=ts, rows_per_mod=seq, alpha=alpha)
    r2, e2, n1, e1 = _route(st_p, tl=min(ROUTE_TOKEN_TILE, t_p))
    y_p = _peer(h2t_p, u_b, vt_b, r2, e2, n1, e1, x1_p, mod_p, ln2_g[l], ln2_b[l],
                tq=tq_peer, ek=ek, slab=min(PEER_EXPERT_SLAB, ek), rows_per_mod=seq, alpha=alpha)

    xs = x_sample.reshape(nb, d)
    glu_s, k_s, v_s, q_s = _inproj(xs, mod_s, w_in_b, b_in_r, tm=nb, rows_per_mod=None,
                                   d_conv=d_conv, d_attn=d_attn)
    state = state_conv[l]
    cy_s = _conv_sample(jnp.swapaxes(state, 0, 1), glu_s, conv_w[l], conv_b[l], conv_ln_g[l], conv_ln_b[l])
    n_pool = cache_k.shape[1]
    pool_rows = lambda a: a.reshape(depth * n_pool, page * heads, hd)
    ay_s = _attn_sample(page_table + l * n_pool, rel_bias, q_s, k_s, v_s, pool_rows(cache_k), pool_rows(cache_v),
                        subln_g, lams, heads=heads, lambda_init=lambda_init)
    x1_s, h2t_s, st_s = _mix(xs, cy_s, ay_s, mod_s, w_out_b, b_out[l], ln1_g[l], ln1_b[l], wq_b, sk_b,
                             tm=nb, rows_per_mod=None, alpha=alpha)
    r2s, e2s, n1s, e1s = _route(st_s, tl=nb)
    y_s = _peer(h2t_s, u_b, vt_b, r2s, e2s, n1s, e1s, x1_s, mod_s, ln2_g[l], ln2_b[l],
                tq=nb, ek=ek, slab=min(PEER_EXPERT_SLAB, ek), rows_per_mod=None, alpha=alpha)

    kv_p = (depth, batch, seq, heads, hd)
    kv_s = (depth, nb, dec_seq, heads, hd)
    conv_p = glu_p.reshape(batch, seq, d_conv)[:, seq - hist:][None]
    conv_s = jnp.concatenate([state[:, 1:], glu_s[:, None, :]], axis=1)[None]
    return (y_p.reshape(batch, seq, d), y_s.reshape(nb, dec_seq, d),
            k_p.reshape(kv_p), v_p.reshape(kv_p), conv_p,
            k_s.reshape(kv_s), v_s.reshape(kv_s), conv_s)
```

```python
import functools
import math

import jax
import jax.numpy as jnp
from jax import lax
from jax.experimental import pallas as pl
from jax.experimental.pallas import tpu as pltpu

F32 = jnp.float32
BF16 = jnp.bfloat16

LN_EPS = 1e-5
NEG_INF = -1e30
ATT_HEAD_DIM = 64
N_BUCKETS = 32
MAX_DISTANCE = 128
PEER_TOPK = 16
LOG2E = math.log2(math.e)
LANES = 128
VMEM_LIMIT_BYTES = 56 * 1024 * 1024
TOKEN_TILE = 1024
PEER_TOKEN_TILE = 512
PEER_EXPERT_TILE = 1024
PEER_EXPERT_SLAB = 256
ROUTE_TOKEN_TILE = 256


def _params(*sem):
    return pltpu.CompilerParams(dimension_semantics=sem, vmem_limit_bytes=VMEM_LIMIT_BYTES)


def _ln(x):
    mu = jnp.mean(x, axis=-1, keepdims=True)
    xc = x - mu
    var = jnp.mean(xc * xc, axis=-1, keepdims=True)
    return xc * lax.rsqrt(var + LN_EPS)


def _full(shape):
    return pl.BlockSpec(shape, lambda *_: (0,) * len(shape))


def _ada_kernel(c_ref, w_ref, b_ref, o_ref):
    c = c_ref[...]
    s = c * jax.nn.sigmoid(c)
    o_ref[0] = jnp.dot(s.astype(BF16), w_ref[...].astype(BF16), preferred_element_type=F32) + b_ref[0]


def _ada(c_all, w_ada, b_ada):
    nc, d = c_all.shape
    return pl.pallas_call(
        _ada_kernel,
        grid=(6,),
        in_specs=[_full((nc, d)),
                  pl.BlockSpec((d, d), lambda k: (0, k)),
                  pl.BlockSpec((1, 1, d), lambda k: (k, 0, 0))],
        out_specs=pl.BlockSpec((1, nc, d), lambda k: (k, 0, 0)),
        out_shape=jax.ShapeDtypeStruct((6, nc, d), F32),
        compiler_params=_params("arbitrary"),
        name="ada",
    )(c_all, w_ada, b_ada.reshape(6, 1, d))


def _inproj_kernel(x_ref, mod_ref, w_ref, b_ref, *outs, per_token_mod, d_conv, d_attn, transposed):
    x = x_ref[...]
    if per_token_mod:
        sh1, sc1 = mod_ref[0], mod_ref[1]
    else:
        sh1, sc1 = mod_ref[0, 0], mod_ref[1, 0]
    h = _ln(x) * (1.0 + sc1) + sh1
    z = jnp.dot(h.astype(BF16), w_ref[...], preferred_element_type=F32) + b_ref[...]
    ga = z[:, :d_conv]
    gb = z[:, d_conv:2 * d_conv]
    o = 2 * d_conv
    q = z[:, o:o + d_attn]
    k = z[:, o + d_attn:o + 2 * d_attn]
    v = z[:, o + 2 * d_attn:o + 3 * d_attn]
    glu = ga * jax.nn.sigmoid(gb)
    if transposed:
        glu_ref, k_ref, v_ref, kb_ref, qt_ref, vt_ref = outs
        kb_ref[...] = k.astype(BF16)
        qt_ref[0] = (q * (ATT_HEAD_DIM ** -0.5 * LOG2E)).T.astype(BF16)
        vt_ref[0] = v.T.astype(BF16)
    else:
        glu_ref, k_ref, v_ref, q_ref = outs
        q_ref[...] = q
    glu_ref[...] = glu
    heads = d_attn // (2 * ATT_HEAD_DIM)
    for h in range(heads):
        cols = slice(h * 2 * ATT_HEAD_DIM, (h + 1) * 2 * ATT_HEAD_DIM)
        k_ref[pl.ds(h, x.shape[0], stride=heads), :] = k[:, cols]
        v_ref[pl.ds(h, x.shape[0], stride=heads), :] = v[:, cols]


def _inproj(x, mod, w_in, b_in, *, tm, rows_per_mod, d_conv, d_attn):
    t, d = x.shape
    n_in = w_in.shape[1]
    nt = t // tm
    per_token = rows_per_mod is None
    if per_token:
        mod_spec = pl.BlockSpec((6, tm, d), lambda i: (0, i, 0))
    else:
        mod_spec = pl.BlockSpec((6, 1, 1, d), lambda i: (0, (i * tm) // rows_per_mod, 0, 0))
    row = lambda n, dt: (jax.ShapeDtypeStruct((t, n), dt), pl.BlockSpec((tm, n), lambda i: (i, 0)))
    hd = 2 * ATT_HEAD_DIM
    heads = d_attn // hd
    head_rows = (jax.ShapeDtypeStruct((t * heads, hd), F32), pl.BlockSpec((tm * heads, hd), lambda i: (i, 0)))
    outs = [row(d_conv, F32), head_rows, head_rows]
    if per_token:
        outs.append(row(d_attn, F32))
    else:
        outs.append(row(d_attn, BF16))
        tr = (jax.ShapeDtypeStruct((nt, d_attn, tm), BF16), pl.BlockSpec((1, d_attn, tm), lambda i: (i, 0, 0)))
        outs += [tr, tr]
    return pl.pallas_call(
        functools.partial(_inproj_kernel, per_token_mod=per_token, d_conv=d_conv, d_attn=d_attn,
                          transposed=not per_token),
        grid=(nt,),
        in_specs=[pl.BlockSpec((tm, d), lambda i: (i, 0)), mod_spec, _full((d, n_in)), _full((1, n_in))],
        out_specs=[o[1] for o in outs],
        out_shape=[o[0] for o in outs],
        compiler_params=_params("arbitrary"),
        name="inproj_sample" if per_token else "inproj_prompt",
    )(x, mod, w_in, b_in)


CONV_PAD = 32


def _conv_post(acc, cb_ref, g_ref, b_ref):
    y = _ln(acc + cb_ref[...]) * g_ref[...] + b_ref[...]
    return y * jax.nn.sigmoid(y)


def _conv_prompt_kernel(glu_ref, cw_ref, cb_ref, g_ref, b_ref, y_ref, full_ref, sh_ref, *, ts, width, chunk):
    s = pl.program_id(1)
    hist = width - 1

    @pl.when(s == 0)
    def _():
        full_ref[0:CONV_PAD, :] = jnp.zeros((CONV_PAD, full_ref.shape[1]), F32)

    @pl.when(s > 0)
    def _():
        full_ref[0:CONV_PAD, :] = full_ref[ts:ts + CONV_PAD, :]

    full_ref[CONV_PAD:CONV_PAD + ts, :] = glu_ref[...]
    n_sh = sh_ref.shape[1]
    for r in range(1, 8):
        sh_ref[r - 1] = full_ref[r:r + n_sh, :]
    base = CONV_PAD - hist
    for c in range(ts // chunk):
        r0 = c * chunk
        acc = None
        for w in range(width):
            q, r = divmod(base + w, 8)
            rows = slice(r0 + 8 * q, r0 + 8 * q + chunk)
            tap = (full_ref[rows, :] if r == 0 else sh_ref[r - 1, rows, :]) * cw_ref[w:w + 1, :]
            acc = tap if acc is None else acc + tap
        y_ref[r0:r0 + chunk, :] = _conv_post(acc, cb_ref, g_ref, b_ref).astype(y_ref.dtype)


def _conv_prompt(glu, conv_w, conv_b, ln_g, ln_b, *, batch, seq, ts):
    t, dc = glu.shape
    width = conv_w.shape[0]
    ns = seq // ts
    vec = lambda a: a.reshape(1, dc)
    return pl.pallas_call(
        functools.partial(_conv_prompt_kernel, ts=ts, width=width, chunk=min(64, ts)),
        grid=(batch, ns),
        in_specs=[pl.BlockSpec((ts, dc), lambda b, s: (b * ns + s, 0)),
                  _full((width, dc)), _full((1, dc)), _full((1, dc)), _full((1, dc))],
        out_specs=pl.BlockSpec((ts, dc), lambda b, s: (b * ns + s, 0)),
        out_shape=jax.ShapeDtypeStruct((t, dc), BF16),
        scratch_shapes=[pltpu.VMEM((CONV_PAD + ts, dc), F32), pltpu.VMEM((7, CONV_PAD + ts - 8, dc), F32)],
        compiler_params=_params("arbitrary", "arbitrary"),
        name="conv_prompt",
    )(glu, conv_w, vec(conv_b), vec(ln_g), vec(ln_b))


def _conv_sample_kernel(st_ref, glu_ref, cw_ref, cb_ref, g_ref, b_ref, y_ref, *, width):
    hist = width - 1
    acc = glu_ref[...] * cw_ref[hist:hist + 1, :]
    for w in range(hist):
        acc = acc + st_ref[w] * cw_ref[w:w + 1, :]
    y_ref[...] = _conv_post(acc, cb_ref, g_ref, b_ref).astype(y_ref.dtype)


def _conv_sample(state_t, glu, conv_w, conv_b, ln_g, ln_b):
    hist, nb, dc = state_t.shape
    width = conv_w.shape[0]
    vec = lambda a: a.reshape(1, dc)
    return pl.pallas_call(
        functools.partial(_conv_sample_kernel, width=width),
        grid=(1,),
        in_specs=[_full((hist, nb, dc)), _full((nb, dc)), _full((width, dc)),
                  _full((1, dc)), _full((1, dc)), _full((1, dc))],
        out_specs=_full((nb, dc)),
        out_shape=jax.ShapeDtypeStruct((nb, dc), BF16),
        compiler_params=_params("arbitrary"),
        name="conv_sample",
    )(state_t, glu, conv_w, vec(conv_b), vec(ln_g), vec(ln_b))


def _bucket(n):
    max_exact = N_BUCKETS // 2
    nf = jnp.maximum(n, 1).astype(F32)
    large = max_exact + (jnp.log(nf / max_exact) / math.log(MAX_DISTANCE / max_exact)
                         * (N_BUCKETS - max_exact)).astype(jnp.int32)
    large = jnp.minimum(large, N_BUCKETS - 1)
    return jnp.where(n < max_exact, n, large)


def _bias_of(n, rb_ref, h):
    bucket = _bucket(n)
    far = jnp.full(n.shape, rb_ref[N_BUCKETS - 1, h], F32)
    out = far
    for j in range(N_BUCKETS - 1):
        out = jnp.where(bucket == j, rb_ref[j, h], out)
    return out - far


def _bias_kernel(rb_ref, o_ref, *, ts):
    h = pl.program_id(0)
    blk = LANES
    ik = lax.broadcasted_iota(jnp.int32, (blk, blk), 0)
    jq = lax.broadcasted_iota(jnp.int32, (blk, blk), 1)
    for d in range(2):
        for bi in range(ts // blk):
            for bj in range(ts // blk):
                off = (bj - bi) * blk + ts * (1 - d)
                if off + (blk - 1) < 0:
                    tile = jnp.full((blk, blk), NEG_INF, F32)
                elif off - (blk - 1) >= MAX_DISTANCE:
                    tile = jnp.zeros((blk, blk), F32)
                else:
                    n = jq - ik + off
                    tile = jnp.where(n >= 0, _bias_of(jnp.maximum(n, 0), rb_ref, h) * LOG2E, NEG_INF)
                o_ref[0, d, bi * blk:(bi + 1) * blk, bj * blk:(bj + 1) * blk] = tile


def _bias_tiles(rel_bias, *, heads, ts):
    return pl.pallas_call(
        functools.partial(_bias_kernel, ts=ts),
        grid=(heads,),
        in_specs=[pl.BlockSpec(memory_space=pltpu.SMEM)],
        out_specs=pl.BlockSpec((1, 2, ts, ts), lambda h: (h, 0, 0, 0)),
        out_shape=jax.ShapeDtypeStruct((heads, 2, ts, ts), F32),
        compiler_params=_params("arbitrary"),
        name="bias_tiles",
    )(rel_bias)


def _lambda(lq1, lk1, lq2, lk2, lambda_init):
    s1 = jnp.sum(lq1[...] * lk1[...], axis=-1, keepdims=True)
    s2 = jnp.sum(lq2[...] * lk2[...], axis=-1, keepdims=True)
    return jnp.exp(s1) - jnp.exp(s2) + lambda_init


def _attn_kernel(qt_ref, k_ref, vt_ref, bias_ref, g_ref, lq1, lk1, lq2, lk2, o_ref,
                 qs_ref, m_ref, l_ref, acc_ref, sa_ref, sb_ref, *, ts, lambda_init):
    qi = pl.program_id(2)
    d = ATT_HEAD_DIM
    qt = qt_ref[0]
    row = lax.broadcasted_iota(jnp.int32, qt.shape, 0)
    zero = jnp.zeros_like(qt)
    qs_ref[:, :ts] = jnp.where(row < d, qt, zero)
    qs_ref[:, ts:] = jnp.where(row >= d, qt, zero)
    m_ref[...] = jnp.full(m_ref.shape, NEG_INF, F32)
    l_ref[...] = jnp.zeros(l_ref.shape, F32)
    acc_ref[...] = jnp.zeros(acc_ref.shape, F32)

    def scores(ki, bias):
        kblk = k_ref[pl.ds(pl.multiple_of(ki * ts, ts), ts), :]
        s = jnp.dot(kblk, qs_ref[...], preferred_element_type=F32)
        if bias is not None:
            s = s + jnp.concatenate([bias, bias], axis=1)
        return s

    def update(s_ref, ki):
        s = s_ref[...]
        m_prev = m_ref[...]
        m_new = jnp.maximum(m_prev, jnp.max(s, axis=0, keepdims=True))
        alpha = jnp.exp2(m_prev - m_new)
        p = jnp.exp2(s - m_new)
        l_ref[...] = alpha * l_ref[...] + jnp.sum(p, axis=0, keepdims=True)
        acc_ref[...] = alpha * acc_ref[...] + jnp.dot(vt_ref[ki], p.astype(BF16), preferred_element_type=F32)
        m_ref[...] = m_new

    n_far = jnp.maximum(qi - 1, 0)
    sa_ref[...] = scores(qi, bias_ref[0, 1])

    @pl.when(qi > 0)
    def _():
        sb_ref[...] = scores(qi - 1, bias_ref[0, 0])
        update(sa_ref, qi)

    def far_pair(jj, carry):
        ka = qi - 2 - 2 * jj
        sa_ref[...] = scores(ka, None)
        update(sb_ref, ka + 1)
        sb_ref[...] = scores(ka - 1, None)
        update(sa_ref, ka)
        return carry

    lax.fori_loop(0, n_far // 2, far_pair, 0)

    @pl.when(n_far % 2 == 1)
    def _():
        sa_ref[...] = scores(0, None)
        update(sb_ref, 1)

    @pl.when(qi % 2 == 0)
    def _():
        update(sa_ref, 0)

    @pl.when(qi % 2 == 1)
    def _():
        update(sb_ref, 0)

    lam = _lambda(lq1, lk1, lq2, lk2, lambda_init)
    inv_l = 1.0 / l_ref[...]
    acc = acc_ref[...]
    o = acc[:, :ts] * inv_l[:, :ts] - lam * (acc[:, ts:] * inv_l[:, ts:])
    o = o * lax.rsqrt(jnp.mean(o * o, axis=0, keepdims=True) + LN_EPS)
    o = o * g_ref[...] * (1.0 - lambda_init)
    o_ref[...] = o.T.astype(o_ref.dtype)


def _attn_prompt(qt, kb, vt, bias, subln_g, lams, *, batch, seq, heads, ts, lambda_init):
    t, d_attn = kb.shape
    hd = 2 * ATT_HEAD_DIM
    nq = seq // ts
    lam_spec = _full((1, ATT_HEAD_DIM))
    return pl.pallas_call(
        functools.partial(_attn_kernel, ts=ts, lambda_init=lambda_init),
        grid=(batch, heads, nq),
        in_specs=[pl.BlockSpec((1, hd, ts), lambda b, h, q: (b * nq + q, h, 0)),
                  pl.BlockSpec((seq, hd), lambda b, h, q: (b, h)),
                  pl.BlockSpec((nq, hd, ts), lambda b, h, q: (b, h, 0)),
                  pl.BlockSpec((1, 2, ts, ts), lambda b, h, q: (h, 0, 0, 0)),
                  _full((hd, 1)), lam_spec, lam_spec, lam_spec, lam_spec],
        out_specs=pl.BlockSpec((ts, hd), lambda b, h, q: (b * nq + q, h)),
        out_shape=jax.ShapeDtypeStruct((t, d_attn), BF16),
        scratch_shapes=[pltpu.VMEM((hd, 2 * ts), BF16), pltpu.VMEM((1, 2 * ts), F32),
                        pltpu.VMEM((1, 2 * ts), F32), pltpu.VMEM((hd, 2 * ts), F32),
                        pltpu.VMEM((ts, 2 * ts), F32), pltpu.VMEM((ts, 2 * ts), F32)],
        compiler_params=_params("arbitrary", "arbitrary", "arbitrary"),
        name="attn_prompt",
    )(qt, kb, vt, bias, subln_g.reshape(hd, 1), *lams)


def _decode_kernel(pt_ref, rb_ref, q_ref, kn_ref, vn_ref, g_ref, lq1, lk1, lq2, lk2, *rest,
                   n_pages, page, heads, lambda_init):
    k_refs = rest[:n_pages]
    v_refs = rest[n_pages:2 * n_pages]
    o_ref = rest[2 * n_pages]
    s_ref, bias_ref = rest[2 * n_pages + 1:]
    d = ATT_HEAD_DIM
    hd = 2 * d
    rows = page * heads
    past = n_pages * rows
    nr = 2 * heads
    log_heads = int(math.log2(heads))
    nt = (((1,), (1,)), ((), ()))

    q4 = q_ref[0] * (d ** -0.5)
    lane = lax.broadcasted_iota(jnp.int32, (nr, hd), 1)
    top = lax.broadcasted_iota(jnp.int32, (nr, hd), 0) < heads
    q8 = jnp.concatenate([q4, q4], axis=0)
    q8 = jnp.where(top == (lane < d), q8, 0.0).astype(BF16)

    def own(n_cols):
        r = lax.broadcasted_iota(jnp.int32, (nr, n_cols), 0)
        c = lax.broadcasted_iota(jnp.int32, (nr, n_cols), 1)
        return (r & (heads - 1)) == (c & (heads - 1))

    @pl.when(pl.program_id(0) == 0)
    def _():
        key = lax.shift_right_logical(lax.broadcasted_iota(jnp.int32, (nr, rows), 1), log_heads)
        rh = lax.broadcasted_iota(jnp.int32, (nr, rows), 0) & (heads - 1)
        rh_tail = lax.broadcasted_iota(jnp.int32, (nr, LANES), 0) & (heads - 1)
        bias = jnp.zeros((nr, rows), F32)
        bias_new = jnp.zeros((nr, LANES), F32)
        for h in range(heads):
            bias = jnp.where(rh == h, _bias_of(page - key, rb_ref, h), bias)
            bias_new = jnp.where(rh_tail == h, _bias_of(jnp.zeros((nr, LANES), jnp.int32), rb_ref, h), bias_new)
        bias_ref[:, 0:rows] = bias
        bias_ref[:, rows:rows + LANES] = bias_new

    own_page = own(rows)
    for j in range(n_pages):
        sj = lax.dot_general(q8, k_refs[j][0].astype(BF16), nt, preferred_element_type=F32)
        if j == n_pages - 1:
            sj = sj + bias_ref[:, 0:rows]
        s_ref[:, j * rows:(j + 1) * rows] = jnp.where(own_page, sj, NEG_INF)
    kn = jnp.concatenate([kn_ref[0], jnp.zeros((LANES - heads, hd), F32)], axis=0).astype(BF16)
    s_new = lax.dot_general(q8, kn, nt, preferred_element_type=F32)
    tail_col = lax.broadcasted_iota(jnp.int32, (nr, LANES), 1)
    s_ref[:, past:past + LANES] = jnp.where(own(LANES) & (tail_col < heads),
                                            s_new + bias_ref[:, rows:rows + LANES], NEG_INF)

    s = s_ref[...]
    m = jnp.max(s, axis=-1, keepdims=True)
    p = jnp.exp(s - m)
    l = jnp.sum(p, axis=-1, keepdims=True)
    pb = p.astype(BF16)
    out = jnp.zeros((nr, hd), F32)
    for j in range(n_pages):
        out = out + jnp.dot(pb[:, j * rows:(j + 1) * rows], v_refs[j][0].astype(BF16), preferred_element_type=F32)
    vn = jnp.concatenate([vn_ref[0], jnp.zeros((LANES - heads, hd), F32)], axis=0).astype(BF16)
    out = (out + jnp.dot(pb[:, past:past + LANES], vn, preferred_element_type=F32)) / l
    lam = _lambda(lq1, lk1, lq2, lk2, lambda_init)
    o = out[0:heads, :] - lam * out[heads:nr, :]
    o = o * lax.rsqrt(jnp.mean(o * o, axis=-1, keepdims=True) + LN_EPS)
    o_ref[0] = (o * g_ref[...] * (1.0 - lambda_init)).astype(o_ref.dtype)


def _attn_sample(page_table, rel_bias, q, k_new, v_new, cache_k, cache_v, subln_g, lams, *, heads, lambda_init):
    nb, n_pages = page_table.shape
    n_pool, rows, hd = cache_k.shape
    page = rows // heads
    assert heads & (heads - 1) == 0 and 2 * heads <= 8
    tok = lambda a: a.reshape(nb, heads, hd)
    tok_spec = pl.BlockSpec((1, heads, hd), lambda b, pt: (b, 0, 0))
    lam_spec = pl.BlockSpec((1, ATT_HEAD_DIM), lambda b, pt: (0, 0))
    page_specs = [pl.BlockSpec((1, rows, hd), lambda b, pt, j=j: (pt[b, j], 0, 0)) for j in range(n_pages)]
    grid_spec = pltpu.PrefetchScalarGridSpec(
        num_scalar_prefetch=1,
        grid=(nb,),
        in_specs=[pl.BlockSpec(memory_space=pltpu.SMEM), tok_spec, tok_spec, tok_spec,
                  pl.BlockSpec((1, hd), lambda b, pt: (0, 0)),
                  lam_spec, lam_spec, lam_spec, lam_spec] + page_specs + page_specs,
        out_specs=tok_spec,
        scratch_shapes=[pltpu.VMEM((2 * heads, n_pages * rows + LANES), F32),
                        pltpu.VMEM((2 * heads, rows + LANES), F32)],
    )
    out = pl.pallas_call(
        functools.partial(_decode_kernel, n_pages=n_pages, page=page, heads=heads, lambda_init=lambda_init),
        grid_spec=grid_spec,
        out_shape=jax.ShapeDtypeStruct((nb, heads, hd), BF16),
        compiler_params=_params("arbitrary"),
        name="attn_sample",
    )(page_table, rel_bias, tok(q), tok(k_new), tok(v_new), subln_g.reshape(1, hd), *lams,
      *([cache_k] * n_pages), *([cache_v] * n_pages))
    return out.reshape(nb, heads * hd)


def _mix_kernel(x_ref, cy_ref, ay_ref, mod_ref, wo_ref, bo_ref, g1_ref, b1_ref, wq_ref, sk_ref,
                x1_ref, h2t_ref, st_ref, *, per_token_mod, alpha, d_conv):
    if per_token_mod:
        gate1, sh2, sc2 = mod_ref[2], mod_ref[3], mod_ref[4]
    else:
        gate1, sh2, sc2 = mod_ref[2, 0], mod_ref[3, 0], mod_ref[4, 0]
    mix = (jnp.dot(cy_ref[...], wo_ref[:d_conv, :], preferred_element_type=F32)
           + jnp.dot(ay_ref[...], wo_ref[d_conv:, :], preferred_element_type=F32) + bo_ref[...])
    x1 = _ln(alpha * x_ref[...] + gate1 * mix) * g1_ref[...] + b1_ref[...]
    x1_ref[...] = x1
    h2 = _ln(x1) * (1.0 + sc2) + sh2
    h2b = h2.astype(BF16)
    h2t_ref[...] = h2.T.astype(BF16)
    qh = jnp.dot(h2b, wq_ref[...], preferred_element_type=F32).astype(BF16)
    nk = sk_ref.shape[2]
    for hc in range(sk_ref.shape[0]):
        st_ref[hc] = lax.dot_general(sk_ref[hc], qh[:, hc * nk:(hc + 1) * nk], (((1,), (1,)), ((), ())),
                                     preferred_element_type=F32)


def _mix(x, cy, ay, mod, w_out, b_out, ln_g, ln_b, w_query, sub_keys, *, tm, rows_per_mod, alpha):
    t, d = x.shape
    d_conv = cy.shape[1]
    d_attn = ay.shape[1]
    n_hc, n_keys, half = sub_keys.shape
    per_token = rows_per_mod is None
    if per_token:
        mod_spec = pl.BlockSpec((6, tm, d), lambda i: (0, i, 0))
    else:
        mod_spec = pl.BlockSpec((6, 1, 1, d), lambda i: (0, (i * tm) // rows_per_mod, 0, 0))
    vec = lambda a: a.reshape(1, d)
    return pl.pallas_call(
        functools.partial(_mix_kernel, per_token_mod=per_token, alpha=alpha, d_conv=d_conv),
        grid=(t // tm,),
        in_specs=[pl.BlockSpec((tm, d), lambda i: (i, 0)),
                  pl.BlockSpec((tm, d_conv), lambda i: (i, 0)),
                  pl.BlockSpec((tm, d_attn), lambda i: (i, 0)),
                  mod_spec, _full(w_out.shape), _full((1, d)), _full((1, d)), _full((1, d)),
                  _full(w_query.shape), _full(sub_keys.shape)],
        out_specs=[pl.BlockSpec((tm, d), lambda i: (i, 0)),
                   pl.BlockSpec((d, tm), lambda i: (0, i)),
                   pl.BlockSpec((n_hc, n_keys, tm), lambda i: (0, 0, i))],
        out_shape=[jax.ShapeDtypeStruct((t, d), F32),
                   jax.ShapeDtypeStruct((d, t), BF16),
                   jax.ShapeDtypeStruct((n_hc, n_keys, t), F32)],
        compiler_params=_params("arbitrary"),
        name="mix_sample" if per_token else "mix_prompt",
    )(x, cy, ay, mod, w_out, vec(b_out), vec(ln_g), vec(ln_b), w_query, sub_keys)


def _top_values(s, with_rank):
    work = s
    rank = jnp.full(s.shape, float(PEER_TOPK), F32) if with_rank else None
    vals = []
    for r in range(PEER_TOPK):
        m = jnp.max(work, axis=0, keepdims=True)
        hit = work == m
        if with_rank:
            rank = jnp.where(hit, float(r), rank)
        work = jnp.where(hit, -jnp.inf, work)
        vals.append(m)
    return rank, vals


def _route_kernel(s_ref, r2_ref, e2_ref, n1_ref, e1_ref):
    k = PEER_TOPK
    s1 = s_ref[0]
    s2 = s_ref[1]
    _, v1 = _top_values(s1, with_rank=False)
    rank2, v2 = _top_values(s2, with_rank=True)
    rowk = lax.broadcasted_iota(jnp.int32, (k,) + s1.shape[1:], 0)

    def stack(vals):
        out = jnp.zeros(rowk.shape, F32)
        for r in range(k):
            out = jnp.where(rowk == r, vals[r], out)
        return out

    v1m = stack(v1)
    v2m = stack(v2)
    row8 = lax.broadcasted_iota(jnp.int32, (8,) + s1.shape[1:], 0)
    cands = [v1[0] + v2m]
    for a in range(1, 8):
        cands.append(jnp.where(row8 < k // (a + 1), v1[a] + v2m[0:8], -jnp.inf))
    cands.append(v1m[8:k] + v2[0])
    work = jnp.concatenate(cands, axis=0)
    thr = None
    for _ in range(k):
        thr = jnp.max(work, axis=0, keepdims=True)
        work = jnp.where(work == thr, -jnp.inf, work)
    e2top = jnp.exp(v2m - v2[0])
    z = jnp.zeros_like(thr)
    n1 = jnp.zeros(s1.shape, F32)
    for a in range(k):
        sel = (v1[a] + v2m) >= thr
        cnt = jnp.sum(jnp.where(sel, 1.0, 0.0), axis=0, keepdims=True)
        z = z + jnp.exp(v1[a] - v1[0]) * jnp.sum(jnp.where(sel, e2top, 0.0), axis=0, keepdims=True)
        n1 = jnp.where(s1 == v1[a], cnt, n1)
    e1 = jnp.where(s1 >= v1[k - 1], jnp.exp(s1 - v1[0]) / z, 0.0)
    e2 = jnp.where(rank2 < float(k), jnp.exp(s2 - v2[0]), 0.0)
    r2_ref[0] = rank2.astype(r2_ref.dtype)
    e2_ref[0] = e2.astype(e2_ref.dtype)
    n1_ref[0] = _bf16_pair(n1)
    e1_ref[0] = _bf16_pair(e1)


def _bf16_pair(x):
    bits = pltpu.bitcast(x.astype(BF16).astype(F32), jnp.uint32)
    return bits | lax.shift_right_logical(bits, jnp.uint32(16))


def _route(st, *, tl):
    n_hc, n_keys, t = st.shape
    heads = n_hc // 2
    out = lambda dt: jax.ShapeDtypeStruct((heads, n_keys, t), dt)
    spec = pl.BlockSpec((1, n_keys, tl), lambda i, h: (h, 0, i))
    return pl.pallas_call(
        _route_kernel,
        grid=(t // tl, heads),
        in_specs=[pl.BlockSpec((2, n_keys, tl), lambda i, h: (h, 0, i))],
        out_specs=[spec] * 4,
        out_shape=[out(BF16), out(BF16), out(jnp.uint32), out(jnp.uint32)],
        compiler_params=_params("arbitrary", "arbitrary"),
        name="route",
    )(st)


GELU_C1 = math.sqrt(2.0 / math.pi)
GELU_C2 = GELU_C1 * 0.044715


def _gelu_tanh(x):
    hx = 0.5 * x
    return hx + hx * jnp.tanh(x * (GELU_C1 + GELU_C2 * (x * x)))


def _packed_rows(words):
    return pltpu.bitcast(jnp.broadcast_to(words, (8, words.shape[1])), BF16)


def _peer_kernel(h2t_ref, u_ref, vt_ref, r2_ref, e2_ref, n1_ref, e1_ref, x1_ref, mod_ref, g_ref, b_ref, y_ref,
                 at0_ref, at1_ref, zt0_ref, zt1_ref, acc_ref, h2s_ref, *, per_token_mod, alpha, n_keys, slab, n_blocks):
    t = pl.program_id(1)
    ek, tq = at0_ref.shape
    heads = r2_ref.shape[0]
    sub = 16
    col = min(256, tq)
    groups = slab // n_keys
    n_slabs = ek // slab

    def hidden(s, at_ref):
        rows = pl.ds(pl.multiple_of(s * slab, slab), slab)
        u = pltpu.bitcast(u_ref[pl.ds(pl.multiple_of(s * (slab // 2), slab // 2), slab // 2), :], BF16)
        at_ref[rows, :] = jnp.dot(u, h2s_ref[...], preferred_element_type=F32)

    def mask(s, at_ref, zt_ref):
        base = pl.multiple_of(s * slab, slab)
        for g in range(groups):
            i1 = (t - 1) * (ek // n_keys) + s * groups + g
            n1w = [n1_ref[h, pl.ds(i1, 1), :] for h in range(heads)]
            e1w = [e1_ref[h, pl.ds(i1, 1), :] for h in range(heads)]
            for c in range(tq // col):
                tok = slice(c * col, (c + 1) * col)
                n1b = [_packed_rows(n1w[h][:, tok]) for h in range(heads)]
                e1b = [_packed_rows(e1w[h][:, tok]) for h in range(heads)]
                for k in range(n_keys // sub):
                    rows = pl.ds(base + (g * n_keys + k * sub), sub)
                    keys = slice(k * sub, (k + 1) * sub)
                    act = _gelu_tanh(at_ref[rows, tok].astype(BF16))
                    w = jnp.zeros((sub, col), BF16)
                    for h in range(heads):
                        w = jnp.where(r2_ref[h, keys, tok] < n1b[h], w + e2_ref[h, keys, tok] * e1b[h], w)
                    zt_ref[rows, tok] = w * act

    def project(piece, zt_ref):
        rows = pl.ds(pl.multiple_of(piece * slab, slab), slab)
        vt = pltpu.bitcast(vt_ref[pl.ds(pl.multiple_of(piece * (slab // 2), slab // 2), slab // 2), :], BF16)
        acc_ref[rows, :] += jnp.dot(vt, zt_ref[...], preferred_element_type=F32)

    n_pieces = acc_ref.shape[0] // slab
    per_iter = n_pieces // n_slabs

    def step(parity, with_hidden=True, with_project=True):
        at = (at0_ref, at1_ref)
        zt = (zt0_ref, zt1_ref)
        cur, other = parity, 1 - parity
        for s in range(n_slabs):
            if with_hidden:
                hidden(s, at[other])
            mask(s, at[cur], zt[cur])
            if with_project:
                for i in range(per_iter):
                    project(s * per_iter + i, zt[other])

    @pl.when(t == 0)
    def _():
        acc_ref[...] = jnp.zeros(acc_ref.shape, F32)
        h2s_ref[...] = h2t_ref[...]

        def body(s, carry):
            hidden(s, at0_ref)
            return carry

        lax.fori_loop(0, n_slabs, body, 0)

    @pl.when(t == 1)
    def _():
        step(0, with_project=False)

    @pl.when((t >= 2) & (t < n_blocks) & (t % 2 == 1))
    def _():
        step(0)

    @pl.when((t >= 2) & (t < n_blocks) & (t % 2 == 0))
    def _():
        step(1)

    @pl.when(t == n_blocks)
    def _():
        step((n_blocks - 1) % 2, with_hidden=False)

    @pl.when(t == n_blocks + 1)
    def _():
        last = zt0_ref if (n_blocks - 1) % 2 == 0 else zt1_ref

        def body(p, carry):
            project(p, last)
            return carry

        lax.fori_loop(0, n_pieces, body, 0)
        gate2 = mod_ref[5] if per_token_mod else mod_ref[5, 0]
        ff = acc_ref[...].T
        y_ref[...] = _ln(alpha * x1_ref[...] + gate2 * ff) * g_ref[...] + b_ref[...]


def _peer(h2t, u_b, vt_b, r2, e2, n1, e1, x1, mod, ln_g, ln_b, *, tq, ek, slab, rows_per_mod, alpha):
    d, t = h2t.shape
    ne = 2 * u_b.shape[0]
    heads, n_keys, _ = r2.shape
    n_blocks = ne // ek
    assert n_blocks >= 2
    per_token = rows_per_mod is None
    if per_token:
        mod_spec = pl.BlockSpec((6, tq, d), lambda i, j: (0, i, 0))
    else:
        mod_spec = pl.BlockSpec((6, 1, 1, d), lambda i, j: (0, (i * tq) // rows_per_mod, 0, 0))
    tab = pl.BlockSpec((heads, n_keys, tq), lambda i, j: (0, 0, i))
    vec = lambda a: a.reshape(1, d)
    clamp = lambda b: jnp.clip(b, 0, n_blocks - 1)
    return pl.pallas_call(
        functools.partial(_peer_kernel, per_token_mod=per_token, alpha=alpha, n_keys=n_keys, slab=slab,
                          n_blocks=n_blocks),
        grid=(t // tq, n_blocks + 2),
        in_specs=[pl.BlockSpec((d, tq), lambda i, j: (0, i)),
                  pl.BlockSpec((ek // 2, d), lambda i, j: (clamp(j), 0)),
                  pl.BlockSpec((d // 2, ek), lambda i, j: (0, clamp(j - 2))),
                  tab, tab, tab, tab,
                  pl.BlockSpec((tq, d), lambda i, j: (i, 0)),
                  mod_spec, _full((1, d)), _full((1, d))],
        out_specs=pl.BlockSpec((tq, d), lambda i, j: (i, 0)),
        out_shape=jax.ShapeDtypeStruct((t, d), F32),
        scratch_shapes=[pltpu.VMEM((ek, tq), F32), pltpu.VMEM((ek, tq), F32),
                        pltpu.VMEM((ek, tq), BF16), pltpu.VMEM((ek, tq), BF16), pltpu.VMEM((d, tq), F32),
                        pltpu.VMEM((d, tq), BF16)],
        compiler_params=_params("arbitrary", "arbitrary"),
        name="peer_sample" if per_token else "peer_prompt",
    )(h2t, u_b, vt_b, r2, e2, n1, e1, x1, mod, vec(ln_g), vec(ln_b))


def _pack_kernel(x_ref, o_ref, *, transpose):
    x = x_ref[...]
    if transpose:
        x = x.T
    o_ref[...] = pltpu.bitcast(x.astype(BF16), jnp.uint32)


def _pack_bf16(x, *, rows, transpose):
    n, d = x.shape
    if transpose:
        out_spec, out_shape = pl.BlockSpec((d // 2, rows), lambda i: (0, i)), (d // 2, n)
    else:
        out_spec, out_shape = pl.BlockSpec((rows // 2, d), lambda i: (i, 0)), (n // 2, d)
    return pl.pallas_call(
        functools.partial(_pack_kernel, transpose=transpose),
        grid=(n // rows,),
        in_specs=[pl.BlockSpec((rows, d), lambda i: (i, 0))],
        out_specs=out_spec,
        out_shape=jax.ShapeDtypeStruct(out_shape, jnp.uint32),
        compiler_params=_params("arbitrary"),
        name="pack_vt" if transpose else "pack_u",
    )(x)


def kernel(x_prompt, x_sample, cache_k, cache_v, state_conv, page_table, c_prompt, c_sample, w_ada, b_ada, w_in, b_in, conv_w, conv_b, conv_ln_g, conv_ln_b, lambda_q1, lambda_k1, lambda_q2, lambda_k2, attn_subln_g, rel_bias, w_out, b_out, ln1_g, ln1_b, peer_w_query, peer_sub_keys, peer_u, peer_v, ln2_g, ln2_b):
    batch, seq, d = x_prompt.shape
    nb, dec_seq, _ = x_sample.shape
    depth = w_ada.shape[0]
    assert depth == 1 and dec_seq == 1
    heads = cache_k.shape[3]
    hd = cache_k.shape[4]
    assert hd == 2 * ATT_HEAD_DIM
    d_attn = heads * hd
    d_conv = conv_w.shape[2]
    hist = conv_w.shape[1] - 1
    assert hist <= CONV_PAD - 1
    page = cache_k.shape[2]
    n_keys = peer_sub_keys.shape[3]
    ne = peer_u.shape[1]
    alpha = (2 * depth) ** 0.25
    lambda_init = 0.8 - 0.6 * math.exp(-0.3 * 0)
    t_p = batch * seq
    ts = min(TOKEN_TILE, seq)
    tq_peer = min(PEER_TOKEN_TILE, seq)
    ek = min(PEER_EXPERT_TILE, ne)
    assert seq % ts == 0 and nb % LANES == 0 and page >= MAX_DISTANCE

    l = 0
    n_c = batch + nb
    n_c_pad = -(-n_c // 8) * 8
    c_all = jnp.concatenate([c_prompt, c_sample, jnp.zeros((n_c_pad - n_c, d), F32)], axis=0)
    mod = _ada(c_all, w_ada[l], b_ada[l])
    mod_p = mod[:, :batch].reshape(6, batch, 1, d)
    mod_s = mod[:, batch:n_c]

    w_in_b = w_in[l].astype(BF16)
    b_in_r = b_in[l].reshape(1, -1)
    w_out_b = w_out[l].astype(BF16)
    wq_b = peer_w_query[l].astype(BF16)
    sk_b = peer_sub_keys[l].reshape(-1, n_keys, peer_sub_keys.shape[-1]).astype(BF16)
    u_b = _pack_bf16(peer_u[l], rows=min(512, ne), transpose=False)
    vt_b = _pack_bf16(peer_v[l], rows=min(512, ne), transpose=True)
    lams = [a[l].reshape(1, -1) for a in (lambda_q1, lambda_k1, lambda_q2, lambda_k2)]
    subln_g = attn_subln_g[l]

    xp = x_prompt.reshape(t_p, d)
    glu_p, k_p, v_p, kb_p, qt_p, vt_p = _inproj(xp, mod_p, w_in_b, b_in_r, tm=ts, rows_per_mod=seq,
                                                d_conv=d_conv, d_attn=d_attn)
    cy_p = _conv_prompt(glu_p, conv_w[l], conv_b[l], conv_ln_g[l], conv_ln_b[l], batch=batch, seq=seq, ts=ts)
    bias = _bias_tiles(rel_bias, heads=heads, ts=ts)
    ay_p = _attn_prompt(qt_p, kb_p, vt_p, bias, subln_g, lams, batch=batch, seq=seq, heads=heads, ts=ts,
                        lambda_init=lambda_init)
    x1_p, h2t_p, st_p = _mix(xp, cy_p, ay_p, mod_p, w_out_b, b_out[l], ln1_g[l], ln1_b[l], wq_b, sk_b,
                             tm=ts, rows_per_mod=seq, alpha=alpha)
    r2, e2, n1, e1 = _route(st_p, tl=min(ROUTE_TOKEN_TILE, t_p))
    y_p = _peer(h2t_p, u_b, vt_b, r2, e2, n1, e1, x1_p, mod_p, ln2_g[l], ln2_b[l],
                tq=tq_peer, ek=ek, slab=min(PEER_EXPERT_SLAB, ek), rows_per_mod=seq, alpha=alpha)

    xs = x_sample.reshape(nb, d)
    glu_s, k_s, v_s, q_s = _inproj(xs, mod_s, w_in_b, b_in_r, tm=nb, rows_per_mod=None,
                                   d_conv=d_conv, d_attn=d_attn)
    state = state_conv[l]
    cy_s = _conv_sample(jnp.swapaxes(state, 0, 1), glu_s, conv_w[l], conv_b[l], conv_ln_g[l], conv_ln_b[l])
    n_pool = cache_k.shape[1]
    pool_rows = lambda a: a.reshape(depth * n_pool, page * heads, hd)
    ay_s = _attn_sample(page_table + l * n_pool, rel_bias, q_s, k_s, v_s, pool_rows(cache_k), pool_rows(cache_v),
                        subln_g, lams, heads=heads, lambda_init=lambda_init)
    x1_s, h2t_s, st_s = _mix(xs, cy_s, ay_s, mod_s, w_out_b, b_out[l], ln1_g[l], ln1_b[l], wq_b, sk_b,
                             tm=nb, rows_per_mod=None, alpha=alpha)
    r2s, e2s, n1s, e1s = _route(st_s, tl=nb)
    y_s = _peer(h2t_s, u_b, vt_b, r2s, e2s, n1s, e1s, x1_s, mod_s, ln2_g[l], ln2_b[l],
                tq=nb, ek=ek, slab=min(PEER_EXPERT_SLAB, ek), rows_per_mod=None, alpha=alpha)

    kv_p = (depth, batch, seq, heads, hd)
    kv_s = (depth, nb, dec_seq, heads, hd)
    conv_p = glu_p.reshape(batch, seq, d_conv)[:, seq - hist:][None]
    conv_s = jnp.concatenate([state[:, 1:], glu_s[:, None, :]], axis=1)[None]
    return (y_p.reshape(batch, seq, d), y_s.reshape(nb, dec_seq, d),
            k_p.reshape(kv_p), v_p.reshape(kv_p), conv_p,
            k_s.reshape(kv_s), v_s.reshape(kv_s), conv_s)
```

```python
import functools
import math

import jax
import jax.numpy as jnp
from jax import lax
from jax.experimental import pallas as pl
from jax.experimental.pallas import tpu as pltpu

F32 = jnp.float32
BF16 = jnp.bfloat16

LN_EPS = 1e-5
NEG_INF = -1e30
ATT_HEAD_DIM = 64
N_BUCKETS = 32
MAX_DISTANCE = 128
PEER_TOPK = 16
LOG2E = math.log2(math.e)
LANES = 128
SUBLANES = 8
PACKED_ROWS = 16
VMEM_LIMIT_BYTES = 56 * 1024 * 1024
TOKEN_TILE = 1024
PEER_TOKEN_TILE = 512
PEER_EXPERT_TILE = 1024
PEER_EXPERT_SLAB = 256
PEER_MASK_LANES = 256
ROUTE_TOKEN_TILE = 256


def _params(*sem):
    return pltpu.CompilerParams(dimension_semantics=sem, vmem_limit_bytes=VMEM_LIMIT_BYTES)


def _ln(x):
    mu = jnp.mean(x, axis=-1, keepdims=True)
    xc = x - mu
    var = jnp.mean(xc * xc, axis=-1, keepdims=True)
    return xc * lax.rsqrt(var + LN_EPS)


def _full(shape):
    return pl.BlockSpec(shape, lambda *_: (0,) * len(shape))


def _ada_kernel(c_ref, w_ref, b_ref, o_ref):
    c = c_ref[...]
    s = c * jax.nn.sigmoid(c)
    o_ref[0] = jnp.dot(s.astype(BF16), w_ref[...].astype(BF16), preferred_element_type=F32) + b_ref[0]


def _ada(c_all, w_ada, b_ada):
    nc, d = c_all.shape
    return pl.pallas_call(
        _ada_kernel,
        grid=(6,),
        in_specs=[_full((nc, d)),
                  pl.BlockSpec((d, d), lambda k: (0, k)),
                  pl.BlockSpec((1, 1, d), lambda k: (k, 0, 0))],
        out_specs=pl.BlockSpec((1, nc, d), lambda k: (k, 0, 0)),
        out_shape=jax.ShapeDtypeStruct((6, nc, d), F32),
        compiler_params=_params("arbitrary"),
        name="ada",
    )(c_all, w_ada, b_ada.reshape(6, 1, d))


def _inproj_kernel(x_ref, mod_ref, w_ref, b_ref, *outs, per_token_mod, d_conv, d_attn, transposed):
    x = x_ref[...]
    if per_token_mod:
        sh1, sc1 = mod_ref[0], mod_ref[1]
    else:
        sh1, sc1 = mod_ref[0, 0], mod_ref[1, 0]
    h = _ln(x) * (1.0 + sc1) + sh1
    z = jnp.dot(h.astype(BF16), w_ref[...], preferred_element_type=F32) + b_ref[...]
    ga = z[:, :d_conv]
    gb = z[:, d_conv:2 * d_conv]
    o = 2 * d_conv
    q = z[:, o:o + d_attn]
    k = z[:, o + d_attn:o + 2 * d_attn]
    v = z[:, o + 2 * d_attn:o + 3 * d_attn]
    glu = ga * jax.nn.sigmoid(gb)
    if transposed:
        glu_ref, k_ref, v_ref, kb_ref, qt_ref, vt_ref = outs
        kb_ref[...] = k.astype(BF16)
        qt_ref[0] = (q * (ATT_HEAD_DIM ** -0.5 * LOG2E)).T.astype(BF16)
        vt_ref[0] = v.T.astype(BF16)
    else:
        glu_ref, k_ref, v_ref, q_ref = outs
        q_ref[...] = q
    glu_ref[...] = glu
    heads = d_attn // (2 * ATT_HEAD_DIM)
    for h in range(heads):
        cols = slice(h * 2 * ATT_HEAD_DIM, (h + 1) * 2 * ATT_HEAD_DIM)
        k_ref[pl.ds(h, x.shape[0], stride=heads), :] = k[:, cols]
        v_ref[pl.ds(h, x.shape[0], stride=heads), :] = v[:, cols]


def _inproj(x, mod, w_in, b_in, *, tm, rows_per_mod, d_conv, d_attn):
    t, d = x.shape
    n_in = w_in.shape[1]
    nt = t // tm
    per_token = rows_per_mod is None
    if per_token:
        mod_spec = pl.BlockSpec((6, tm, d), lambda i: (0, i, 0))
    else:
        mod_spec = pl.BlockSpec((6, 1, 1, d), lambda i: (0, (i * tm) // rows_per_mod, 0, 0))
    row = lambda n, dt: (jax.ShapeDtypeStruct((t, n), dt), pl.BlockSpec((tm, n), lambda i: (i, 0)))
    hd = 2 * ATT_HEAD_DIM
    heads = d_attn // hd
    head_rows = (jax.ShapeDtypeStruct((t * heads, hd), F32), pl.BlockSpec((tm * heads, hd), lambda i: (i, 0)))
    outs = [row(d_conv, F32), head_rows, head_rows]
    if per_token:
        outs.append(row(d_attn, F32))
    else:
        outs.append(row(d_attn, BF16))
        tr = (jax.ShapeDtypeStruct((nt, d_attn, tm), BF16), pl.BlockSpec((1, d_attn, tm), lambda i: (i, 0, 0)))
        outs += [tr, tr]
    return pl.pallas_call(
        functools.partial(_inproj_kernel, per_token_mod=per_token, d_conv=d_conv, d_attn=d_attn,
                          transposed=not per_token),
        grid=(nt,),
        in_specs=[pl.BlockSpec((tm, d), lambda i: (i, 0)), mod_spec, _full((d, n_in)), _full((1, n_in))],
        out_specs=[o[1] for o in outs],
        out_shape=[o[0] for o in outs],
        compiler_params=_params("arbitrary"),
        name="inproj_sample" if per_token else "inproj_prompt",
    )(x, mod, w_in, b_in)


CONV_PAD = 32


def _conv_post(acc, cb_ref, g_ref, b_ref):
    y = _ln(acc + cb_ref[...]) * g_ref[...] + b_ref[...]
    return y * jax.nn.sigmoid(y)


def _conv_prompt_kernel(glu_ref, cw_ref, cb_ref, g_ref, b_ref, y_ref, full_ref, sh_ref, *, ts, width, chunk):
    s = pl.program_id(1)
    hist = width - 1

    @pl.when(s == 0)
    def _():
        full_ref[0:CONV_PAD, :] = jnp.zeros((CONV_PAD, full_ref.shape[1]), F32)

    @pl.when(s > 0)
    def _():
        full_ref[0:CONV_PAD, :] = full_ref[ts:ts + CONV_PAD, :]

    full_ref[CONV_PAD:CONV_PAD + ts, :] = glu_ref[...]
    n_sh = sh_ref.shape[1]
    for r in range(1, SUBLANES):
        sh_ref[r - 1] = full_ref[r:r + n_sh, :]
    base = CONV_PAD - hist
    for c in range(ts // chunk):
        r0 = c * chunk
        acc = None
        for w in range(width):
            q, r = divmod(base + w, SUBLANES)
            rows = slice(r0 + SUBLANES * q, r0 + SUBLANES * q + chunk)
            tap = (full_ref[rows, :] if r == 0 else sh_ref[r - 1, rows, :]) * cw_ref[w:w + 1, :]
            acc = tap if acc is None else acc + tap
        y_ref[r0:r0 + chunk, :] = _conv_post(acc, cb_ref, g_ref, b_ref).astype(y_ref.dtype)


def _conv_prompt(glu, conv_w, conv_b, ln_g, ln_b, *, batch, seq, ts):
    t, dc = glu.shape
    width = conv_w.shape[0]
    ns = seq // ts
    vec = lambda a: a.reshape(1, dc)
    return pl.pallas_call(
        functools.partial(_conv_prompt_kernel, ts=ts, width=width, chunk=min(64, ts)),
        grid=(batch, ns),
        in_specs=[pl.BlockSpec((ts, dc), lambda b, s: (b * ns + s, 0)),
                  _full((width, dc)), _full((1, dc)), _full((1, dc)), _full((1, dc))],
        out_specs=pl.BlockSpec((ts, dc), lambda b, s: (b * ns + s, 0)),
        out_shape=jax.ShapeDtypeStruct((t, dc), BF16),
        scratch_shapes=[pltpu.VMEM((CONV_PAD + ts, dc), F32), pltpu.VMEM((SUBLANES - 1, CONV_PAD + ts - SUBLANES, dc), F32)],
        compiler_params=_params("arbitrary", "arbitrary"),
        name="conv_prompt",
    )(glu, conv_w, vec(conv_b), vec(ln_g), vec(ln_b))


def _conv_sample_kernel(st_ref, glu_ref, cw_ref, cb_ref, g_ref, b_ref, y_ref, *, width):
    hist = width - 1
    acc = glu_ref[...] * cw_ref[hist:hist + 1, :]
    for w in range(hist):
        acc = acc + st_ref[w] * cw_ref[w:w + 1, :]
    y_ref[...] = _conv_post(acc, cb_ref, g_ref, b_ref).astype(y_ref.dtype)


def _conv_sample(state_t, glu, conv_w, conv_b, ln_g, ln_b):
    hist, nb, dc = state_t.shape
    width = conv_w.shape[0]
    vec = lambda a: a.reshape(1, dc)
    return pl.pallas_call(
        functools.partial(_conv_sample_kernel, width=width),
        grid=(1,),
        in_specs=[_full((hist, nb, dc)), _full((nb, dc)), _full((width, dc)),
                  _full((1, dc)), _full((1, dc)), _full((1, dc))],
        out_specs=_full((nb, dc)),
        out_shape=jax.ShapeDtypeStruct((nb, dc), BF16),
        compiler_params=_params("arbitrary"),
        name="conv_sample",
    )(state_t, glu, conv_w, vec(conv_b), vec(ln_g), vec(ln_b))


def _bucket(n):
    max_exact = N_BUCKETS // 2
    nf = jnp.maximum(n, 1).astype(F32)
    large = max_exact + (jnp.log(nf / max_exact) / math.log(MAX_DISTANCE / max_exact)
                         * (N_BUCKETS - max_exact)).astype(jnp.int32)
    large = jnp.minimum(large, N_BUCKETS - 1)
    return jnp.where(n < max_exact, n, large)


def _bias_of(n, rb_ref, h):
    bucket = _bucket(n)
    far = jnp.full(n.shape, rb_ref[N_BUCKETS - 1, h], F32)
    out = far
    for j in range(N_BUCKETS - 1):
        out = jnp.where(bucket == j, rb_ref[j, h], out)
    return out - far


def _bias_kernel(rb_ref, o_ref, *, ts):
    h = pl.program_id(0)
    blk = LANES
    ik = lax.broadcasted_iota(jnp.int32, (blk, blk), 0)
    jq = lax.broadcasted_iota(jnp.int32, (blk, blk), 1)
    for d in range(2):
        for bi in range(ts // blk):
            for bj in range(ts // blk):
                off = (bj - bi) * blk + ts * (1 - d)
                if off + (blk - 1) < 0:
                    tile = jnp.full((blk, blk), NEG_INF, F32)
                elif off - (blk - 1) >= MAX_DISTANCE:
                    tile = jnp.zeros((blk, blk), F32)
                else:
                    n = jq - ik + off
                    tile = jnp.where(n >= 0, _bias_of(jnp.maximum(n, 0), rb_ref, h) * LOG2E, NEG_INF)
                o_ref[0, d, bi * blk:(bi + 1) * blk, bj * blk:(bj + 1) * blk] = tile


def _bias_tiles(rel_bias, *, heads, ts):
    return pl.pallas_call(
        functools.partial(_bias_kernel, ts=ts),
        grid=(heads,),
        in_specs=[pl.BlockSpec(memory_space=pltpu.SMEM)],
        out_specs=pl.BlockSpec((1, 2, ts, ts), lambda h: (h, 0, 0, 0)),
        out_shape=jax.ShapeDtypeStruct((heads, 2, ts, ts), F32),
        compiler_params=_params("arbitrary"),
        name="bias_tiles",
    )(rel_bias)


def _lambda(lq1, lk1, lq2, lk2, lambda_init):
    s1 = jnp.sum(lq1[...] * lk1[...], axis=-1, keepdims=True)
    s2 = jnp.sum(lq2[...] * lk2[...], axis=-1, keepdims=True)
    return jnp.exp(s1) - jnp.exp(s2) + lambda_init


def _attn_kernel(qt_ref, k_ref, vt_ref, bias_ref, g_ref, lq1, lk1, lq2, lk2, o_ref,
                 qs_ref, m_ref, l_ref, acc_ref, sa_ref, sb_ref, *, ts, lambda_init):
    qi = pl.program_id(2)
    d = ATT_HEAD_DIM
    qt = qt_ref[0]
    row = lax.broadcasted_iota(jnp.int32, qt.shape, 0)
    zero = jnp.zeros_like(qt)
    qs_ref[:, :ts] = jnp.where(row < d, qt, zero)
    qs_ref[:, ts:] = jnp.where(row >= d, qt, zero)
    m_ref[...] = jnp.full(m_ref.shape, NEG_INF, F32)
    l_ref[...] = jnp.zeros(l_ref.shape, F32)
    acc_ref[...] = jnp.zeros(acc_ref.shape, F32)

    def scores(ki, bias):
        kblk = k_ref[pl.ds(pl.multiple_of(ki * ts, ts), ts), :]
        s = jnp.dot(kblk, qs_ref[...], preferred_element_type=F32)
        if bias is not None:
            s = s + jnp.concatenate([bias, bias], axis=1)
        return s

    def update(s_ref, ki):
        s = s_ref[...]
        m_prev = m_ref[...]
        m_new = jnp.maximum(m_prev, jnp.max(s, axis=0, keepdims=True))
        alpha = jnp.exp2(m_prev - m_new)
        p = jnp.exp2(s - m_new)
        l_ref[...] = alpha * l_ref[...] + jnp.sum(p, axis=0, keepdims=True)
        acc_ref[...] = alpha * acc_ref[...] + jnp.dot(vt_ref[ki], p.astype(BF16), preferred_element_type=F32)
        m_ref[...] = m_new

    n_far = jnp.maximum(qi - 1, 0)
    sa_ref[...] = scores(qi, bias_ref[0, 1])

    @pl.when(qi > 0)
    def _():
        sb_ref[...] = scores(qi - 1, bias_ref[0, 0])
        update(sa_ref, qi)

    def far_pair(jj, carry):
        ka = qi - 2 - 2 * jj
        sa_ref[...] = scores(ka, None)
        update(sb_ref, ka + 1)
        sb_ref[...] = scores(ka - 1, None)
        update(sa_ref, ka)
        return carry

    lax.fori_loop(0, n_far // 2, far_pair, 0)

    @pl.when(n_far % 2 == 1)
    def _():
        sa_ref[...] = scores(0, None)
        update(sb_ref, 1)

    @pl.when(qi % 2 == 0)
    def _():
        update(sa_ref, 0)

    @pl.when(qi % 2 == 1)
    def _():
        update(sb_ref, 0)

    lam = _lambda(lq1, lk1, lq2, lk2, lambda_init)
    inv_l = 1.0 / l_ref[...]
    acc = acc_ref[...]
    o = acc[:, :ts] * inv_l[:, :ts] - lam * (acc[:, ts:] * inv_l[:, ts:])
    o = o * lax.rsqrt(jnp.mean(o * o, axis=0, keepdims=True) + LN_EPS)
    o = o * g_ref[...] * (1.0 - lambda_init)
    o_ref[...] = o.T.astype(o_ref.dtype)


def _attn_prompt(qt, kb, vt, bias, subln_g, lams, *, batch, seq, heads, ts, lambda_init):
    t, d_attn = kb.shape
    hd = 2 * ATT_HEAD_DIM
    nq = seq // ts
    lam_spec = _full((1, ATT_HEAD_DIM))
    return pl.pallas_call(
        functools.partial(_attn_kernel, ts=ts, lambda_init=lambda_init),
        grid=(batch, heads, nq),
        in_specs=[pl.BlockSpec((1, hd, ts), lambda b, h, q: (b * nq + q, h, 0)),
                  pl.BlockSpec((seq, hd), lambda b, h, q: (b, h)),
                  pl.BlockSpec((nq, hd, ts), lambda b, h, q: (b, h, 0)),
                  pl.BlockSpec((1, 2, ts, ts), lambda b, h, q: (h, 0, 0, 0)),
                  _full((hd, 1)), lam_spec, lam_spec, lam_spec, lam_spec],
        out_specs=pl.BlockSpec((ts, hd), lambda b, h, q: (b * nq + q, h)),
        out_shape=jax.ShapeDtypeStruct((t, d_attn), BF16),
        scratch_shapes=[pltpu.VMEM((hd, 2 * ts), BF16), pltpu.VMEM((1, 2 * ts), F32),
                        pltpu.VMEM((1, 2 * ts), F32), pltpu.VMEM((hd, 2 * ts), F32),
                        pltpu.VMEM((ts, 2 * ts), F32), pltpu.VMEM((ts, 2 * ts), F32)],
        compiler_params=_params("arbitrary", "arbitrary", "arbitrary"),
        name="attn_prompt",
    )(qt, kb, vt, bias, subln_g.reshape(hd, 1), *lams)


def _decode_kernel(pt_ref, rb_ref, q_ref, kn_ref, vn_ref, g_ref, lq1, lk1, lq2, lk2, *rest,
                   n_pages, page, heads, lambda_init):
    k_refs = rest[:n_pages]
    v_refs = rest[n_pages:2 * n_pages]
    o_ref = rest[2 * n_pages]
    s_ref, bias_ref = rest[2 * n_pages + 1:]
    d = ATT_HEAD_DIM
    hd = 2 * d
    rows = page * heads
    past = n_pages * rows
    nr = 2 * heads
    log_heads = int(math.log2(heads))
    nt = (((1,), (1,)), ((), ()))

    q4 = q_ref[0] * (d ** -0.5)
    lane = lax.broadcasted_iota(jnp.int32, (nr, hd), 1)
    top = lax.broadcasted_iota(jnp.int32, (nr, hd), 0) < heads
    q8 = jnp.concatenate([q4, q4], axis=0)
    q8 = jnp.where(top == (lane < d), q8, 0.0).astype(BF16)

    def own(n_cols):
        r = lax.broadcasted_iota(jnp.int32, (nr, n_cols), 0)
        c = lax.broadcasted_iota(jnp.int32, (nr, n_cols), 1)
        return (r & (heads - 1)) == (c & (heads - 1))

    @pl.when(pl.program_id(0) == 0)
    def _():
        key = lax.shift_right_logical(lax.broadcasted_iota(jnp.int32, (nr, rows), 1), log_heads)
        rh = lax.broadcasted_iota(jnp.int32, (nr, rows), 0) & (heads - 1)
        rh_tail = lax.broadcasted_iota(jnp.int32, (nr, LANES), 0) & (heads - 1)
        bias = jnp.zeros((nr, rows), F32)
        bias_new = jnp.zeros((nr, LANES), F32)
        for h in range(heads):
            bias = jnp.where(rh == h, _bias_of(page - key, rb_ref, h), bias)
            bias_new = jnp.where(rh_tail == h, _bias_of(jnp.zeros((nr, LANES), jnp.int32), rb_ref, h), bias_new)
        bias_ref[:, 0:rows] = bias
        bias_ref[:, rows:rows + LANES] = bias_new

    own_page = own(rows)
    for j in range(n_pages):
        sj = lax.dot_general(q8, k_refs[j][0].astype(BF16), nt, preferred_element_type=F32)
        if j == n_pages - 1:
            sj = sj + bias_ref[:, 0:rows]
        s_ref[:, j * rows:(j + 1) * rows] = jnp.where(own_page, sj, NEG_INF)
    kn = jnp.concatenate([kn_ref[0], jnp.zeros((LANES - heads, hd), F32)], axis=0).astype(BF16)
    s_new = lax.dot_general(q8, kn, nt, preferred_element_type=F32)
    tail_col = lax.broadcasted_iota(jnp.int32, (nr, LANES), 1)
    s_ref[:, past:past + LANES] = jnp.where(own(LANES) & (tail_col < heads),
                                            s_new + bias_ref[:, rows:rows + LANES], NEG_INF)

    s = s_ref[...]
    m = jnp.max(s, axis=-1, keepdims=True)
    p = jnp.exp(s - m)
    l = jnp.sum(p, axis=-1, keepdims=True)
    pb = p.astype(BF16)
    out = jnp.zeros((nr, hd), F32)
    for j in range(n_pages):
        out = out + jnp.dot(pb[:, j * rows:(j + 1) * rows], v_refs[j][0].astype(BF16), preferred_element_type=F32)
    vn = jnp.concatenate([vn_ref[0], jnp.zeros((LANES - heads, hd), F32)], axis=0).astype(BF16)
    out = (out + jnp.dot(pb[:, past:past + LANES], vn, preferred_element_type=F32)) / l
    lam = _lambda(lq1, lk1, lq2, lk2, lambda_init)
    o = out[0:heads, :] - lam * out[heads:nr, :]
    o = o * lax.rsqrt(jnp.mean(o * o, axis=-1, keepdims=True) + LN_EPS)
    o_ref[0] = (o * g_ref[...] * (1.0 - lambda_init)).astype(o_ref.dtype)


def _attn_sample(page_table, rel_bias, q, k_new, v_new, cache_k, cache_v, subln_g, lams, *, heads, lambda_init):
    nb, n_pages = page_table.shape
    n_pool, rows, hd = cache_k.shape
    page = rows // heads
    assert heads & (heads - 1) == 0 and 2 * heads <= 8
    tok = lambda a: a.reshape(nb, heads, hd)
    tok_spec = pl.BlockSpec((1, heads, hd), lambda b, pt: (b, 0, 0))
    lam_spec = pl.BlockSpec((1, ATT_HEAD_DIM), lambda b, pt: (0, 0))
    page_specs = [pl.BlockSpec((1, rows, hd), lambda b, pt, j=j: (pt[b, j], 0, 0)) for j in range(n_pages)]
    grid_spec = pltpu.PrefetchScalarGridSpec(
        num_scalar_prefetch=1,
        grid=(nb,),
        in_specs=[pl.BlockSpec(memory_space=pltpu.SMEM), tok_spec, tok_spec, tok_spec,
                  pl.BlockSpec((1, hd), lambda b, pt: (0, 0)),
                  lam_spec, lam_spec, lam_spec, lam_spec] + page_specs + page_specs,
        out_specs=tok_spec,
        scratch_shapes=[pltpu.VMEM((2 * heads, n_pages * rows + LANES), F32),
                        pltpu.VMEM((2 * heads, rows + LANES), F32)],
    )
    out = pl.pallas_call(
        functools.partial(_decode_kernel, n_pages=n_pages, page=page, heads=heads, lambda_init=lambda_init),
        grid_spec=grid_spec,
        out_shape=jax.ShapeDtypeStruct((nb, heads, hd), BF16),
        compiler_params=_params("arbitrary"),
        name="attn_sample",
    )(page_table, rel_bias, tok(q), tok(k_new), tok(v_new), subln_g.reshape(1, hd), *lams,
      *([cache_k] * n_pages), *([cache_v] * n_pages))
    return out.reshape(nb, heads * hd)


def _mix_kernel(x_ref, cy_ref, ay_ref, mod_ref, wo_ref, bo_ref, g1_ref, b1_ref, wq_ref, sk_ref,
                x1_ref, h2t_ref, st_ref, *, per_token_mod, alpha, d_conv):
    if per_token_mod:
        gate1, sh2, sc2 = mod_ref[2], mod_ref[3], mod_ref[4]
    else:
        gate1, sh2, sc2 = mod_ref[2, 0], mod_ref[3, 0], mod_ref[4, 0]
    mix = (jnp.dot(cy_ref[...], wo_ref[:d_conv, :], preferred_element_type=F32)
           + jnp.dot(ay_ref[...], wo_ref[d_conv:, :], preferred_element_type=F32) + bo_ref[...])
    x1 = _ln(alpha * x_ref[...] + gate1 * mix) * g1_ref[...] + b1_ref[...]
    x1_ref[...] = x1
    h2 = _ln(x1) * (1.0 + sc2) + sh2
    h2b = h2.astype(BF16)
    h2t_ref[...] = h2.T.astype(BF16)
    qh = jnp.dot(h2b, wq_ref[...], preferred_element_type=F32).astype(BF16)
    nk = sk_ref.shape[2]
    for hc in range(sk_ref.shape[0]):
        st_ref[hc] = lax.dot_general(sk_ref[hc], qh[:, hc * nk:(hc + 1) * nk], (((1,), (1,)), ((), ())),
                                     preferred_element_type=F32)


def _mix(x, cy, ay, mod, w_out, b_out, ln_g, ln_b, w_query, sub_keys, *, tm, rows_per_mod, alpha):
    t, d = x.shape
    d_conv = cy.shape[1]
    d_attn = ay.shape[1]
    n_hc, n_keys, half = sub_keys.shape
    per_token = rows_per_mod is None
    if per_token:
        mod_spec = pl.BlockSpec((6, tm, d), lambda i: (0, i, 0))
    else:
        mod_spec = pl.BlockSpec((6, 1, 1, d), lambda i: (0, (i * tm) // rows_per_mod, 0, 0))
    vec = lambda a: a.reshape(1, d)
    return pl.pallas_call(
        functools.partial(_mix_kernel, per_token_mod=per_token, alpha=alpha, d_conv=d_conv),
        grid=(t // tm,),
        in_specs=[pl.BlockSpec((tm, d), lambda i: (i, 0)),
                  pl.BlockSpec((tm, d_conv), lambda i: (i, 0)),
                  pl.BlockSpec((tm, d_attn), lambda i: (i, 0)),
                  mod_spec, _full(w_out.shape), _full((1, d)), _full((1, d)), _full((1, d)),
                  _full(w_query.shape), _full(sub_keys.shape)],
        out_specs=[pl.BlockSpec((tm, d), lambda i: (i, 0)),
                   pl.BlockSpec((d, tm), lambda i: (0, i)),
                   pl.BlockSpec((n_hc, n_keys, tm), lambda i: (0, 0, i))],
        out_shape=[jax.ShapeDtypeStruct((t, d), F32),
                   jax.ShapeDtypeStruct((d, t), BF16),
                   jax.ShapeDtypeStruct((n_hc, n_keys, t), F32)],
        compiler_params=_params("arbitrary"),
        name="mix_sample" if per_token else "mix_prompt",
    )(x, cy, ay, mod, w_out, vec(b_out), vec(ln_g), vec(ln_b), w_query, sub_keys)


def _top_values(s, with_rank):
    work = s
    rank = jnp.full(s.shape, float(PEER_TOPK), F32) if with_rank else None
    vals = []
    for r in range(PEER_TOPK):
        m = jnp.max(work, axis=0, keepdims=True)
        hit = work == m
        if with_rank:
            rank = jnp.where(hit, float(r), rank)
        work = jnp.where(hit, -jnp.inf, work)
        vals.append(m)
    return rank, vals


def _route_kernel(s_ref, r2_ref, e2_ref, n1_ref, e1_ref):
    k = PEER_TOPK
    s1 = s_ref[0]
    s2 = s_ref[1]
    _, v1 = _top_values(s1, with_rank=False)
    rank2, v2 = _top_values(s2, with_rank=True)
    rowk = lax.broadcasted_iota(jnp.int32, (k,) + s1.shape[1:], 0)

    def stack(vals):
        out = jnp.zeros(rowk.shape, F32)
        for r in range(k):
            out = jnp.where(rowk == r, vals[r], out)
        return out

    v1m = stack(v1)
    v2m = stack(v2)
    half = k // 2
    row_half = lax.broadcasted_iota(jnp.int32, (half,) + s1.shape[1:], 0)
    cands = [v1[0] + v2m]
    for a in range(1, half):
        cands.append(jnp.where(row_half < k // (a + 1), v1[a] + v2m[0:half], -jnp.inf))
    cands.append(v1m[half:k] + v2[0])
    work = jnp.concatenate(cands, axis=0)
    thr = None
    for _ in range(k):
        thr = jnp.max(work, axis=0, keepdims=True)
        work = jnp.where(work == thr, -jnp.inf, work)
    e2top = jnp.exp(v2m - v2[0])
    z = jnp.zeros_like(thr)
    n1 = jnp.zeros(s1.shape, F32)
    for a in range(k):
        sel = (v1[a] + v2m) >= thr
        cnt = jnp.sum(jnp.where(sel, 1.0, 0.0), axis=0, keepdims=True)
        z = z + jnp.exp(v1[a] - v1[0]) * jnp.sum(jnp.where(sel, e2top, 0.0), axis=0, keepdims=True)
        n1 = jnp.where(s1 == v1[a], cnt, n1)
    e1 = jnp.where(s1 >= v1[k - 1], jnp.exp(s1 - v1[0]) / z, 0.0)
    e2 = jnp.where(rank2 < float(k), jnp.exp(s2 - v2[0]), 0.0)
    r2_ref[0] = rank2.astype(r2_ref.dtype)
    e2_ref[0] = e2.astype(e2_ref.dtype)
    n1_ref[0] = _bf16_pair(n1)
    e1_ref[0] = _bf16_pair(e1)


def _bf16_pair(x):
    bits = pltpu.bitcast(x.astype(BF16).astype(F32), jnp.uint32)
    return bits | lax.shift_right_logical(bits, jnp.uint32(16))


def _route(st, *, tl):
    n_hc, n_keys, t = st.shape
    heads = n_hc // 2
    out = lambda dt: jax.ShapeDtypeStruct((heads, n_keys, t), dt)
    spec = pl.BlockSpec((1, n_keys, tl), lambda i, h: (h, 0, i))
    return pl.pallas_call(
        _route_kernel,
        grid=(t // tl, heads),
        in_specs=[pl.BlockSpec((2, n_keys, tl), lambda i, h: (h, 0, i))],
        out_specs=[spec] * 4,
        out_shape=[out(BF16), out(BF16), out(jnp.uint32), out(jnp.uint32)],
        compiler_params=_params("arbitrary", "arbitrary"),
        name="route",
    )(st)


GELU_C1 = math.sqrt(2.0 / math.pi)
GELU_C2 = GELU_C1 * 0.044715


def _gelu_tanh(x):
    hx = 0.5 * x
    return hx + hx * jnp.tanh(x * (GELU_C1 + GELU_C2 * (x * x)))


def _packed_rows(words):
    return pltpu.bitcast(jnp.broadcast_to(words, (SUBLANES, words.shape[1])), BF16)


def _peer_kernel(h2t_ref, u_ref, vt_ref, r2_ref, e2_ref, n1_ref, e1_ref, x1_ref, mod_ref, g_ref, b_ref, y_ref,
                 at0_ref, at1_ref, zt0_ref, zt1_ref, acc_ref, h2s_ref, *, per_token_mod, alpha, n_keys, slab, n_blocks):
    t = pl.program_id(1)
    ek, tq = at0_ref.shape
    heads = r2_ref.shape[0]
    sub = PACKED_ROWS
    col = min(PEER_MASK_LANES, tq)
    groups = slab // n_keys
    n_slabs = ek // slab

    def hidden(s, at_ref):
        rows = pl.ds(pl.multiple_of(s * slab, slab), slab)
        u = pltpu.bitcast(u_ref[pl.ds(pl.multiple_of(s * (slab // 2), slab // 2), slab // 2), :], BF16)
        at_ref[rows, :] = jnp.dot(u, h2s_ref[...], preferred_element_type=F32)

    def mask(s, at_ref, zt_ref):
        base = pl.multiple_of(s * slab, slab)
        for g in range(groups):
            i1 = (t - 1) * (ek // n_keys) + s * groups + g
            n1w = [n1_ref[h, pl.ds(i1, 1), :] for h in range(heads)]
            e1w = [e1_ref[h, pl.ds(i1, 1), :] for h in range(heads)]
            for c in range(tq // col):
                tok = slice(c * col, (c + 1) * col)
                n1b = [_packed_rows(n1w[h][:, tok]) for h in range(heads)]
                e1b = [_packed_rows(e1w[h][:, tok]) for h in range(heads)]
                for k in range(n_keys // sub):
                    rows = pl.ds(base + (g * n_keys + k * sub), sub)
                    keys = slice(k * sub, (k + 1) * sub)
                    act = _gelu_tanh(at_ref[rows, tok].astype(BF16))
                    w = jnp.zeros((sub, col), BF16)
                    for h in range(heads):
                        w = jnp.where(r2_ref[h, keys, tok] < n1b[h], w + e2_ref[h, keys, tok] * e1b[h], w)
                    zt_ref[rows, tok] = w * act

    def project(piece, zt_ref):
        rows = pl.ds(pl.multiple_of(piece * slab, slab), slab)
        vt = pltpu.bitcast(vt_ref[pl.ds(pl.multiple_of(piece * (slab // 2), slab // 2), slab // 2), :], BF16)
        acc_ref[rows, :] += jnp.dot(vt, zt_ref[...], preferred_element_type=F32)

    n_pieces = acc_ref.shape[0] // slab
    per_iter = n_pieces // n_slabs

    def step(parity, with_hidden=True, with_project=True):
        at = (at0_ref, at1_ref)
        zt = (zt0_ref, zt1_ref)
        cur, other = parity, 1 - parity
        for s in range(n_slabs):
            if with_hidden:
                hidden(s, at[other])
            mask(s, at[cur], zt[cur])
            if with_project:
                for i in range(per_iter):
                    project(s * per_iter + i, zt[other])

    @pl.when(t == 0)
    def _():
        acc_ref[...] = jnp.zeros(acc_ref.shape, F32)
        h2s_ref[...] = h2t_ref[...]

        def body(s, carry):
            hidden(s, at0_ref)
            return carry

        lax.fori_loop(0, n_slabs, body, 0)

    @pl.when(t == 1)
    def _():
        step(0, with_project=False)

    @pl.when((t >= 2) & (t < n_blocks) & (t % 2 == 1))
    def _():
        step(0)

    @pl.when((t >= 2) & (t < n_blocks) & (t % 2 == 0))
    def _():
        step(1)

    @pl.when(t == n_blocks)
    def _():
        step((n_blocks - 1) % 2, with_hidden=False)

    @pl.when(t == n_blocks + 1)
    def _():
        last = zt0_ref if (n_blocks - 1) % 2 == 0 else zt1_ref

        def body(p, carry):
            project(p, last)
            return carry

        lax.fori_loop(0, n_pieces, body, 0)
        gate2 = mod_ref[5] if per_token_mod else mod_ref[5, 0]
        ff = acc_ref[...].T
        y_ref[...] = _ln(alpha * x1_ref[...] + gate2 * ff) * g_ref[...] + b_ref[...]


def _peer(h2t, u_b, vt_b, r2, e2, n1, e1, x1, mod, ln_g, ln_b, *, tq, ek, slab, rows_per_mod, alpha):
    d, t = h2t.shape
    ne = 2 * u_b.shape[0]
    heads, n_keys, _ = r2.shape
    n_blocks = ne // ek
    assert n_blocks >= 2
    per_token = rows_per_mod is None
    if per_token:
        mod_spec = pl.BlockSpec((6, tq, d), lambda i, j: (0, i, 0))
    else:
        mod_spec = pl.BlockSpec((6, 1, 1, d), lambda i, j: (0, (i * tq) // rows_per_mod, 0, 0))
    tab = pl.BlockSpec((heads, n_keys, tq), lambda i, j: (0, 0, i))
    vec = lambda a: a.reshape(1, d)
    clamp = lambda b: jnp.clip(b, 0, n_blocks - 1)
    return pl.pallas_call(
        functools.partial(_peer_kernel, per_token_mod=per_token, alpha=alpha, n_keys=n_keys, slab=slab,
                          n_blocks=n_blocks),
        grid=(t // tq, n_blocks + 2),
        in_specs=[pl.BlockSpec((d, tq), lambda i, j: (0, i)),
                  pl.BlockSpec((ek // 2, d), lambda i, j: (clamp(j), 0)),
                  pl.BlockSpec((d // 2, ek), lambda i, j: (0, clamp(j - 2))),
                  tab, tab, tab, tab,
                  pl.BlockSpec((tq, d), lambda i, j: (i, 0)),
                  mod_spec, _full((1, d)), _full((1, d))],
        out_specs=pl.BlockSpec((tq, d), lambda i, j: (i, 0)),
        out_shape=jax.ShapeDtypeStruct((t, d), F32),
        scratch_shapes=[pltpu.VMEM((ek, tq), F32), pltpu.VMEM((ek, tq), F32),
                        pltpu.VMEM((ek, tq), BF16), pltpu.VMEM((ek, tq), BF16), pltpu.VMEM((d, tq), F32),
                        pltpu.VMEM((d, tq), BF16)],
        compiler_params=_params("arbitrary", "arbitrary"),
        name="peer_sample" if per_token else "peer_prompt",
    )(h2t, u_b, vt_b, r2, e2, n1, e1, x1, mod, vec(ln_g), vec(ln_b))


def _pack_kernel(x_ref, o_ref, *, transpose):
    x = x_ref[...]
    if transpose:
        x = x.T
    o_ref[...] = pltpu.bitcast(x.astype(BF16), jnp.uint32)


def _pack_bf16(x, *, rows, transpose):
    n, d = x.shape
    if transpose:
        out_spec, out_shape = pl.BlockSpec((d // 2, rows), lambda i: (0, i)), (d // 2, n)
    else:
        out_spec, out_shape = pl.BlockSpec((rows // 2, d), lambda i: (i, 0)), (n // 2, d)
    return pl.pallas_call(
        functools.partial(_pack_kernel, transpose=transpose),
        grid=(n // rows,),
        in_specs=[pl.BlockSpec((rows, d), lambda i: (i, 0))],
        out_specs=out_spec,
        out_shape=jax.ShapeDtypeStruct(out_shape, jnp.uint32),
        compiler_params=_params("arbitrary"),
        name="pack_vt" if transpose else "pack_u",
    )(x)


def kernel(x_prompt, x_sample, cache_k, cache_v, state_conv, page_table, c_prompt, c_sample, w_ada, b_ada, w_in, b_in, conv_w, conv_b, conv_ln_g, conv_ln_b, lambda_q1, lambda_k1, lambda_q2, lambda_k2, attn_subln_g, rel_bias, w_out, b_out, ln1_g, ln1_b, peer_w_query, peer_sub_keys, peer_u, peer_v, ln2_g, ln2_b):
    batch, seq, d = x_prompt.shape
    nb, dec_seq, _ = x_sample.shape
    depth = w_ada.shape[0]
    assert depth == 1 and dec_seq == 1
    heads = cache_k.shape[3]
    hd = cache_k.shape[4]
    assert hd == 2 * ATT_HEAD_DIM
    d_attn = heads * hd
    d_conv = conv_w.shape[2]
    hist = conv_w.shape[1] - 1
    assert hist <= CONV_PAD - 1
    page = cache_k.shape[2]
    n_keys = peer_sub_keys.shape[3]
    ne = peer_u.shape[1]
    alpha = (2 * depth) ** 0.25
    lambda_init = 0.8 - 0.6 * math.exp(-0.3 * 0)
    t_p = batch * seq
    ts = min(TOKEN_TILE, seq)
    tq_peer = min(PEER_TOKEN_TILE, seq)
    ek = min(PEER_EXPERT_TILE, ne)
    assert seq % ts == 0 and nb % LANES == 0 and page >= MAX_DISTANCE

    l = 0
    n_c = batch + nb
    n_c_pad = -(-n_c // SUBLANES) * SUBLANES
    c_all = jnp.concatenate([c_prompt, c_sample, jnp.zeros((n_c_pad - n_c, d), F32)], axis=0)
    mod = _ada(c_all, w_ada[l], b_ada[l])
    mod_p = mod[:, :batch].reshape(6, batch, 1, d)
    mod_s = mod[:, batch:n_c]

    w_in_b = w_in[l].astype(BF16)
    b_in_r = b_in[l].reshape(1, -1)
    w_out_b = w_out[l].astype(BF16)
    wq_b = peer_w_query[l].astype(BF16)
    sk_b = peer_sub_keys[l].reshape(-1, n_keys, peer_sub_keys.shape[-1]).astype(BF16)
    u_b = _pack_bf16(peer_u[l], rows=min(512, ne), transpose=False)
    vt_b = _pack_bf16(peer_v[l], rows=min(512, ne), transpose=True)
    lams = [a[l].reshape(1, -1) for a in (lambda_q1, lambda_k1, lambda_q2, lambda_k2)]
    subln_g = attn_subln_g[l]

    xp = x_prompt.reshape(t_p, d)
    glu_p, k_p, v_p, kb_p, qt_p, vt_p = _inproj(xp, mod_p, w_in_b, b_in_r, tm=ts, rows_per_mod=seq,
                                                d_conv=d_conv, d_attn=d_attn)
    cy_p = _conv_prompt(glu_p, conv_w[l], conv_b[l], conv_ln_g[l], conv_ln_b[l], batch=batch, seq=seq, ts=ts)
    bias = _bias_tiles(rel_bias, heads=heads, ts=ts)
    ay_p = _attn_prompt(qt_p, kb_p, vt_p, bias, subln_g, lams, batch=batch, seq=seq, heads=heads, ts=ts,
                        lambda_init=lambda_init)
    x1_p, h2t_p, st_p = _mix(xp, cy_p, ay_p, mod_p, w_out_b, b_out[l], ln1_g[l], ln1_b[l], wq_b, sk_b,
                             tm=ts, rows_per_mod=seq, alpha=alpha)
    r2, e2, n1, e1 = _route(st_p, tl=min(ROUTE_TOKEN_TILE, t_p))
    y_p = _peer(h2t_p, u_b, vt_b, r2, e2, n1, e1, x1_p, mod_p, ln2_g[l], ln2_b[l],
                tq=tq_peer, ek=ek, slab=min(PEER_EXPERT_SLAB, ek), rows_per_mod=seq, alpha=alpha)

    xs = x_sample.reshape(nb, d)
    glu_s, k_s, v_s, q_s = _inproj(xs, mod_s, w_in_b, b_in_r, tm=nb, rows_per_mod=None,
                                   d_conv=d_conv, d_attn=d_attn)
    state = state_conv[l]
    cy_s = _conv_sample(jnp.swapaxes(state, 0, 1), glu_s, conv_w[l], conv_b[l], conv_ln_g[l], conv_ln_b[l])
    n_pool = cache_k.shape[1]
    pool_rows = lambda a: a.reshape(depth * n_pool, page * heads, hd)
    ay_s = _attn_sample(page_table + l * n_pool, rel_bias, q_s, k_s, v_s, pool_rows(cache_k), pool_rows(cache_v),
                        subln_g, lams, heads=heads, lambda_init=lambda_init)
    x1_s, h2t_s, st_s = _mix(xs, cy_s, ay_s, mod_s, w_out_b, b_out[l], ln1_g[l], ln1_b[l], wq_b, sk_b,
                             tm=nb, rows_per_mod=None, alpha=alpha)
    r2s, e2s, n1s, e1s = _route(st_s, tl=nb)
    y_s = _peer(h2t_s, u_b, vt_b, r2s, e2s, n1s, e1s, x1_s, mod_s, ln2_g[l], ln2_b[l],
                tq=nb, ek=ek, slab=min(PEER_EXPERT_SLAB, ek), rows_per_mod=None, alpha=alpha)

    kv_p = (depth, batch, seq, heads, hd)
    kv_s = (depth, nb, dec_seq, heads, hd)
    conv_p = glu_p.reshape(batch, seq, d_conv)[:, seq - hist:][None]
    conv_s = jnp.concatenate([state[:, 1:], glu_s[:, None, :]], axis=1)[None]
    return (y_p.reshape(batch, seq, d), y_s.reshape(nb, dec_seq, d),
            k_p.reshape(kv_p), v_p.reshape(kv_p), conv_p,
            k_s.reshape(kv_s), v_s.reshape(kv_s), conv_s)
```

```python
import functools
import math

import jax
import jax.numpy as jnp
from jax import lax
from jax.experimental import pallas as pl
from jax.experimental.pallas import tpu as pltpu

F32 = jnp.float32
BF16 = jnp.bfloat16

LN_EPS = 1e-5
NEG_INF = -1e30
ATT_HEAD_DIM = 64
N_BUCKETS = 32
MAX_DISTANCE = 128
PEER_TOPK = 16
LOG2E = math.log2(math.e)
LANES = 128
SUBLANES = 8
PACKED_ROWS = 16
VMEM_LIMIT_BYTES = 56 * 1024 * 1024
TOKEN_TILE = 1024
PEER_TOKEN_TILE = 512
PEER_EXPERT_TILE = 1024
PEER_EXPERT_SLAB = 256
PEER_MASK_LANES = 256
ROUTE_TOKEN_TILE = 512
ROUTE_COLUMN = 256


def _params(*sem):
    return pltpu.CompilerParams(dimension_semantics=sem, vmem_limit_bytes=VMEM_LIMIT_BYTES)


def _ln(x):
    mu = jnp.mean(x, axis=-1, keepdims=True)
    xc = x - mu
    var = jnp.mean(xc * xc, axis=-1, keepdims=True)
    return xc * lax.rsqrt(var + LN_EPS)


def _full(shape):
    return pl.BlockSpec(shape, lambda *_: (0,) * len(shape))


def _ada_kernel(c_ref, w_ref, b_ref, o_ref):
    c = c_ref[...]
    s = c * jax.nn.sigmoid(c)
    o_ref[0] = jnp.dot(s.astype(BF16), w_ref[...].astype(BF16), preferred_element_type=F32) + b_ref[0]


def _ada(c_all, w_ada, b_ada):
    nc, d = c_all.shape
    return pl.pallas_call(
        _ada_kernel,
        grid=(6,),
        in_specs=[_full((nc, d)),
                  pl.BlockSpec((d, d), lambda k: (0, k)),
                  pl.BlockSpec((1, 1, d), lambda k: (k, 0, 0))],
        out_specs=pl.BlockSpec((1, nc, d), lambda k: (k, 0, 0)),
        out_shape=jax.ShapeDtypeStruct((6, nc, d), F32),
        compiler_params=_params("arbitrary"),
        name="ada",
    )(c_all, w_ada, b_ada.reshape(6, 1, d))


def _inproj_kernel(x_ref, mod_ref, w_ref, b_ref, *outs, per_token_mod, d_conv, d_attn, transposed):
    x = x_ref[...]
    if per_token_mod:
        sh1, sc1 = mod_ref[0], mod_ref[1]
    else:
        sh1, sc1 = mod_ref[0, 0], mod_ref[1, 0]
    h = _ln(x) * (1.0 + sc1) + sh1
    z = jnp.dot(h.astype(BF16), w_ref[...], preferred_element_type=F32) + b_ref[...]
    ga = z[:, :d_conv]
    gb = z[:, d_conv:2 * d_conv]
    o = 2 * d_conv
    q = z[:, o:o + d_attn]
    k = z[:, o + d_attn:o + 2 * d_attn]
    v = z[:, o + 2 * d_attn:o + 3 * d_attn]
    glu = ga * jax.nn.sigmoid(gb)
    if transposed:
        glu_ref, k_ref, v_ref, kb_ref, qt_ref, vt_ref = outs
        kb_ref[...] = k.astype(BF16)
        qt_ref[0] = (q * (ATT_HEAD_DIM ** -0.5 * LOG2E)).T.astype(BF16)
        vt_ref[0] = v.T.astype(BF16)
    else:
        glu_ref, k_ref, v_ref, q_ref = outs
        q_ref[...] = q
    glu_ref[...] = glu
    heads = d_attn // (2 * ATT_HEAD_DIM)
    for h in range(heads):
        cols = slice(h * 2 * ATT_HEAD_DIM, (h + 1) * 2 * ATT_HEAD_DIM)
        k_ref[pl.ds(h, x.shape[0], stride=heads), :] = k[:, cols]
        v_ref[pl.ds(h, x.shape[0], stride=heads), :] = v[:, cols]


def _inproj(x, mod, w_in, b_in, *, tm, rows_per_mod, d_conv, d_attn):
    t, d = x.shape
    n_in = w_in.shape[1]
    nt = t // tm
    per_token = rows_per_mod is None
    if per_token:
        mod_spec = pl.BlockSpec((6, tm, d), lambda i: (0, i, 0))
    else:
        mod_spec = pl.BlockSpec((6, 1, 1, d), lambda i: (0, (i * tm) // rows_per_mod, 0, 0))
    row = lambda n, dt: (jax.ShapeDtypeStruct((t, n), dt), pl.BlockSpec((tm, n), lambda i: (i, 0)))
    hd = 2 * ATT_HEAD_DIM
    heads = d_attn // hd
    head_rows = (jax.ShapeDtypeStruct((t * heads, hd), F32), pl.BlockSpec((tm * heads, hd), lambda i: (i, 0)))
    outs = [row(d_conv, F32), head_rows, head_rows]
    if per_token:
        outs.append(row(d_attn, F32))
    else:
        outs.append(row(d_attn, BF16))
        tr = (jax.ShapeDtypeStruct((nt, d_attn, tm), BF16), pl.BlockSpec((1, d_attn, tm), lambda i: (i, 0, 0)))
        outs += [tr, tr]
    return pl.pallas_call(
        functools.partial(_inproj_kernel, per_token_mod=per_token, d_conv=d_conv, d_attn=d_attn,
                          transposed=not per_token),
        grid=(nt,),
        in_specs=[pl.BlockSpec((tm, d), lambda i: (i, 0)), mod_spec, _full((d, n_in)), _full((1, n_in))],
        out_specs=[o[1] for o in outs],
        out_shape=[o[0] for o in outs],
        compiler_params=_params("arbitrary"),
        name="inproj_sample" if per_token else "inproj_prompt",
    )(x, mod, w_in, b_in)


CONV_PAD = 32


def _conv_post(acc, cb_ref, g_ref, b_ref):
    y = _ln(acc + cb_ref[...]) * g_ref[...] + b_ref[...]
    return y * jax.nn.sigmoid(y)


def _conv_prompt_kernel(glu_ref, cw_ref, cb_ref, g_ref, b_ref, y_ref, full_ref, sh_ref, *, ts, width, chunk):
    s = pl.program_id(1)
    hist = width - 1

    @pl.when(s == 0)
    def _():
        full_ref[0:CONV_PAD, :] = jnp.zeros((CONV_PAD, full_ref.shape[1]), F32)

    @pl.when(s > 0)
    def _():
        full_ref[0:CONV_PAD, :] = full_ref[ts:ts + CONV_PAD, :]

    full_ref[CONV_PAD:CONV_PAD + ts, :] = glu_ref[...]
    n_sh = sh_ref.shape[1]
    for r in range(1, SUBLANES):
        sh_ref[r - 1] = full_ref[r:r + n_sh, :]
    base = CONV_PAD - hist
    for c in range(ts // chunk):
        r0 = c * chunk
        acc = None
        for w in range(width):
            q, r = divmod(base + w, SUBLANES)
            rows = slice(r0 + SUBLANES * q, r0 + SUBLANES * q + chunk)
            tap = (full_ref[rows, :] if r == 0 else sh_ref[r - 1, rows, :]) * cw_ref[w:w + 1, :]
            acc = tap if acc is None else acc + tap
        y_ref[r0:r0 + chunk, :] = _conv_post(acc, cb_ref, g_ref, b_ref).astype(y_ref.dtype)


def _conv_prompt(glu, conv_w, conv_b, ln_g, ln_b, *, batch, seq, ts):
    t, dc = glu.shape
    width = conv_w.shape[0]
    ns = seq // ts
    vec = lambda a: a.reshape(1, dc)
    return pl.pallas_call(
        functools.partial(_conv_prompt_kernel, ts=ts, width=width, chunk=min(64, ts)),
        grid=(batch, ns),
        in_specs=[pl.BlockSpec((ts, dc), lambda b, s: (b * ns + s, 0)),
                  _full((width, dc)), _full((1, dc)), _full((1, dc)), _full((1, dc))],
        out_specs=pl.BlockSpec((ts, dc), lambda b, s: (b * ns + s, 0)),
        out_shape=jax.ShapeDtypeStruct((t, dc), BF16),
        scratch_shapes=[pltpu.VMEM((CONV_PAD + ts, dc), F32), pltpu.VMEM((SUBLANES - 1, CONV_PAD + ts - SUBLANES, dc), F32)],
        compiler_params=_params("arbitrary", "arbitrary"),
        name="conv_prompt",
    )(glu, conv_w, vec(conv_b), vec(ln_g), vec(ln_b))


def _conv_sample_kernel(st_ref, glu_ref, cw_ref, cb_ref, g_ref, b_ref, y_ref, *, width):
    hist = width - 1
    acc = glu_ref[...] * cw_ref[hist:hist + 1, :]
    for w in range(hist):
        acc = acc + st_ref[w] * cw_ref[w:w + 1, :]
    y_ref[...] = _conv_post(acc, cb_ref, g_ref, b_ref).astype(y_ref.dtype)


def _conv_sample(state_t, glu, conv_w, conv_b, ln_g, ln_b):
    hist, nb, dc = state_t.shape
    width = conv_w.shape[0]
    vec = lambda a: a.reshape(1, dc)
    return pl.pallas_call(
        functools.partial(_conv_sample_kernel, width=width),
        grid=(1,),
        in_specs=[_full((hist, nb, dc)), _full((nb, dc)), _full((width, dc)),
                  _full((1, dc)), _full((1, dc)), _full((1, dc))],
        out_specs=_full((nb, dc)),
        out_shape=jax.ShapeDtypeStruct((nb, dc), BF16),
        compiler_params=_params("arbitrary"),
        name="conv_sample",
    )(state_t, glu, conv_w, vec(conv_b), vec(ln_g), vec(ln_b))


def _bucket(n):
    max_exact = N_BUCKETS // 2
    nf = jnp.maximum(n, 1).astype(F32)
    large = max_exact + (jnp.log(nf / max_exact) / math.log(MAX_DISTANCE / max_exact)
                         * (N_BUCKETS - max_exact)).astype(jnp.int32)
    large = jnp.minimum(large, N_BUCKETS - 1)
    return jnp.where(n < max_exact, n, large)


def _bias_of(n, rb_ref, h):
    bucket = _bucket(n)
    far = jnp.full(n.shape, rb_ref[N_BUCKETS - 1, h], F32)
    out = far
    for j in range(N_BUCKETS - 1):
        out = jnp.where(bucket == j, rb_ref[j, h], out)
    return out - far


def _bias_kernel(rb_ref, o_ref, *, ts):
    h = pl.program_id(0)
    blk = LANES
    ik = lax.broadcasted_iota(jnp.int32, (blk, blk), 0)
    jq = lax.broadcasted_iota(jnp.int32, (blk, blk), 1)
    for d in range(2):
        for bi in range(ts // blk):
            for bj in range(ts // blk):
                off = (bj - bi) * blk + ts * (1 - d)
                if off + (blk - 1) < 0:
                    tile = jnp.full((blk, blk), NEG_INF, F32)
                elif off - (blk - 1) >= MAX_DISTANCE:
                    tile = jnp.zeros((blk, blk), F32)
                else:
                    n = jq - ik + off
                    tile = jnp.where(n >= 0, _bias_of(jnp.maximum(n, 0), rb_ref, h) * LOG2E, NEG_INF)
                o_ref[0, d, bi * blk:(bi + 1) * blk, bj * blk:(bj + 1) * blk] = tile


def _bias_tiles(rel_bias, *, heads, ts):
    return pl.pallas_call(
        functools.partial(_bias_kernel, ts=ts),
        grid=(heads,),
        in_specs=[pl.BlockSpec(memory_space=pltpu.SMEM)],
        out_specs=pl.BlockSpec((1, 2, ts, ts), lambda h: (h, 0, 0, 0)),
        out_shape=jax.ShapeDtypeStruct((heads, 2, ts, ts), F32),
        compiler_params=_params("arbitrary"),
        name="bias_tiles",
    )(rel_bias)


def _lambda(lq1, lk1, lq2, lk2, lambda_init):
    s1 = jnp.sum(lq1[...] * lk1[...], axis=-1, keepdims=True)
    s2 = jnp.sum(lq2[...] * lk2[...], axis=-1, keepdims=True)
    return jnp.exp(s1) - jnp.exp(s2) + lambda_init


def _attn_kernel(qt_ref, k_ref, vt_ref, bias_ref, g_ref, lq1, lk1, lq2, lk2, o_ref,
                 qs_ref, m_ref, l_ref, acc_ref, sa_ref, sb_ref, *, ts, lambda_init):
    qi = pl.program_id(2)
    d = ATT_HEAD_DIM
    qt = qt_ref[0]
    row = lax.broadcasted_iota(jnp.int32, qt.shape, 0)
    zero = jnp.zeros_like(qt)
    qs_ref[:, :ts] = jnp.where(row < d, qt, zero)
    qs_ref[:, ts:] = jnp.where(row >= d, qt, zero)
    m_ref[...] = jnp.full(m_ref.shape, NEG_INF, F32)
    l_ref[...] = jnp.zeros(l_ref.shape, F32)
    acc_ref[...] = jnp.zeros(acc_ref.shape, F32)

    def scores(ki, bias):
        kblk = k_ref[pl.ds(pl.multiple_of(ki * ts, ts), ts), :]
        s = jnp.dot(kblk, qs_ref[...], preferred_element_type=F32)
        if bias is not None:
            s = s + jnp.concatenate([bias, bias], axis=1)
        return s

    def update(s_ref, ki):
        s = s_ref[...]
        m_prev = m_ref[...]
        m_new = jnp.maximum(m_prev, jnp.max(s, axis=0, keepdims=True))
        alpha = jnp.exp2(m_prev - m_new)
        p = jnp.exp2(s - m_new)
        l_ref[...] = alpha * l_ref[...] + jnp.sum(p, axis=0, keepdims=True)
        acc_ref[...] = alpha * acc_ref[...] + jnp.dot(vt_ref[ki], p.astype(BF16), preferred_element_type=F32)
        m_ref[...] = m_new

    n_far = jnp.maximum(qi - 1, 0)
    sa_ref[...] = scores(qi, bias_ref[0, 1])

    @pl.when(qi > 0)
    def _():
        sb_ref[...] = scores(qi - 1, bias_ref[0, 0])
        update(sa_ref, qi)

    def far_pair(jj, carry):
        ka = qi - 2 - 2 * jj
        sa_ref[...] = scores(ka, None)
        update(sb_ref, ka + 1)
        sb_ref[...] = scores(ka - 1, None)
        update(sa_ref, ka)
        return carry

    lax.fori_loop(0, n_far // 2, far_pair, 0)

    @pl.when(n_far % 2 == 1)
    def _():
        sa_ref[...] = scores(0, None)
        update(sb_ref, 1)

    @pl.when(qi % 2 == 0)
    def _():
        update(sa_ref, 0)

    @pl.when(qi % 2 == 1)
    def _():
        update(sb_ref, 0)

    lam = _lambda(lq1, lk1, lq2, lk2, lambda_init)
    inv_l = 1.0 / l_ref[...]
    acc = acc_ref[...]
    o = acc[:, :ts] * inv_l[:, :ts] - lam * (acc[:, ts:] * inv_l[:, ts:])
    o = o * lax.rsqrt(jnp.mean(o * o, axis=0, keepdims=True) + LN_EPS)
    o = o * g_ref[...] * (1.0 - lambda_init)
    o_ref[...] = o.T.astype(o_ref.dtype)


def _attn_prompt(qt, kb, vt, bias, subln_g, lams, *, batch, seq, heads, ts, lambda_init):
    t, d_attn = kb.shape
    hd = 2 * ATT_HEAD_DIM
    nq = seq // ts
    lam_spec = _full((1, ATT_HEAD_DIM))
    return pl.pallas_call(
        functools.partial(_attn_kernel, ts=ts, lambda_init=lambda_init),
        grid=(batch, heads, nq),
        in_specs=[pl.BlockSpec((1, hd, ts), lambda b, h, q: (b * nq + q, h, 0)),
                  pl.BlockSpec((seq, hd), lambda b, h, q: (b, h)),
                  pl.BlockSpec((nq, hd, ts), lambda b, h, q: (b, h, 0)),
                  pl.BlockSpec((1, 2, ts, ts), lambda b, h, q: (h, 0, 0, 0)),
                  _full((hd, 1)), lam_spec, lam_spec, lam_spec, lam_spec],
        out_specs=pl.BlockSpec((ts, hd), lambda b, h, q: (b * nq + q, h)),
        out_shape=jax.ShapeDtypeStruct((t, d_attn), BF16),
        scratch_shapes=[pltpu.VMEM((hd, 2 * ts), BF16), pltpu.VMEM((1, 2 * ts), F32),
                        pltpu.VMEM((1, 2 * ts), F32), pltpu.VMEM((hd, 2 * ts), F32),
                        pltpu.VMEM((ts, 2 * ts), F32), pltpu.VMEM((ts, 2 * ts), F32)],
        compiler_params=_params("arbitrary", "arbitrary", "arbitrary"),
        name="attn_prompt",
    )(qt, kb, vt, bias, subln_g.reshape(hd, 1), *lams)


def _decode_kernel(pt_ref, rb_ref, q_ref, kn_ref, vn_ref, g_ref, lq1, lk1, lq2, lk2, *rest,
                   n_pages, page, heads, lambda_init):
    k_refs = rest[:n_pages]
    v_refs = rest[n_pages:2 * n_pages]
    o_ref = rest[2 * n_pages]
    s_ref, bias_ref = rest[2 * n_pages + 1:]
    d = ATT_HEAD_DIM
    hd = 2 * d
    rows = page * heads
    past = n_pages * rows
    nr = 2 * heads
    log_heads = int(math.log2(heads))
    nt = (((1,), (1,)), ((), ()))

    q4 = q_ref[0] * (d ** -0.5)
    lane = lax.broadcasted_iota(jnp.int32, (nr, hd), 1)
    top = lax.broadcasted_iota(jnp.int32, (nr, hd), 0) < heads
    q8 = jnp.concatenate([q4, q4], axis=0)
    q8 = jnp.where(top == (lane < d), q8, 0.0).astype(BF16)

    def own(n_cols):
        r = lax.broadcasted_iota(jnp.int32, (nr, n_cols), 0)
        c = lax.broadcasted_iota(jnp.int32, (nr, n_cols), 1)
        return (r & (heads - 1)) == (c & (heads - 1))

    @pl.when(pl.program_id(0) == 0)
    def _():
        key = lax.shift_right_logical(lax.broadcasted_iota(jnp.int32, (nr, rows), 1), log_heads)
        rh = lax.broadcasted_iota(jnp.int32, (nr, rows), 0) & (heads - 1)
        rh_tail = lax.broadcasted_iota(jnp.int32, (nr, LANES), 0) & (heads - 1)
        bias = jnp.zeros((nr, rows), F32)
        bias_new = jnp.zeros((nr, LANES), F32)
        for h in range(heads):
            bias = jnp.where(rh == h, _bias_of(page - key, rb_ref, h), bias)
            bias_new = jnp.where(rh_tail == h, _bias_of(jnp.zeros((nr, LANES), jnp.int32), rb_ref, h), bias_new)
        bias_ref[:, 0:rows] = bias
        bias_ref[:, rows:rows + LANES] = bias_new

    own_page = own(rows)
    for j in range(n_pages):
        sj = lax.dot_general(q8, k_refs[j][0].astype(BF16), nt, preferred_element_type=F32)
        if j == n_pages - 1:
            sj = sj + bias_ref[:, 0:rows]
        s_ref[:, j * rows:(j + 1) * rows] = jnp.where(own_page, sj, NEG_INF)
    kn = jnp.concatenate([kn_ref[0], jnp.zeros((LANES - heads, hd), F32)], axis=0).astype(BF16)
    s_new = lax.dot_general(q8, kn, nt, preferred_element_type=F32)
    tail_col = lax.broadcasted_iota(jnp.int32, (nr, LANES), 1)
    s_ref[:, past:past + LANES] = jnp.where(own(LANES) & (tail_col < heads),
                                            s_new + bias_ref[:, rows:rows + LANES], NEG_INF)

    s = s_ref[...]
    m = jnp.max(s, axis=-1, keepdims=True)
    p = jnp.exp(s - m)
    l = jnp.sum(p, axis=-1, keepdims=True)
    pb = p.astype(BF16)
    out = jnp.zeros((nr, hd), F32)
    for j in range(n_pages):
        out = out + jnp.dot(pb[:, j * rows:(j + 1) * rows], v_refs[j][0].astype(BF16), preferred_element_type=F32)
    vn = jnp.concatenate([vn_ref[0], jnp.zeros((LANES - heads, hd), F32)], axis=0).astype(BF16)
    out = (out + jnp.dot(pb[:, past:past + LANES], vn, preferred_element_type=F32)) / l
    lam = _lambda(lq1, lk1, lq2, lk2, lambda_init)
    o = out[0:heads, :] - lam * out[heads:nr, :]
    o = o * lax.rsqrt(jnp.mean(o * o, axis=-1, keepdims=True) + LN_EPS)
    o_ref[0] = (o * g_ref[...] * (1.0 - lambda_init)).astype(o_ref.dtype)


def _attn_sample(page_table, rel_bias, q, k_new, v_new, cache_k, cache_v, subln_g, lams, *, heads, lambda_init):
    nb, n_pages = page_table.shape
    n_pool, rows, hd = cache_k.shape
    page = rows // heads
    assert heads & (heads - 1) == 0 and 2 * heads <= 8
    tok = lambda a: a.reshape(nb, heads, hd)
    tok_spec = pl.BlockSpec((1, heads, hd), lambda b, pt: (b, 0, 0))
    lam_spec = pl.BlockSpec((1, ATT_HEAD_DIM), lambda b, pt: (0, 0))
    page_specs = [pl.BlockSpec((1, rows, hd), lambda b, pt, j=j: (pt[b, j], 0, 0)) for j in range(n_pages)]
    grid_spec = pltpu.PrefetchScalarGridSpec(
        num_scalar_prefetch=1,
        grid=(nb,),
        in_specs=[pl.BlockSpec(memory_space=pltpu.SMEM), tok_spec, tok_spec, tok_spec,
                  pl.BlockSpec((1, hd), lambda b, pt: (0, 0)),
                  lam_spec, lam_spec, lam_spec, lam_spec] + page_specs + page_specs,
        out_specs=tok_spec,
        scratch_shapes=[pltpu.VMEM((2 * heads, n_pages * rows + LANES), F32),
                        pltpu.VMEM((2 * heads, rows + LANES), F32)],
    )
    out = pl.pallas_call(
        functools.partial(_decode_kernel, n_pages=n_pages, page=page, heads=heads, lambda_init=lambda_init),
        grid_spec=grid_spec,
        out_shape=jax.ShapeDtypeStruct((nb, heads, hd), BF16),
        compiler_params=_params("arbitrary"),
        name="attn_sample",
    )(page_table, rel_bias, tok(q), tok(k_new), tok(v_new), subln_g.reshape(1, hd), *lams,
      *([cache_k] * n_pages), *([cache_v] * n_pages))
    return out.reshape(nb, heads * hd)


def _mix_kernel(x_ref, cy_ref, ay_ref, mod_ref, wo_ref, bo_ref, g1_ref, b1_ref, wq_ref, sk_ref,
                x1_ref, h2t_ref, st_ref, *, per_token_mod, alpha, d_conv):
    if per_token_mod:
        gate1, sh2, sc2 = mod_ref[2], mod_ref[3], mod_ref[4]
    else:
        gate1, sh2, sc2 = mod_ref[2, 0], mod_ref[3, 0], mod_ref[4, 0]
    mix = (jnp.dot(cy_ref[...], wo_ref[:d_conv, :], preferred_element_type=F32)
           + jnp.dot(ay_ref[...], wo_ref[d_conv:, :], preferred_element_type=F32) + bo_ref[...])
    x1 = _ln(alpha * x_ref[...] + gate1 * mix) * g1_ref[...] + b1_ref[...]
    x1_ref[...] = x1
    h2 = _ln(x1) * (1.0 + sc2) + sh2
    h2b = h2.astype(BF16)
    h2t_ref[...] = h2.T.astype(BF16)
    qh = jnp.dot(h2b, wq_ref[...], preferred_element_type=F32).astype(BF16)
    nk = sk_ref.shape[2]
    for hc in range(sk_ref.shape[0]):
        st_ref[hc] = lax.dot_general(sk_ref[hc], qh[:, hc * nk:(hc + 1) * nk], (((1,), (1,)), ((), ())),
                                     preferred_element_type=F32)


def _mix(x, cy, ay, mod, w_out, b_out, ln_g, ln_b, w_query, sub_keys, *, tm, rows_per_mod, alpha):
    t, d = x.shape
    d_conv = cy.shape[1]
    d_attn = ay.shape[1]
    n_hc, n_keys, half = sub_keys.shape
    per_token = rows_per_mod is None
    if per_token:
        mod_spec = pl.BlockSpec((6, tm, d), lambda i: (0, i, 0))
    else:
        mod_spec = pl.BlockSpec((6, 1, 1, d), lambda i: (0, (i * tm) // rows_per_mod, 0, 0))
    vec = lambda a: a.reshape(1, d)
    return pl.pallas_call(
        functools.partial(_mix_kernel, per_token_mod=per_token, alpha=alpha, d_conv=d_conv),
        grid=(t // tm,),
        in_specs=[pl.BlockSpec((tm, d), lambda i: (i, 0)),
                  pl.BlockSpec((tm, d_conv), lambda i: (i, 0)),
                  pl.BlockSpec((tm, d_attn), lambda i: (i, 0)),
                  mod_spec, _full(w_out.shape), _full((1, d)), _full((1, d)), _full((1, d)),
                  _full(w_query.shape), _full(sub_keys.shape)],
        out_specs=[pl.BlockSpec((tm, d), lambda i: (i, 0)),
                   pl.BlockSpec((d, tm), lambda i: (0, i)),
                   pl.BlockSpec((n_hc, n_keys, tm), lambda i: (0, 0, i))],
        out_shape=[jax.ShapeDtypeStruct((t, d), F32),
                   jax.ShapeDtypeStruct((d, t), BF16),
                   jax.ShapeDtypeStruct((n_hc, n_keys, t), F32)],
        compiler_params=_params("arbitrary"),
        name="mix_sample" if per_token else "mix_prompt",
    )(x, cy, ay, mod, w_out, vec(b_out), vec(ln_g), vec(ln_b), w_query, sub_keys)


def _top_values(s, with_rank):
    work = s
    rank = jnp.full(s.shape, float(PEER_TOPK), F32) if with_rank else None
    vals = []
    for r in range(PEER_TOPK):
        m = jnp.max(work, axis=0, keepdims=True)
        hit = work == m
        if with_rank:
            rank = jnp.where(hit, float(r), rank)
        work = jnp.where(hit, -jnp.inf, work)
        vals.append(m)
    return rank, vals


def _route_kernel(s_ref, r2_ref, e2_ref, n1_ref, e1_ref):
    col = min(ROUTE_COLUMN, s_ref.shape[2])
    for c in range(s_ref.shape[2] // col):
        tok = slice(c * col, (c + 1) * col)
        _route_column(s_ref[0, :, tok], s_ref[1, :, tok], tok, r2_ref, e2_ref, n1_ref, e1_ref)


def _route_column(s1, s2, tok, r2_ref, e2_ref, n1_ref, e1_ref):
    k = PEER_TOPK
    _, v1 = _top_values(s1, with_rank=False)
    rank2, v2 = _top_values(s2, with_rank=True)
    rowk = lax.broadcasted_iota(jnp.int32, (k,) + s1.shape[1:], 0)

    def stack(vals):
        out = jnp.zeros(rowk.shape, F32)
        for r in range(k):
            out = jnp.where(rowk == r, vals[r], out)
        return out

    v1m = stack(v1)
    v2m = stack(v2)
    half = k // 2
    row_half = lax.broadcasted_iota(jnp.int32, (half,) + s1.shape[1:], 0)
    cands = [v1[0] + v2m]
    for a in range(1, half):
        cands.append(jnp.where(row_half < k // (a + 1), v1[a] + v2m[0:half], -jnp.inf))
    cands.append(v1m[half:k] + v2[0])
    work = jnp.concatenate(cands, axis=0)
    thr = None
    for _ in range(k):
        thr = jnp.max(work, axis=0, keepdims=True)
        work = jnp.where(work == thr, -jnp.inf, work)
    e2top = jnp.exp(v2m - v2[0])
    z = jnp.zeros_like(thr)
    n1 = jnp.zeros(s1.shape, F32)
    for a in range(k):
        sel = (v1[a] + v2m) >= thr
        cnt = jnp.sum(jnp.where(sel, 1.0, 0.0), axis=0, keepdims=True)
        z = z + jnp.exp(v1[a] - v1[0]) * jnp.sum(jnp.where(sel, e2top, 0.0), axis=0, keepdims=True)
        n1 = jnp.where(s1 == v1[a], cnt, n1)
    e1 = jnp.where(s1 >= v1[k - 1], jnp.exp(s1 - v1[0]) / z, 0.0)
    e2 = jnp.where(rank2 < float(k), jnp.exp(s2 - v2[0]), 0.0)
    r2_ref[0, :, tok] = rank2.astype(r2_ref.dtype)
    e2_ref[0, :, tok] = e2.astype(e2_ref.dtype)
    n1_ref[0, :, tok] = _bf16_pair(n1)
    e1_ref[0, :, tok] = _bf16_pair(e1)


def _bf16_pair(x):
    bits = pltpu.bitcast(x.astype(BF16).astype(F32), jnp.uint32)
    return bits | lax.shift_right_logical(bits, jnp.uint32(16))


def _route(st, *, tl):
    n_hc, n_keys, t = st.shape
    heads = n_hc // 2
    out = lambda dt: jax.ShapeDtypeStruct((heads, n_keys, t), dt)
    spec = pl.BlockSpec((1, n_keys, tl), lambda i, h: (h, 0, i))
    return pl.pallas_call(
        _route_kernel,
        grid=(t // tl, heads),
        in_specs=[pl.BlockSpec((2, n_keys, tl), lambda i, h: (h, 0, i))],
        out_specs=[spec] * 4,
        out_shape=[out(BF16), out(BF16), out(jnp.uint32), out(jnp.uint32)],
        compiler_params=_params("arbitrary", "arbitrary"),
        name="route",
    )(st)


GELU_C1 = math.sqrt(2.0 / math.pi)
GELU_C2 = GELU_C1 * 0.044715


def _gelu_tanh(x):
    hx = 0.5 * x
    return hx + hx * jnp.tanh(x * (GELU_C1 + GELU_C2 * (x * x)))


def _packed_rows(words):
    return pltpu.bitcast(jnp.broadcast_to(words, (SUBLANES, words.shape[1])), BF16)


def _peer_kernel(h2t_ref, u_ref, vt_ref, r2_ref, e2_ref, n1_ref, e1_ref, x1_ref, mod_ref, g_ref, b_ref, y_ref,
                 at0_ref, at1_ref, zt0_ref, zt1_ref, acc_ref, h2s_ref, *, per_token_mod, alpha, n_keys, slab, n_blocks):
    t = pl.program_id(1)
    ek, tq = at0_ref.shape
    heads = r2_ref.shape[0]
    sub = PACKED_ROWS
    col = min(PEER_MASK_LANES, tq)
    groups = slab // n_keys
    n_slabs = ek // slab

    def hidden(s, at_ref):
        rows = pl.ds(pl.multiple_of(s * slab, slab), slab)
        u = pltpu.bitcast(u_ref[pl.ds(pl.multiple_of(s * (slab // 2), slab // 2), slab // 2), :], BF16)
        at_ref[rows, :] = jnp.dot(u, h2s_ref[...], preferred_element_type=F32)

    def mask(s, at_ref, zt_ref):
        base = pl.multiple_of(s * slab, slab)
        for g in range(groups):
            i1 = (t - 1) * (ek // n_keys) + s * groups + g
            n1w = [n1_ref[h, pl.ds(i1, 1), :] for h in range(heads)]
            e1w = [e1_ref[h, pl.ds(i1, 1), :] for h in range(heads)]
            for c in range(tq // col):
                tok = slice(c * col, (c + 1) * col)
                n1b = [_packed_rows(n1w[h][:, tok]) for h in range(heads)]
                e1b = [_packed_rows(e1w[h][:, tok]) for h in range(heads)]
                for k in range(n_keys // sub):
                    rows = pl.ds(base + (g * n_keys + k * sub), sub)
                    keys = slice(k * sub, (k + 1) * sub)
                    act = _gelu_tanh(at_ref[rows, tok].astype(BF16))
                    w = jnp.zeros((sub, col), BF16)
                    for h in range(heads):
                        w = jnp.where(r2_ref[h, keys, tok] < n1b[h], w + e2_ref[h, keys, tok] * e1b[h], w)
                    zt_ref[rows, tok] = w * act

    def project(piece, zt_ref):
        rows = pl.ds(pl.multiple_of(piece * slab, slab), slab)
        vt = pltpu.bitcast(vt_ref[pl.ds(pl.multiple_of(piece * (slab // 2), slab // 2), slab // 2), :], BF16)
        acc_ref[rows, :] += jnp.dot(vt, zt_ref[...], preferred_element_type=F32)

    n_pieces = acc_ref.shape[0] // slab
    per_iter = n_pieces // n_slabs

    def step(parity, with_hidden=True, with_project=True):
        at = (at0_ref, at1_ref)
        zt = (zt0_ref, zt1_ref)
        cur, other = parity, 1 - parity
        for s in range(n_slabs):
            if with_hidden:
                hidden(s, at[other])
            mask(s, at[cur], zt[cur])
            if with_project:
                for i in range(per_iter):
                    project(s * per_iter + i, zt[other])

    @pl.when(t == 0)
    def _():
        acc_ref[...] = jnp.zeros(acc_ref.shape, F32)
        h2s_ref[...] = h2t_ref[...]

        def body(s, carry):
            hidden(s, at0_ref)
            return carry

        lax.fori_loop(0, n_slabs, body, 0)

    @pl.when(t == 1)
    def _():
        step(0, with_project=False)

    @pl.when((t >= 2) & (t < n_blocks) & (t % 2 == 1))
    def _():
        step(0)

    @pl.when((t >= 2) & (t < n_blocks) & (t % 2 == 0))
    def _():
        step(1)

    @pl.when(t == n_blocks)
    def _():
        step((n_blocks - 1) % 2, with_hidden=False)

    @pl.when(t == n_blocks + 1)
    def _():
        last = zt0_ref if (n_blocks - 1) % 2 == 0 else zt1_ref

        def body(p, carry):
            project(p, last)
            return carry

        lax.fori_loop(0, n_pieces, body, 0)
        gate2 = mod_ref[5] if per_token_mod else mod_ref[5, 0]
        ff = acc_ref[...].T
        y_ref[...] = _ln(alpha * x1_ref[...] + gate2 * ff) * g_ref[...] + b_ref[...]


def _peer(h2t, u_b, vt_b, r2, e2, n1, e1, x1, mod, ln_g, ln_b, *, tq, ek, slab, rows_per_mod, alpha):
    d, t = h2t.shape
    ne = 2 * u_b.shape[0]
    heads, n_keys, _ = r2.shape
    n_blocks = ne // ek
    assert n_blocks >= 2
    per_token = rows_per_mod is None
    if per_token:
        mod_spec = pl.BlockSpec((6, tq, d), lambda i, j: (0, i, 0))
    else:
        mod_spec = pl.BlockSpec((6, 1, 1, d), lambda i, j: (0, (i * tq) // rows_per_mod, 0, 0))
    tab = pl.BlockSpec((heads, n_keys, tq), lambda i, j: (0, 0, i))
    vec = lambda a: a.reshape(1, d)
    clamp = lambda b: jnp.clip(b, 0, n_blocks - 1)
    return pl.pallas_call(
        functools.partial(_peer_kernel, per_token_mod=per_token, alpha=alpha, n_keys=n_keys, slab=slab,
                          n_blocks=n_blocks),
        grid=(t // tq, n_blocks + 2),
        in_specs=[pl.BlockSpec((d, tq), lambda i, j: (0, i)),
                  pl.BlockSpec((ek // 2, d), lambda i, j: (clamp(j), 0)),
                  pl.BlockSpec((d // 2, ek), lambda i, j: (0, clamp(j - 2))),
                  tab, tab, tab, tab,
                  pl.BlockSpec((tq, d), lambda i, j: (i, 0)),
                  mod_spec, _full((1, d)), _full((1, d))],
        out_specs=pl.BlockSpec((tq, d), lambda i, j: (i, 0)),
        out_shape=jax.ShapeDtypeStruct((t, d), F32),
        scratch_shapes=[pltpu.VMEM((ek, tq), F32), pltpu.VMEM((ek, tq), F32),
                        pltpu.VMEM((ek, tq), BF16), pltpu.VMEM((ek, tq), BF16), pltpu.VMEM((d, tq), F32),
                        pltpu.VMEM((d, tq), BF16)],
        compiler_params=_params("arbitrary", "arbitrary"),
        name="peer_sample" if per_token else "peer_prompt",
    )(h2t, u_b, vt_b, r2, e2, n1, e1, x1, mod, vec(ln_g), vec(ln_b))


def _pack_kernel(x_ref, o_ref, *, transpose):
    x = x_ref[...]
    if transpose:
        x = x.T
    o_ref[...] = pltpu.bitcast(x.astype(BF16), jnp.uint32)


def _pack_bf16(x, *, rows, transpose):
    n, d = x.shape
    if transpose:
        out_spec, out_shape = pl.BlockSpec((d // 2, rows), lambda i: (0, i)), (d // 2, n)
    else:
        out_spec, out_shape = pl.BlockSpec((rows // 2, d), lambda i: (i, 0)), (n // 2, d)
    return pl.pallas_call(
        functools.partial(_pack_kernel, transpose=transpose),
        grid=(n // rows,),
        in_specs=[pl.BlockSpec((rows, d), lambda i: (i, 0))],
        out_specs=out_spec,
        out_shape=jax.ShapeDtypeStruct(out_shape, jnp.uint32),
        compiler_params=_params("arbitrary"),
        name="pack_vt" if transpose else "pack_u",
    )(x)


def kernel(x_prompt, x_sample, cache_k, cache_v, state_conv, page_table, c_prompt, c_sample, w_ada, b_ada, w_in, b_in, conv_w, conv_b, conv_ln_g, conv_ln_b, lambda_q1, lambda_k1, lambda_q2, lambda_k2, attn_subln_g, rel_bias, w_out, b_out, ln1_g, ln1_b, peer_w_query, peer_sub_keys, peer_u, peer_v, ln2_g, ln2_b):
    batch, seq, d = x_prompt.shape
    nb, dec_seq, _ = x_sample.shape
    depth = w_ada.shape[0]
    assert depth == 1 and dec_seq == 1
    heads = cache_k.shape[3]
    hd = cache_k.shape[4]
    assert hd == 2 * ATT_HEAD_DIM
    d_attn = heads * hd
    d_conv = conv_w.shape[2]
    hist = conv_w.shape[1] - 1
    assert hist <= CONV_PAD - 1
    page = cache_k.shape[2]
    n_keys = peer_sub_keys.shape[3]
    ne = peer_u.shape[1]
    alpha = (2 * depth) ** 0.25
    lambda_init = 0.8 - 0.6 * math.exp(-0.3 * 0)
    t_p = batch * seq
    ts = min(TOKEN_TILE, seq)
    tq_peer = min(PEER_TOKEN_TILE, seq)
    ek = min(PEER_EXPERT_TILE, ne)
    assert seq % ts == 0 and nb % LANES == 0 and page >= MAX_DISTANCE

    l = 0
    n_c = batch + nb
    n_c_pad = -(-n_c // SUBLANES) * SUBLANES
    c_all = jnp.concatenate([c_prompt, c_sample, jnp.zeros((n_c_pad - n_c, d), F32)], axis=0)
    mod = _ada(c_all, w_ada[l], b_ada[l])
    mod_p = mod[:, :batch].reshape(6, batch, 1, d)
    mod_s = mod[:, batch:n_c]

    w_in_b = w_in[l].astype(BF16)
    b_in_r = b_in[l].reshape(1, -1)
    w_out_b = w_out[l].astype(BF16)
    wq_b = peer_w_query[l].astype(BF16)
    sk_b = peer_sub_keys[l].reshape(-1, n_keys, peer_sub_keys.shape[-1]).astype(BF16)
    u_b = _pack_bf16(peer_u[l], rows=min(512, ne), transpose=False)
    vt_b = _pack_bf16(peer_v[l], rows=min(512, ne), transpose=True)
    lams = [a[l].reshape(1, -1) for a in (lambda_q1, lambda_k1, lambda_q2, lambda_k2)]
    subln_g = attn_subln_g[l]

    xp = x_prompt.reshape(t_p, d)
    glu_p, k_p, v_p, kb_p, qt_p, vt_p = _inproj(xp, mod_p, w_in_b, b_in_r, tm=ts, rows_per_mod=seq,
                                                d_conv=d_conv, d_attn=d_attn)
    cy_p = _conv_prompt(glu_p, conv_w[l], conv_b[l], conv_ln_g[l], conv_ln_b[l], batch=batch, seq=seq, ts=ts)
    bias = _bias_tiles(rel_bias, heads=heads, ts=ts)
    ay_p = _attn_prompt(qt_p, kb_p, vt_p, bias, subln_g, lams, batch=batch, seq=seq, heads=heads, ts=ts,
                        lambda_init=lambda_init)
    x1_p, h2t_p, st_p = _mix(xp, cy_p, ay_p, mod_p, w_out_b, b_out[l], ln1_g[l], ln1_b[l], wq_b, sk_b,
                             tm=ts, rows_per_mod=seq, alpha=alpha)
    r2, e2, n1, e1 = _route(st_p, tl=min(ROUTE_TOKEN_TILE, t_p))
    y_p = _peer(h2t_p, u_b, vt_b, r2, e2, n1, e1, x1_p, mod_p, ln2_g[l], ln2_b[l],
                tq=tq_peer, ek=ek, slab=min(PEER_EXPERT_SLAB, ek), rows_per_mod=seq, alpha=alpha)

    xs = x_sample.reshape(nb, d)
    glu_s, k_s, v_s, q_s = _inproj(xs, mod_s, w_in_b, b_in_r, tm=nb, rows_per_mod=None,
                                   d_conv=d_conv, d_attn=d_attn)
    state = state_conv[l]
    cy_s = _conv_sample(jnp.swapaxes(state, 0, 1), glu_s, conv_w[l], conv_b[l], conv_ln_g[l], conv_ln_b[l])
    n_pool = cache_k.shape[1]
    pool_rows = lambda a: a.reshape(depth * n_pool, page * heads, hd)
    ay_s = _attn_sample(page_table + l * n_pool, rel_bias, q_s, k_s, v_s, pool_rows(cache_k), pool_rows(cache_v),
                        subln_g, lams, heads=heads, lambda_init=lambda_init)
    x1_s, h2t_s, st_s = _mix(xs, cy_s, ay_s, mod_s, w_out_b, b_out[l], ln1_g[l], ln1_b[l], wq_b, sk_b,
                             tm=nb, rows_per_mod=None, alpha=alpha)
    r2s, e2s, n1s, e1s = _route(st_s, tl=nb)
    y_s = _peer(h2t_s, u_b, vt_b, r2s, e2s, n1s, e1s, x1_s, mod_s, ln2_g[l], ln2_b[l],
                tq=nb, ek=ek, slab=min(PEER_EXPERT_SLAB, ek), rows_per_mod=None, alpha=alpha)

    kv_p = (depth, batch, seq, heads, hd)
    kv_s = (depth, nb, dec_seq, heads, hd)
    conv_p = glu_p.reshape(batch, seq, d_conv)[:, seq - hist:][None]
    conv_s = jnp.concatenate([state[:, 1:], glu_s[:, None, :]], axis=1)[None]
    return (y_p.reshape(batch, seq, d), y_s.reshape(nb, dec_seq, d),
            k_p.reshape(kv_p), v_p.reshape(kv_p), conv_p,
            k_s.reshape(kv_s), v_s.reshape(kv_s), conv_s)
```

```python
import functools
import math

import jax
import jax.numpy as jnp
from jax import lax
from jax.experimental import pallas as pl
from jax.experimental.pallas import tpu as pltpu

F32 = jnp.float32
BF16 = jnp.bfloat16

LN_EPS = 1e-5
NEG_INF = -1e30
ATT_HEAD_DIM = 64
N_BUCKETS = 32
MAX_DISTANCE = 128
PEER_TOPK = 16
LOG2E = math.log2(math.e)
LANES = 128
SUBLANES = 8
PACKED_ROWS = 16
VMEM_LIMIT_BYTES = 56 * 1024 * 1024
TOKEN_TILE = 1024
PEER_TOKEN_TILE = 512
PEER_EXPERT_TILE = 1024
PEER_EXPERT_SLAB = 256
PEER_MASK_LANES = 256
ROUTE_TOKEN_TILE = 256


def _params(*sem):
    return pltpu.CompilerParams(dimension_semantics=sem, vmem_limit_bytes=VMEM_LIMIT_BYTES)


def _ln(x):
    mu = jnp.mean(x, axis=-1, keepdims=True)
    xc = x - mu
    var = jnp.mean(xc * xc, axis=-1, keepdims=True)
    return xc * lax.rsqrt(var + LN_EPS)


def _full(shape):
    return pl.BlockSpec(shape, lambda *_: (0,) * len(shape), pipeline_mode=pl.Buffered(1))


def _ada_kernel(c_ref, w_ref, b_ref, o_ref):
    c = c_ref[...]
    s = c * jax.nn.sigmoid(c)
    o_ref[0] = jnp.dot(s.astype(BF16), w_ref[...].astype(BF16), preferred_element_type=F32) + b_ref[0]


def _ada(c_all, w_ada, b_ada):
    nc, d = c_all.shape
    return pl.pallas_call(
        _ada_kernel,
        grid=(6,),
        in_specs=[_full((nc, d)),
                  pl.BlockSpec((d, d), lambda k: (0, k)),
                  pl.BlockSpec((1, 1, d), lambda k: (k, 0, 0))],
        out_specs=pl.BlockSpec((1, nc, d), lambda k: (k, 0, 0)),
        out_shape=jax.ShapeDtypeStruct((6, nc, d), F32),
        compiler_params=_params("arbitrary"),
        name="ada",
    )(c_all, w_ada, b_ada.reshape(6, 1, d))


def _inproj_kernel(x_ref, mod_ref, w_ref, b_ref, *outs, per_token_mod, d_conv, d_attn, transposed):
    x = x_ref[...]
    if per_token_mod:
        sh1, sc1 = mod_ref[0], mod_ref[1]
    else:
        sh1, sc1 = mod_ref[0, 0], mod_ref[1, 0]
    h = _ln(x) * (1.0 + sc1) + sh1
    z = jnp.dot(h.astype(BF16), w_ref[...], preferred_element_type=F32) + b_ref[...]
    ga = z[:, :d_conv]
    gb = z[:, d_conv:2 * d_conv]
    o = 2 * d_conv
    q = z[:, o:o + d_attn]
    k = z[:, o + d_attn:o + 2 * d_attn]
    v = z[:, o + 2 * d_attn:o + 3 * d_attn]
    glu = ga * jax.nn.sigmoid(gb)
    if transposed:
        glu_ref, k_ref, v_ref, kb_ref, qt_ref, vt_ref = outs
        kb_ref[...] = k.astype(BF16)
        qt_ref[0] = (q * (ATT_HEAD_DIM ** -0.5 * LOG2E)).T.astype(BF16)
        vt_ref[0] = v.T.astype(BF16)
    else:
        glu_ref, k_ref, v_ref, q_ref = outs
        q_ref[...] = q
    glu_ref[...] = glu
    heads = d_attn // (2 * ATT_HEAD_DIM)
    for h in range(heads):
        cols = slice(h * 2 * ATT_HEAD_DIM, (h + 1) * 2 * ATT_HEAD_DIM)
        k_ref[pl.ds(h, x.shape[0], stride=heads), :] = k[:, cols]
        v_ref[pl.ds(h, x.shape[0], stride=heads), :] = v[:, cols]


def _inproj(x, mod, w_in, b_in, *, tm, rows_per_mod, d_conv, d_attn):
    t, d = x.shape
    n_in = w_in.shape[1]
    nt = t // tm
    per_token = rows_per_mod is None
    if per_token:
        mod_spec = pl.BlockSpec((6, tm, d), lambda i: (0, i, 0))
    else:
        mod_spec = pl.BlockSpec((6, 1, 1, d), lambda i: (0, (i * tm) // rows_per_mod, 0, 0))
    row = lambda n, dt: (jax.ShapeDtypeStruct((t, n), dt), pl.BlockSpec((tm, n), lambda i: (i, 0)))
    hd = 2 * ATT_HEAD_DIM
    heads = d_attn // hd
    head_rows = (jax.ShapeDtypeStruct((t * heads, hd), F32), pl.BlockSpec((tm * heads, hd), lambda i: (i, 0)))
    outs = [row(d_conv, F32), head_rows, head_rows]
    if per_token:
        outs.append(row(d_attn, F32))
    else:
        outs.append(row(d_attn, BF16))
        tr = (jax.ShapeDtypeStruct((nt, d_attn, tm), BF16), pl.BlockSpec((1, d_attn, tm), lambda i: (i, 0, 0)))
        outs += [tr, tr]
    return pl.pallas_call(
        functools.partial(_inproj_kernel, per_token_mod=per_token, d_conv=d_conv, d_attn=d_attn,
                          transposed=not per_token),
        grid=(nt,),
        in_specs=[pl.BlockSpec((tm, d), lambda i: (i, 0)), mod_spec, _full((d, n_in)), _full((1, n_in))],
        out_specs=[o[1] for o in outs],
        out_shape=[o[0] for o in outs],
        compiler_params=_params("arbitrary"),
        name="inproj_sample" if per_token else "inproj_prompt",
    )(x, mod, w_in, b_in)


CONV_PAD = 32


def _conv_post(acc, cb_ref, g_ref, b_ref):
    y = _ln(acc + cb_ref[...]) * g_ref[...] + b_ref[...]
    return y * jax.nn.sigmoid(y)


def _conv_prompt_kernel(glu_ref, cw_ref, cb_ref, g_ref, b_ref, y_ref, full_ref, sh_ref, *, ts, width, chunk):
    s = pl.program_id(1)
    hist = width - 1

    @pl.when(s == 0)
    def _():
        full_ref[0:CONV_PAD, :] = jnp.zeros((CONV_PAD, full_ref.shape[1]), F32)

    @pl.when(s > 0)
    def _():
        full_ref[0:CONV_PAD, :] = full_ref[ts:ts + CONV_PAD, :]

    full_ref[CONV_PAD:CONV_PAD + ts, :] = glu_ref[...]
    n_sh = sh_ref.shape[1]
    for r in range(1, SUBLANES):
        sh_ref[r - 1] = full_ref[r:r + n_sh, :]
    base = CONV_PAD - hist
    for c in range(ts // chunk):
        r0 = c * chunk
        acc = None
        for w in range(width):
            q, r = divmod(base + w, SUBLANES)
            rows = slice(r0 + SUBLANES * q, r0 + SUBLANES * q + chunk)
            tap = (full_ref[rows, :] if r == 0 else sh_ref[r - 1, rows, :]) * cw_ref[w:w + 1, :]
            acc = tap if acc is None else acc + tap
        y_ref[r0:r0 + chunk, :] = _conv_post(acc, cb_ref, g_ref, b_ref).astype(y_ref.dtype)


def _conv_prompt(glu, conv_w, conv_b, ln_g, ln_b, *, batch, seq, ts):
    t, dc = glu.shape
    width = conv_w.shape[0]
    ns = seq // ts
    vec = lambda a: a.reshape(1, dc)
    return pl.pallas_call(
        functools.partial(_conv_prompt_kernel, ts=ts, width=width, chunk=min(64, ts)),
        grid=(batch, ns),
        in_specs=[pl.BlockSpec((ts, dc), lambda b, s: (b * ns + s, 0)),
                  _full((width, dc)), _full((1, dc)), _full((1, dc)), _full((1, dc))],
        out_specs=pl.BlockSpec((ts, dc), lambda b, s: (b * ns + s, 0)),
        out_shape=jax.ShapeDtypeStruct((t, dc), BF16),
        scratch_shapes=[pltpu.VMEM((CONV_PAD + ts, dc), F32), pltpu.VMEM((SUBLANES - 1, CONV_PAD + ts - SUBLANES, dc), F32)],
        compiler_params=_params("arbitrary", "arbitrary"),
        name="conv_prompt",
    )(glu, conv_w, vec(conv_b), vec(ln_g), vec(ln_b))


def _conv_sample_kernel(st_ref, glu_ref, cw_ref, cb_ref, g_ref, b_ref, y_ref, *, width):
    hist = width - 1
    acc = glu_ref[...] * cw_ref[hist:hist + 1, :]
    for w in range(hist):
        acc = acc + st_ref[w] * cw_ref[w:w + 1, :]
    y_ref[...] = _conv_post(acc, cb_ref, g_ref, b_ref).astype(y_ref.dtype)


def _conv_sample(state_t, glu, conv_w, conv_b, ln_g, ln_b):
    hist, nb, dc = state_t.shape
    width = conv_w.shape[0]
    vec = lambda a: a.reshape(1, dc)
    return pl.pallas_call(
        functools.partial(_conv_sample_kernel, width=width),
        grid=(1,),
        in_specs=[_full((hist, nb, dc)), _full((nb, dc)), _full((width, dc)),
                  _full((1, dc)), _full((1, dc)), _full((1, dc))],
        out_specs=_full((nb, dc)),
        out_shape=jax.ShapeDtypeStruct((nb, dc), BF16),
        compiler_params=_params("arbitrary"),
        name="conv_sample",
    )(state_t, glu, conv_w, vec(conv_b), vec(ln_g), vec(ln_b))


def _bucket(n):
    max_exact = N_BUCKETS // 2
    nf = jnp.maximum(n, 1).astype(F32)
    large = max_exact + (jnp.log(nf / max_exact) / math.log(MAX_DISTANCE / max_exact)
                         * (N_BUCKETS - max_exact)).astype(jnp.int32)
    large = jnp.minimum(large, N_BUCKETS - 1)
    return jnp.where(n < max_exact, n, large)


def _bias_of(n, rb_ref, h):
    bucket = _bucket(n)
    far = jnp.full(n.shape, rb_ref[N_BUCKETS - 1, h], F32)
    out = far
    for j in range(N_BUCKETS - 1):
        out = jnp.where(bucket == j, rb_ref[j, h], out)
    return out - far


def _bias_kernel(rb_ref, o_ref, *, ts):
    h = pl.program_id(0)
    blk = LANES
    ik = lax.broadcasted_iota(jnp.int32, (blk, blk), 0)
    jq = lax.broadcasted_iota(jnp.int32, (blk, blk), 1)
    for d in range(2):
        for bi in range(ts // blk):
            for bj in range(ts // blk):
                off = (bj - bi) * blk + ts * (1 - d)
                if off + (blk - 1) < 0:
                    tile = jnp.full((blk, blk), NEG_INF, F32)
                elif off - (blk - 1) >= MAX_DISTANCE:
                    tile = jnp.zeros((blk, blk), F32)
                else:
                    n = jq - ik + off
                    tile = jnp.where(n >= 0, _bias_of(jnp.maximum(n, 0), rb_ref, h) * LOG2E, NEG_INF)
                o_ref[0, d, bi * blk:(bi + 1) * blk, bj * blk:(bj + 1) * blk] = tile


def _bias_tiles(rel_bias, *, heads, ts):
    return pl.pallas_call(
        functools.partial(_bias_kernel, ts=ts),
        grid=(heads,),
        in_specs=[pl.BlockSpec(memory_space=pltpu.SMEM)],
        out_specs=pl.BlockSpec((1, 2, ts, ts), lambda h: (h, 0, 0, 0)),
        out_shape=jax.ShapeDtypeStruct((heads, 2, ts, ts), F32),
        compiler_params=_params("arbitrary"),
        name="bias_tiles",
    )(rel_bias)


def _lambda(lq1, lk1, lq2, lk2, lambda_init):
    s1 = jnp.sum(lq1[...] * lk1[...], axis=-1, keepdims=True)
    s2 = jnp.sum(lq2[...] * lk2[...], axis=-1, keepdims=True)
    return jnp.exp(s1) - jnp.exp(s2) + lambda_init


def _attn_kernel(qt_ref, k_ref, vt_ref, bias_ref, g_ref, lq1, lk1, lq2, lk2, o_ref,
                 qs_ref, m_ref, l_ref, acc_ref, sa_ref, sb_ref, *, ts, lambda_init):
    qi = pl.program_id(2)
    d = ATT_HEAD_DIM
    qt = qt_ref[0]
    row = lax.broadcasted_iota(jnp.int32, qt.shape, 0)
    zero = jnp.zeros_like(qt)
    qs_ref[:, :ts] = jnp.where(row < d, qt, zero)
    qs_ref[:, ts:] = jnp.where(row >= d, qt, zero)
    m_ref[...] = jnp.full(m_ref.shape, NEG_INF, F32)
    l_ref[...] = jnp.zeros(l_ref.shape, F32)
    acc_ref[...] = jnp.zeros(acc_ref.shape, F32)

    def scores(ki, bias):
        kblk = k_ref[pl.ds(pl.multiple_of(ki * ts, ts), ts), :]
        s = jnp.dot(kblk, qs_ref[...], preferred_element_type=F32)
        if bias is not None:
            s = s + jnp.concatenate([bias, bias], axis=1)
        return s

    def update(s_ref, ki):
        s = s_ref[...]
        m_prev = m_ref[...]
        m_new = jnp.maximum(m_prev, jnp.max(s, axis=0, keepdims=True))
        alpha = jnp.exp2(m_prev - m_new)
        p = jnp.exp2(s - m_new)
        l_ref[...] = alpha * l_ref[...] + jnp.sum(p, axis=0, keepdims=True)
        acc_ref[...] = alpha * acc_ref[...] + jnp.dot(vt_ref[ki], p.astype(BF16), preferred_element_type=F32)
        m_ref[...] = m_new

    n_far = jnp.maximum(qi - 1, 0)
    sa_ref[...] = scores(qi, bias_ref[0, 1])

    @pl.when(qi > 0)
    def _():
        sb_ref[...] = scores(qi - 1, bias_ref[0, 0])
        update(sa_ref, qi)

    def far_pair(jj, carry):
        ka = qi - 2 - 2 * jj
        sa_ref[...] = scores(ka, None)
        update(sb_ref, ka + 1)
        sb_ref[...] = scores(ka - 1, None)
        update(sa_ref, ka)
        return carry

    lax.fori_loop(0, n_far // 2, far_pair, 0)

    @pl.when(n_far % 2 == 1)
    def _():
        sa_ref[...] = scores(0, None)
        update(sb_ref, 1)

    @pl.when(qi % 2 == 0)
    def _():
        update(sa_ref, 0)

    @pl.when(qi % 2 == 1)
    def _():
        update(sb_ref, 0)

    lam = _lambda(lq1, lk1, lq2, lk2, lambda_init)
    inv_l = 1.0 / l_ref[...]
    acc = acc_ref[...]
    o = acc[:, :ts] * inv_l[:, :ts] - lam * (acc[:, ts:] * inv_l[:, ts:])
    o = o * lax.rsqrt(jnp.mean(o * o, axis=0, keepdims=True) + LN_EPS)
    o = o * g_ref[...] * (1.0 - lambda_init)
    o_ref[...] = o.T.astype(o_ref.dtype)


def _attn_prompt(qt, kb, vt, bias, subln_g, lams, *, batch, seq, heads, ts, lambda_init):
    t, d_attn = kb.shape
    hd = 2 * ATT_HEAD_DIM
    nq = seq // ts
    lam_spec = _full((1, ATT_HEAD_DIM))
    return pl.pallas_call(
        functools.partial(_attn_kernel, ts=ts, lambda_init=lambda_init),
        grid=(batch, heads, nq),
        in_specs=[pl.BlockSpec((1, hd, ts), lambda b, h, q: (b * nq + q, h, 0)),
                  pl.BlockSpec((seq, hd), lambda b, h, q: (b, h)),
                  pl.BlockSpec((nq, hd, ts), lambda b, h, q: (b, h, 0)),
                  pl.BlockSpec((1, 2, ts, ts), lambda b, h, q: (h, 0, 0, 0)),
                  _full((hd, 1)), lam_spec, lam_spec, lam_spec, lam_spec],
        out_specs=pl.BlockSpec((ts, hd), lambda b, h, q: (b * nq + q, h)),
        out_shape=jax.ShapeDtypeStruct((t, d_attn), BF16),
        scratch_shapes=[pltpu.VMEM((hd, 2 * ts), BF16), pltpu.VMEM((1, 2 * ts), F32),
                        pltpu.VMEM((1, 2 * ts), F32), pltpu.VMEM((hd, 2 * ts), F32),
                        pltpu.VMEM((ts, 2 * ts), F32), pltpu.VMEM((ts, 2 * ts), F32)],
        compiler_params=_params("arbitrary", "arbitrary", "arbitrary"),
        name="attn_prompt",
    )(qt, kb, vt, bias, subln_g.reshape(hd, 1), *lams)


def _decode_kernel(pt_ref, rb_ref, q_ref, kn_ref, vn_ref, g_ref, lq1, lk1, lq2, lk2, *rest,
                   n_pages, page, heads, lambda_init):
    k_refs = rest[:n_pages]
    v_refs = rest[n_pages:2 * n_pages]
    o_ref = rest[2 * n_pages]
    s_ref, bias_ref = rest[2 * n_pages + 1:]
    d = ATT_HEAD_DIM
    hd = 2 * d
    rows = page * heads
    past = n_pages * rows
    nr = 2 * heads
    log_heads = int(math.log2(heads))
    nt = (((1,), (1,)), ((), ()))

    q4 = q_ref[0] * (d ** -0.5)
    lane = lax.broadcasted_iota(jnp.int32, (nr, hd), 1)
    top = lax.broadcasted_iota(jnp.int32, (nr, hd), 0) < heads
    q8 = jnp.concatenate([q4, q4], axis=0)
    q8 = jnp.where(top == (lane < d), q8, 0.0).astype(BF16)

    def own(n_cols):
        r = lax.broadcasted_iota(jnp.int32, (nr, n_cols), 0)
        c = lax.broadcasted_iota(jnp.int32, (nr, n_cols), 1)
        return (r & (heads - 1)) == (c & (heads - 1))

    @pl.when(pl.program_id(0) == 0)
    def _():
        key = lax.shift_right_logical(lax.broadcasted_iota(jnp.int32, (nr, rows), 1), log_heads)
        rh = lax.broadcasted_iota(jnp.int32, (nr, rows), 0) & (heads - 1)
        rh_tail = lax.broadcasted_iota(jnp.int32, (nr, LANES), 0) & (heads - 1)
        bias = jnp.zeros((nr, rows), F32)
        bias_new = jnp.zeros((nr, LANES), F32)
        for h in range(heads):
            bias = jnp.where(rh == h, _bias_of(page - key, rb_ref, h), bias)
            bias_new = jnp.where(rh_tail == h, _bias_of(jnp.zeros((nr, LANES), jnp.int32), rb_ref, h), bias_new)
        bias_ref[:, 0:rows] = bias
        bias_ref[:, rows:rows + LANES] = bias_new

    own_page = own(rows)
    for j in range(n_pages):
        sj = lax.dot_general(q8, k_refs[j][0].astype(BF16), nt, preferred_element_type=F32)
        if j == n_pages - 1:
            sj = sj + bias_ref[:, 0:rows]
        s_ref[:, j * rows:(j + 1) * rows] = jnp.where(own_page, sj, NEG_INF)
    kn = jnp.concatenate([kn_ref[0], jnp.zeros((LANES - heads, hd), F32)], axis=0).astype(BF16)
    s_new = lax.dot_general(q8, kn, nt, preferred_element_type=F32)
    tail_col = lax.broadcasted_iota(jnp.int32, (nr, LANES), 1)
    s_ref[:, past:past + LANES] = jnp.where(own(LANES) & (tail_col < heads),
                                            s_new + bias_ref[:, rows:rows + LANES], NEG_INF)

    s = s_ref[...]
    m = jnp.max(s, axis=-1, keepdims=True)
    p = jnp.exp(s - m)
    l = jnp.sum(p, axis=-1, keepdims=True)
    pb = p.astype(BF16)
    out = jnp.zeros((nr, hd), F32)
    for j in range(n_pages):
        out = out + jnp.dot(pb[:, j * rows:(j + 1) * rows], v_refs[j][0].astype(BF16), preferred_element_type=F32)
    vn = jnp.concatenate([vn_ref[0], jnp.zeros((LANES - heads, hd), F32)], axis=0).astype(BF16)
    out = (out + jnp.dot(pb[:, past:past + LANES], vn, preferred_element_type=F32)) / l
    lam = _lambda(lq1, lk1, lq2, lk2, lambda_init)
    o = out[0:heads, :] - lam * out[heads:nr, :]
    o = o * lax.rsqrt(jnp.mean(o * o, axis=-1, keepdims=True) + LN_EPS)
    o_ref[0] = (o * g_ref[...] * (1.0 - lambda_init)).astype(o_ref.dtype)


def _attn_sample(page_table, rel_bias, q, k_new, v_new, cache_k, cache_v, subln_g, lams, *, heads, lambda_init):
    nb, n_pages = page_table.shape
    n_pool, rows, hd = cache_k.shape
    page = rows // heads
    assert heads & (heads - 1) == 0 and 2 * heads <= 8
    tok = lambda a: a.reshape(nb, heads, hd)
    tok_spec = pl.BlockSpec((1, heads, hd), lambda b, pt: (b, 0, 0))
    lam_spec = pl.BlockSpec((1, ATT_HEAD_DIM), lambda b, pt: (0, 0))
    page_specs = [pl.BlockSpec((1, rows, hd), lambda b, pt, j=j: (pt[b, j], 0, 0)) for j in range(n_pages)]
    grid_spec = pltpu.PrefetchScalarGridSpec(
        num_scalar_prefetch=1,
        grid=(nb,),
        in_specs=[pl.BlockSpec(memory_space=pltpu.SMEM), tok_spec, tok_spec, tok_spec,
                  pl.BlockSpec((1, hd), lambda b, pt: (0, 0)),
                  lam_spec, lam_spec, lam_spec, lam_spec] + page_specs + page_specs,
        out_specs=tok_spec,
        scratch_shapes=[pltpu.VMEM((2 * heads, n_pages * rows + LANES), F32),
                        pltpu.VMEM((2 * heads, rows + LANES), F32)],
    )
    out = pl.pallas_call(
        functools.partial(_decode_kernel, n_pages=n_pages, page=page, heads=heads, lambda_init=lambda_init),
        grid_spec=grid_spec,
        out_shape=jax.ShapeDtypeStruct((nb, heads, hd), BF16),
        compiler_params=_params("arbitrary"),
        name="attn_sample",
    )(page_table, rel_bias, tok(q), tok(k_new), tok(v_new), subln_g.reshape(1, hd), *lams,
      *([cache_k] * n_pages), *([cache_v] * n_pages))
    return out.reshape(nb, heads * hd)


def _mix_kernel(x_ref, cy_ref, ay_ref, mod_ref, wo_ref, bo_ref, g1_ref, b1_ref, wq_ref, sk_ref,
                x1_ref, h2t_ref, st_ref, *, per_token_mod, alpha, d_conv):
    if per_token_mod:
        gate1, sh2, sc2 = mod_ref[2], mod_ref[3], mod_ref[4]
    else:
        gate1, sh2, sc2 = mod_ref[2, 0], mod_ref[3, 0], mod_ref[4, 0]
    mix = (jnp.dot(cy_ref[...], wo_ref[:d_conv, :], preferred_element_type=F32)
           + jnp.dot(ay_ref[...], wo_ref[d_conv:, :], preferred_element_type=F32) + bo_ref[...])
    x1 = _ln(alpha * x_ref[...] + gate1 * mix) * g1_ref[...] + b1_ref[...]
    x1_ref[...] = x1
    h2 = _ln(x1) * (1.0 + sc2) + sh2
    h2b = h2.astype(BF16)
    h2t_ref[...] = h2.T.astype(BF16)
    qh = jnp.dot(h2b, wq_ref[...], preferred_element_type=F32).astype(BF16)
    nk = sk_ref.shape[2]
    for hc in range(sk_ref.shape[0]):
        st_ref[hc] = lax.dot_general(sk_ref[hc], qh[:, hc * nk:(hc + 1) * nk], (((1,), (1,)), ((), ())),
                                     preferred_element_type=F32)


def _mix(x, cy, ay, mod, w_out, b_out, ln_g, ln_b, w_query, sub_keys, *, tm, rows_per_mod, alpha):
    t, d = x.shape
    d_conv = cy.shape[1]
    d_attn = ay.shape[1]
    n_hc, n_keys, half = sub_keys.shape
    per_token = rows_per_mod is None
    if per_token:
        mod_spec = pl.BlockSpec((6, tm, d), lambda i: (0, i, 0))
    else:
        mod_spec = pl.BlockSpec((6, 1, 1, d), lambda i: (0, (i * tm) // rows_per_mod, 0, 0))
    vec = lambda a: a.reshape(1, d)
    return pl.pallas_call(
        functools.partial(_mix_kernel, per_token_mod=per_token, alpha=alpha, d_conv=d_conv),
        grid=(t // tm,),
        in_specs=[pl.BlockSpec((tm, d), lambda i: (i, 0)),
                  pl.BlockSpec((tm, d_conv), lambda i: (i, 0)),
                  pl.BlockSpec((tm, d_attn), lambda i: (i, 0)),
                  mod_spec, _full(w_out.shape), _full((1, d)), _full((1, d)), _full((1, d)),
                  _full(w_query.shape), _full(sub_keys.shape)],
        out_specs=[pl.BlockSpec((tm, d), lambda i: (i, 0)),
                   pl.BlockSpec((d, tm), lambda i: (0, i)),
                   pl.BlockSpec((n_hc, n_keys, tm), lambda i: (0, 0, i))],
        out_shape=[jax.ShapeDtypeStruct((t, d), F32),
                   jax.ShapeDtypeStruct((d, t), BF16),
                   jax.ShapeDtypeStruct((n_hc, n_keys, t), F32)],
        compiler_params=_params("arbitrary"),
        name="mix_sample" if per_token else "mix_prompt",
    )(x, cy, ay, mod, w_out, vec(b_out), vec(ln_g), vec(ln_b), w_query, sub_keys)


def _top_values(s, with_rank):
    work = s
    rank = jnp.full(s.shape, float(PEER_TOPK), F32) if with_rank else None
    vals = []
    for r in range(PEER_TOPK):
        m = jnp.max(work, axis=0, keepdims=True)
        hit = work == m
        if with_rank:
            rank = jnp.where(hit, float(r), rank)
        work = jnp.where(hit, -jnp.inf, work)
        vals.append(m)
    return rank, vals


def _route_kernel(s_ref, r2_ref, e2_ref, n1_ref, e1_ref):
    k = PEER_TOPK
    s1 = s_ref[0]
    s2 = s_ref[1]
    _, v1 = _top_values(s1, with_rank=False)
    rank2, v2 = _top_values(s2, with_rank=True)
    rowk = lax.broadcasted_iota(jnp.int32, (k,) + s1.shape[1:], 0)

    def stack(vals):
        out = jnp.zeros(rowk.shape, F32)
        for r in range(k):
            out = jnp.where(rowk == r, vals[r], out)
        return out

    v1m = stack(v1)
    v2m = stack(v2)
    half = k // 2
    row_half = lax.broadcasted_iota(jnp.int32, (half,) + s1.shape[1:], 0)
    cands = [v1[0] + v2m]
    for a in range(1, half):
        cands.append(jnp.where(row_half < k // (a + 1), v1[a] + v2m[0:half], -jnp.inf))
    cands.append(v1m[half:k] + v2[0])
    work = jnp.concatenate(cands, axis=0)
    thr = None
    for _ in range(k):
        thr = jnp.max(work, axis=0, keepdims=True)
        work = jnp.where(work == thr, -jnp.inf, work)
    e2top = jnp.exp(v2m - v2[0])
    z = jnp.zeros_like(thr)
    n1 = jnp.zeros(s1.shape, F32)
    for a in range(k):
        sel = (v1[a] + v2m) >= thr
        cnt = jnp.sum(jnp.where(sel, 1.0, 0.0), axis=0, keepdims=True)
        z = z + jnp.exp(v1[a] - v1[0]) * jnp.sum(jnp.where(sel, e2top, 0.0), axis=0, keepdims=True)
        n1 = jnp.where(s1 == v1[a], cnt, n1)
    e1 = jnp.where(s1 >= v1[k - 1], jnp.exp(s1 - v1[0]) / z, 0.0)
    e2 = jnp.where(rank2 < float(k), jnp.exp(s2 - v2[0]), 0.0)
    r2_ref[0] = rank2.astype(r2_ref.dtype)
    e2_ref[0] = e2.astype(e2_ref.dtype)
    n1_ref[0] = _bf16_pair(n1)
    e1_ref[0] = _bf16_pair(e1)


def _bf16_pair(x):
    bits = pltpu.bitcast(x.astype(BF16).astype(F32), jnp.uint32)
    return bits | lax.shift_right_logical(bits, jnp.uint32(16))


def _route(st, *, tl):
    n_hc, n_keys, t = st.shape
    heads = n_hc // 2
    out = lambda dt: jax.ShapeDtypeStruct((heads, n_keys, t), dt)
    spec = pl.BlockSpec((1, n_keys, tl), lambda i, h: (h, 0, i))
    return pl.pallas_call(
        _route_kernel,
        grid=(t // tl, heads),
        in_specs=[pl.BlockSpec((2, n_keys, tl), lambda i, h: (h, 0, i))],
        out_specs=[spec] * 4,
        out_shape=[out(BF16), out(BF16), out(jnp.uint32), out(jnp.uint32)],
        compiler_params=_params("arbitrary", "arbitrary"),
        name="route",
    )(st)


GELU_C1 = math.sqrt(2.0 / math.pi)
GELU_C2 = GELU_C1 * 0.044715


def _gelu_tanh(x):
    hx = 0.5 * x
    return hx + hx * jnp.tanh(x * (GELU_C1 + GELU_C2 * (x * x)))


def _packed_rows(words):
    return pltpu.bitcast(jnp.broadcast_to(words, (SUBLANES, words.shape[1])), BF16)


def _peer_kernel(h2t_ref, u_ref, vt_ref, r2_ref, e2_ref, n1_ref, e1_ref, x1_ref, mod_ref, g_ref, b_ref, y_ref,
                 at0_ref, at1_ref, zt0_ref, zt1_ref, acc_ref, h2s_ref, *, per_token_mod, alpha, n_keys, slab, n_blocks):
    t = pl.program_id(1)
    ek, tq = at0_ref.shape
    heads = r2_ref.shape[0]
    sub = PACKED_ROWS
    col = min(PEER_MASK_LANES, tq)
    groups = slab // n_keys
    n_slabs = ek // slab

    def hidden(s, at_ref):
        rows = pl.ds(pl.multiple_of(s * slab, slab), slab)
        u = pltpu.bitcast(u_ref[pl.ds(pl.multiple_of(s * (slab // 2), slab // 2), slab // 2), :], BF16)
        at_ref[rows, :] = jnp.dot(u, h2s_ref[...], preferred_element_type=F32)

    def mask(s, at_ref, zt_ref):
        base = pl.multiple_of(s * slab, slab)
        for g in range(groups):
            i1 = (t - 1) * (ek // n_keys) + s * groups + g
            n1w = [n1_ref[h, pl.ds(i1, 1), :] for h in range(heads)]
            e1w = [e1_ref[h, pl.ds(i1, 1), :] for h in range(heads)]
            for c in range(tq // col):
                tok = slice(c * col, (c + 1) * col)
                n1b = [_packed_rows(n1w[h][:, tok]) for h in range(heads)]
                e1b = [_packed_rows(e1w[h][:, tok]) for h in range(heads)]
                for k in range(n_keys // sub):
                    rows = pl.ds(base + (g * n_keys + k * sub), sub)
                    keys = slice(k * sub, (k + 1) * sub)
                    act = _gelu_tanh(at_ref[rows, tok].astype(BF16))
                    w = jnp.zeros((sub, col), BF16)
                    for h in range(heads):
                        w = jnp.where(r2_ref[h, keys, tok] < n1b[h], w + e2_ref[h, keys, tok] * e1b[h], w)
                    zt_ref[rows, tok] = w * act

    def project(piece, zt_ref):
        rows = pl.ds(pl.multiple_of(piece * slab, slab), slab)
        vt = pltpu.bitcast(vt_ref[pl.ds(pl.multiple_of(piece * (slab // 2), slab // 2), slab // 2), :], BF16)
        acc_ref[rows, :] += jnp.dot(vt, zt_ref[...], preferred_element_type=F32)

    n_pieces = acc_ref.shape[0] // slab
    per_iter = n_pieces // n_slabs

    def step(parity, with_hidden=True, with_project=True):
        at = (at0_ref, at1_ref)
        zt = (zt0_ref, zt1_ref)
        cur, other = parity, 1 - parity
        for s in range(n_slabs):
            if with_hidden:
                hidden(s, at[other])
            mask(s, at[cur], zt[cur])
            if with_project:
                for i in range(per_iter):
                    project(s * per_iter + i, zt[other])

    @pl.when(t == 0)
    def _():
        acc_ref[...] = jnp.zeros(acc_ref.shape, F32)
        h2s_ref[...] = h2t_ref[...]

        def body(s, carry):
            hidden(s, at0_ref)
            return carry

        lax.fori_loop(0, n_slabs, body, 0)

    @pl.when(t == 1)
    def _():
        step(0, with_project=False)

    @pl.when((t >= 2) & (t < n_blocks) & (t % 2 == 1))
    def _():
        step(0)

    @pl.when((t >= 2) & (t < n_blocks) & (t % 2 == 0))
    def _():
        step(1)

    @pl.when(t == n_blocks)
    def _():
        step((n_blocks - 1) % 2, with_hidden=False)

    @pl.when(t == n_blocks + 1)
    def _():
        last = zt0_ref if (n_blocks - 1) % 2 == 0 else zt1_ref

        def body(p, carry):
            project(p, last)
            return carry

        lax.fori_loop(0, n_pieces, body, 0)
        gate2 = mod_ref[5] if per_token_mod else mod_ref[5, 0]
        ff = acc_ref[...].T
        y_ref[...] = _ln(alpha * x1_ref[...] + gate2 * ff) * g_ref[...] + b_ref[...]


def _peer(h2t, u_b, vt_b, r2, e2, n1, e1, x1, mod, ln_g, ln_b, *, tq, ek, slab, rows_per_mod, alpha):
    d, t = h2t.shape
    ne = 2 * u_b.shape[0]
    heads, n_keys, _ = r2.shape
    n_blocks = ne // ek
    assert n_blocks >= 2
    per_token = rows_per_mod is None
    if per_token:
        mod_spec = pl.BlockSpec((6, tq, d), lambda i, j: (0, i, 0))
    else:
        mod_spec = pl.BlockSpec((6, 1, 1, d), lambda i, j: (0, (i * tq) // rows_per_mod, 0, 0))
    tab = pl.BlockSpec((heads, n_keys, tq), lambda i, j: (0, 0, i))
    vec = lambda a: a.reshape(1, d)
    clamp = lambda b: jnp.clip(b, 0, n_blocks - 1)
    return pl.pallas_call(
        functools.partial(_peer_kernel, per_token_mod=per_token, alpha=alpha, n_keys=n_keys, slab=slab,
                          n_blocks=n_blocks),
        grid=(t // tq, n_blocks + 2),
        in_specs=[pl.BlockSpec((d, tq), lambda i, j: (0, i)),
                  pl.BlockSpec((ek // 2, d), lambda i, j: (clamp(j), 0)),
                  pl.BlockSpec((d // 2, ek), lambda i, j: (0, clamp(j - 2))),
                  tab, tab, tab, tab,
                  pl.BlockSpec((tq, d), lambda i, j: (i, 0)),
                  mod_spec, _full((1, d)), _full((1, d))],
        out_specs=pl.BlockSpec((tq, d), lambda i, j: (i, 0)),
        out_shape=jax.ShapeDtypeStruct((t, d), F32),
        scratch_shapes=[pltpu.VMEM((ek, tq), F32), pltpu.VMEM((ek, tq), F32),
                        pltpu.VMEM((ek, tq), BF16), pltpu.VMEM((ek, tq), BF16), pltpu.VMEM((d, tq), F32),
                        pltpu.VMEM((d, tq), BF16)],
        compiler_params=_params("arbitrary", "arbitrary"),
        name="peer_sample" if per_token else "peer_prompt",
    )(h2t, u_b, vt_b, r2, e2, n1, e1, x1, mod, vec(ln_g), vec(ln_b))


def _pack_kernel(x_ref, o_ref, *, transpose):
    x = x_ref[...]
    if transpose:
        x = x.T
    o_ref[...] = pltpu.bitcast(x.astype(BF16), jnp.uint32)


def _pack_bf16(x, *, rows, transpose):
    n, d = x.shape
    if transpose:
        out_spec, out_shape = pl.BlockSpec((d // 2, rows), lambda i: (0, i)), (d // 2, n)
    else:
        out_spec, out_shape = pl.BlockSpec((rows // 2, d), lambda i: (i, 0)), (n // 2, d)
    return pl.pallas_call(
        functools.partial(_pack_kernel, transpose=transpose),
        grid=(n // rows,),
        in_specs=[pl.BlockSpec((rows, d), lambda i: (i, 0))],
        out_specs=out_spec,
        out_shape=jax.ShapeDtypeStruct(out_shape, jnp.uint32),
        compiler_params=_params("arbitrary"),
        name="pack_vt" if transpose else "pack_u",
    )(x)


def kernel(x_prompt, x_sample, cache_k, cache_v, state_conv, page_table, c_prompt, c_sample, w_ada, b_ada, w_in, b_in, conv_w, conv_b, conv_ln_g, conv_ln_b, lambda_q1, lambda_k1, lambda_q2, lambda_k2, attn_subln_g, rel_bias, w_out, b_out, ln1_g, ln1_b, peer_w_query, peer_sub_keys, peer_u, peer_v, ln2_g, ln2_b):
    batch, seq, d = x_prompt.shape
    nb, dec_seq, _ = x_sample.shape
    depth = w_ada.shape[0]
    assert depth == 1 and dec_seq == 1
    heads = cache_k.shape[3]
    hd = cache_k.shape[4]
    assert hd == 2 * ATT_HEAD_DIM
    d_attn = heads * hd
    d_conv = conv_w.shape[2]
    hist = conv_w.shape[1] - 1
    assert hist <= CONV_PAD - 1
    page = cache_k.shape[2]
    n_keys = peer_sub_keys.shape[3]
    ne = peer_u.shape[1]
    alpha = (2 * depth) ** 0.25
    lambda_init = 0.8 - 0.6 * math.exp(-0.3 * 0)
    t_p = batch * seq
    ts = min(TOKEN_TILE, seq)
    tq_peer = min(PEER_TOKEN_TILE, seq)
    ek = min(PEER_EXPERT_TILE, ne)
    assert seq % ts == 0 and nb % LANES == 0 and page >= MAX_DISTANCE

    l = 0
    n_c = batch + nb
    n_c_pad = -(-n_c // SUBLANES) * SUBLANES
    c_all = jnp.concatenate([c_prompt, c_sample, jnp.zeros((n_c_pad - n_c, d), F32)], axis=0)
    mod = _ada(c_all, w_ada[l], b_ada[l])
    mod_p = mod[:, :batch].reshape(6, batch, 1, d)
    mod_s = mod[:, batch:n_c]

    w_in_b = w_in[l].astype(BF16)
    b_in_r = b_in[l].reshape(1, -1)
    w_out_b = w_out[l].astype(BF16)
    wq_b = peer_w_query[l].astype(BF16)
    sk_b = peer_sub_keys[l].reshape(-1, n_keys, peer_sub_keys.shape[-1]).astype(BF16)
    u_b = _pack_bf16(peer_u[l], rows=min(512, ne), transpose=False)
    vt_b = _pack_bf16(peer_v[l], rows=min(512, ne), transpose=True)
    lams = [a[l].reshape(1, -1) for a in (lambda_q1, lambda_k1, lambda_q2, lambda_k2)]
    subln_g = attn_subln_g[l]

    xp = x_prompt.reshape(t_p, d)
    glu_p, k_p, v_p, kb_p, qt_p, vt_p = _inproj(xp, mod_p, w_in_b, b_in_r, tm=ts, rows_per_mod=seq,
                                                d_conv=d_conv, d_attn=d_attn)
    cy_p = _conv_prompt(glu_p, conv_w[l], conv_b[l], conv_ln_g[l], conv_ln_b[l], batch=batch, seq=seq, ts=ts)
    bias = _bias_tiles(rel_bias, heads=heads, ts=ts)
    ay_p = _attn_prompt(qt_p, kb_p, vt_p, bias, subln_g, lams, batch=batch, seq=seq, heads=heads, ts=ts,
                        lambda_init=lambda_init)
    x1_p, h2t_p, st_p = _mix(xp, cy_p, ay_p, mod_p, w_out_b, b_out[l], ln1_g[l], ln1_b[l], wq_b, sk_b,
                             tm=ts, rows_per_mod=seq, alpha=alpha)
    r2, e2, n1, e1 = _route(st_p, tl=min(ROUTE_TOKEN_TILE, t_p))
    y_p = _peer(h2t_p, u_b, vt_b, r2, e2, n1, e1, x1_p, mod_p, ln2_g[l], ln2_b[l],
                tq=tq_peer, ek=ek, slab=min(PEER_EXPERT_SLAB, ek), rows_per_mod=seq, alpha=alpha)

    xs = x_sample.reshape(nb, d)
    glu_s, k_s, v_s, q_s = _inproj(xs, mod_s, w_in_b, b_in_r, tm=nb, rows_per_mod=None,
                                   d_conv=d_conv, d_attn=d_attn)
    state = state_conv[l]
    cy_s = _conv_sample(jnp.swapaxes(state, 0, 1), glu_s, conv_w[l], conv_b[l], conv_ln_g[l], conv_ln_b[l])
    n_pool = cache_k.shape[1]
    pool_rows = lambda a: a.reshape(depth * n_pool, page * heads, hd)
    ay_s = _attn_sample(page_table + l * n_pool, rel_bias, q_s, k_s, v_s, pool_rows(cache_k), pool_rows(cache_v),
                        subln_g, lams, heads=heads, lambda_init=lambda_init)
    x1_s, h2t_s, st_s = _mix(xs, cy_s, ay_s, mod_s, w_out_b, b_out[l], ln1_g[l], ln1_b[l], wq_b, sk_b,
                             tm=nb, rows_per_mod=None, alpha=alpha)
    r2s, e2s, n1s, e1s = _route(st_s, tl=nb)
    y_s = _peer(h2t_s, u_b, vt_b, r2s, e2s, n1s, e1s, x1_s, mod_s, ln2_g[l], ln2_b[l],
                tq=nb, ek=ek, slab=min(PEER_EXPERT_SLAB, ek), rows_per_mod=None, alpha=alpha)

    kv_p = (depth, batch, seq, heads, hd)
    kv_s = (depth, nb, dec_seq, heads, hd)
    conv_p = glu_p.reshape(batch, seq, d_conv)[:, seq - hist:][None]
    conv_s = jnp.concatenate([state[:, 1:], glu_s[:, None, :]], axis=1)[None]
    return (y_p.reshape(batch, seq, d), y_s.reshape(nb, dec_seq, d),
            k_p.reshape(kv_p), v_p.reshape(kv_p), conv_p,
            k_s.reshape(kv_s), v_s.reshape(kv_s), conv_s)
```
